```python
import numpy as np
import jax
import jax.numpy as jnp
from jax import lax

D_MODEL = 1024
BATCH = 16
SEQ = 2048
DEPTH = 2

HEAD_DIM = 64
ROPE_DIM = HEAD_DIM // 4
ROPE_THETA = 500000.0
NORM_EPS = 1e-6
ATTN_SCALE = HEAD_DIM ** -0.5
Q_BLOCK = 32

NSA_HEADS = 8
NSA_KV_GROUPS = 2
NSA_GROUP_SIZE = NSA_HEADS // NSA_KV_GROUPS
CMP_BLOCK = 32
CMP_STRIDE = 16
CMP_HIDDEN = 2 * HEAD_DIM
SLC_BLOCK = 64
N_SLC = 16
N_LOCAL_BLOCKS = 2
SEL_FORCE_SCORE = 1e4
WINDOW = 512

DSA_HEADS = 8
IDX_HEADS = 8
IDX_DIM = 64
IDX_WEIGHT_SCALE = (IDX_HEADS * IDX_DIM) ** -0.5
DSA_TOPK_MAX = 256

D_FF = -(-8 * D_MODEL // (3 * 256)) * 256

NSA_WIDTH = NSA_HEADS * HEAD_DIM
DSA_WIDTH = DSA_HEADS * HEAD_DIM
KV_WIDTH = NSA_KV_GROUPS * HEAD_DIM
IN_SIZES = (
    NSA_WIDTH,
    KV_WIDTH, KV_WIDTH,
    KV_WIDTH, KV_WIDTH,
    KV_WIDTH, KV_WIDTH,
    3 * NSA_HEADS,
    DSA_WIDTH, DSA_WIDTH, DSA_WIDTH,
    IDX_HEADS * IDX_DIM, IDX_DIM, IDX_HEADS,
    D_MODEL, D_MODEL,
)
D_IN = sum(IN_SIZES)

kernel_name = 'nsa_dsa_gated_hybrid_block'


def _rmsnorm(x, g):
    xf = x.astype(jnp.float32)
    y = xf * lax.rsqrt(jnp.mean(xf * xf, axis=-1, keepdims=True) + NORM_EPS)
    return (y * g.astype(jnp.float32)).astype(x.dtype)


def _rope_tables(seq_len):
    pos = jnp.arange(seq_len, dtype=jnp.float32)
    inv_freq = ROPE_THETA ** (-jnp.arange(0, ROPE_DIM, 2, dtype=jnp.float32) / ROPE_DIM)
    ang = pos[:, None] * inv_freq[None, :]
    return jnp.cos(ang), jnp.sin(ang)


def _apply_rope(x, cos, sin):
    half = ROPE_DIM // 2
    bshape = (cos.shape[0],) + (1,) * (x.ndim - 3) + (half,)
    c = cos.reshape(bshape).astype(x.dtype)
    s = sin.reshape(bshape).astype(x.dtype)
    x1 = x[..., :half]
    x2 = x[..., half:ROPE_DIM]
    return jnp.concatenate([x1 * c - x2 * s, x2 * c + x1 * s, x[..., ROPE_DIM:]], axis=-1)


def _masked_softmax(scores, mask):
    s = jnp.where(mask, scores.astype(jnp.float32), -1e30)
    s = s - jnp.max(s, axis=-1, keepdims=True)
    p = jnp.where(mask, jnp.exp(s), 0.0)
    return p / jnp.maximum(jnp.sum(p, axis=-1, keepdims=True), 1e-30)


def _compress(kv, pos_emb, w1, b1, w2, b2):
    bsz, seq_len, groups, hd = kv.shape
    n_cmp = (seq_len - CMP_BLOCK) // CMP_STRIDE + 1
    idx = np.arange(n_cmp)[:, None] * CMP_STRIDE + np.arange(CMP_BLOCK)[None, :]
    blocks = kv[:, idx] + pos_emb[:, None, :]
    blocks = blocks.transpose(0, 1, 3, 2, 4).reshape(bsz, n_cmp, groups, CMP_BLOCK * hd)
    hidden = jax.nn.gelu(blocks @ w1 + b1)
    return hidden @ w2 + b2


def _cmp_to_slc_overlap(n_cmp, n_slc):
    c_start = np.arange(n_cmp)[:, None] * CMP_STRIDE
    s_start = np.arange(n_slc)[None, :] * SLC_BLOCK
    ov = np.minimum(c_start + CMP_BLOCK, s_start + SLC_BLOCK) - np.maximum(c_start, s_start)
    return jnp.asarray(np.clip(ov, 0, None) / CMP_BLOCK, dtype=jnp.float32)


def _nsa_mixer(q, k_cmp, v_cmp, k_slc, v_slc, k_win, v_win, gate_logits, cos, sin,
               cmpk_pos, cmpk_w1, cmpk_b1, cmpk_w2, cmpk_b2,
               cmpv_pos, cmpv_w1, cmpv_b1, cmpv_w2, cmpv_b2):
    bsz, seq_len = q.shape[:2]
    G, R, hd = NSA_KV_GROUPS, NSA_GROUP_SIZE, HEAD_DIM
    kv_shape = (bsz, seq_len, G, hd)
    q = _apply_rope(q.reshape(bsz, seq_len, NSA_HEADS, hd), cos, sin)
    k_cmp = _apply_rope(k_cmp.reshape(kv_shape), cos, sin)
    k_slc = _apply_rope(k_slc.reshape(kv_shape), cos, sin)
    k_win = _apply_rope(k_win.reshape(kv_shape), cos, sin)
    v_slc = v_slc.reshape(kv_shape)
    v_win = v_win.reshape(kv_shape)

    kc = _compress(k_cmp, cmpk_pos, cmpk_w1, cmpk_b1, cmpk_w2, cmpk_b2)
    vc = _compress(v_cmp.reshape(kv_shape), cmpv_pos, cmpv_w1, cmpv_b1, cmpv_w2, cmpv_b2)
    n_cmp = kc.shape[1]
    n_slc = seq_len // SLC_BLOCK
    n_sel = min(N_SLC, n_slc)
    nqb = seq_len // Q_BLOCK
    cmp_end = jnp.asarray(np.arange(n_cmp) * CMP_STRIDE + CMP_BLOCK - 1)
    overlap = _cmp_to_slc_overlap(n_cmp, n_slc)
    blk_ids = jnp.arange(n_slc)

    ks_blk = k_slc.reshape(bsz, n_slc, SLC_BLOCK, G, hd).transpose(0, 3, 1, 2, 4)
    vs_blk = v_slc.reshape(bsz, n_slc, SLC_BLOCK, G, hd).transpose(0, 3, 1, 2, 4)
    kw_pad = jnp.pad(k_win, ((0, 0), (WINDOW, 0), (0, 0), (0, 0)))
    vw_pad = jnp.pad(v_win, ((0, 0), (WINDOW, 0), (0, 0), (0, 0)))
    qb = q.reshape(bsz, nqb, Q_BLOCK, G, R, hd).transpose(1, 0, 3, 4, 2, 5)
    gather_blocks = jax.vmap(jax.vmap(lambda blk, ix: blk[ix]))

    def step(args):
        q_i, i = args
        t = i * Q_BLOCK + jnp.arange(Q_BLOCK)
        s_c = jnp.einsum('bgrtd,bcgd->bgrtc', q_i, kc) * ATTN_SCALE
        p_c = _masked_softmax(s_c, cmp_end[None, :] <= t[:, None])
        o_c = jnp.einsum('bgrtc,bcgd->bgrtd', p_c.astype(vc.dtype), vc)
        imp = jnp.einsum('bgrtc,cj->bgtj', p_c, overlap)
        cur = t // SLC_BLOCK
        dist = cur[:, None] - blk_ids[None, :]
        forced = (blk_ids[None, :] == 0) | ((dist >= 0) & (dist < N_LOCAL_BLOCKS))
        admissible = blk_ids[None, :] * SLC_BLOCK <= t[:, None]
        score = jnp.where(forced, SEL_FORCE_SCORE, imp)
        score = jnp.where(admissible, score, -jnp.inf)
        _, sel = lax.top_k(score, n_sel)
        kg = gather_blocks(ks_blk, sel)
        vg = gather_blocks(vs_blk, sel).reshape(bsz, G, Q_BLOCK, n_sel * SLC_BLOCK, hd)
        tok = sel[..., None] * SLC_BLOCK + jnp.arange(SLC_BLOCK)
        mask_s = (tok <= t[None, None, :, None, None]).reshape(bsz, G, 1, Q_BLOCK, n_sel * SLC_BLOCK)
        s_s = jnp.einsum('bgrtd,bgtnkd->bgrtnk', q_i, kg).reshape(bsz, G, R, Q_BLOCK, n_sel * SLC_BLOCK)
        p_s = _masked_softmax(s_s * ATTN_SCALE, mask_s)
        o_s = jnp.einsum('bgrtm,bgtmd->bgrtd', p_s.astype(vg.dtype), vg)
        kw = lax.dynamic_slice_in_dim(kw_pad, i * Q_BLOCK, WINDOW + Q_BLOCK, axis=1)
        vw = lax.dynamic_slice_in_dim(vw_pad, i * Q_BLOCK, WINDOW + Q_BLOCK, axis=1)
        s_pos = i * Q_BLOCK - WINDOW + jnp.arange(WINDOW + Q_BLOCK)
        diff = t[:, None] - s_pos[None, :]
        mask_w = (diff >= 0) & (diff < WINDOW) & (s_pos[None, :] >= 0)
        s_w = jnp.einsum('bgrtd,bsgd->bgrts', q_i, kw) * ATTN_SCALE
        p_w = _masked_softmax(s_w, mask_w)
        o_w = jnp.einsum('bgrts,bsgd->bgrtd', p_w.astype(vw.dtype), vw)
        return o_c, o_s, o_w

    o_c, o_s, o_w = lax.map(step, (qb, jnp.arange(nqb)))

    def unblock(o):
        return o.transpose(1, 0, 4, 2, 3, 5).reshape(bsz, seq_len, NSA_HEADS, hd)

    g = jax.nn.sigmoid(gate_logits).reshape(bsz, seq_len, 3, NSA_HEADS)[..., None]
    out = g[:, :, 0] * unblock(o_c) + g[:, :, 1] * unblock(o_s) + g[:, :, 2] * unblock(o_w)
    return out.reshape(bsz, seq_len, NSA_WIDTH)


def _dsa_mixer(q, k, v, iq, ik, iw, cos, sin):
    bsz, seq_len = q.shape[:2]
    q = _apply_rope(q.reshape(bsz, seq_len, DSA_HEADS, HEAD_DIM), cos, sin)
    k = _apply_rope(k.reshape(bsz, seq_len, DSA_HEADS, HEAD_DIM), cos, sin)
    v = v.reshape(bsz, seq_len, DSA_HEADS, HEAD_DIM)
    iq = _apply_rope(iq.reshape(bsz, seq_len, IDX_HEADS, IDX_DIM), cos, sin)
    ik = _apply_rope(ik, cos, sin)
    iw = iw * IDX_WEIGHT_SCALE
    k_top = min(DSA_TOPK_MAX, seq_len // 4)
    nqb = seq_len // Q_BLOCK
    qb = q.reshape(bsz, nqb, Q_BLOCK, DSA_HEADS, HEAD_DIM).transpose(1, 0, 2, 3, 4)
    iqb = iq.reshape(bsz, nqb, Q_BLOCK, IDX_HEADS, IDX_DIM).transpose(1, 0, 2, 3, 4)
    iwb = iw.reshape(bsz, nqb, Q_BLOCK, IDX_HEADS).transpose(1, 0, 2, 3)
    key_pos = jnp.arange(seq_len)
    gather_tokens = jax.vmap(lambda a, ix: a[ix])

    def step(args):
        q_i, iq_i, iw_i, i = args
        t = i * Q_BLOCK + jnp.arange(Q_BLOCK)
        rel = jax.nn.relu(jnp.einsum('bthd,bsd->bths', iq_i, ik).astype(jnp.float32))
        score = jnp.einsum('bths,bth->bts', rel, iw_i.astype(jnp.float32))
        score = jnp.where(key_pos[None, :] <= t[:, None], score, -jnp.inf)
        _, sel = lax.top_k(score, k_top)
        kg = gather_tokens(k, sel)
        vg = gather_tokens(v, sel)
        valid = (sel <= t[None, :, None])[:, None]
        s = jnp.einsum('bthd,btkhd->bhtk', q_i, kg) * ATTN_SCALE
        p = _masked_softmax(s, valid)
        return jnp.einsum('bhtk,btkhd->bthd', p.astype(vg.dtype), vg)

    o = lax.map(step, (qb, iqb, iwb, jnp.arange(nqb)))
    return o.transpose(1, 0, 2, 3, 4).reshape(bsz, seq_len, DSA_WIDTH)


def _hybrid_mixer(h, cos, sin, w_in,
                  cmpk_pos, cmpk_w1, cmpk_b1, cmpk_w2, cmpk_b2,
                  cmpv_pos, cmpv_w1, cmpv_b1, cmpv_w2, cmpv_b2,
                  w_branch_nsa, w_branch_dsa, w_out):
    proj = h @ w_in
    points = [int(p) for p in np.cumsum(IN_SIZES)[:-1]]
    (nsa_q, k_cmp, v_cmp, k_slc, v_slc, k_win, v_win, nsa_gate,
     dsa_q, dsa_k, dsa_v, idx_q, idx_k, idx_w, gate_nsa, gate_dsa) = jnp.split(proj, points, axis=-1)
    y_nsa = _nsa_mixer(nsa_q, k_cmp, v_cmp, k_slc, v_slc, k_win, v_win, nsa_gate, cos, sin,
                       cmpk_pos, cmpk_w1, cmpk_b1, cmpk_w2, cmpk_b2,
                       cmpv_pos, cmpv_w1, cmpv_b1, cmpv_w2, cmpv_b2) @ w_branch_nsa
    y_dsa = _dsa_mixer(dsa_q, dsa_k, dsa_v, idx_q, idx_k, idx_w, cos, sin) @ w_branch_dsa
    merged = jax.nn.sigmoid(gate_nsa) * y_nsa + jax.nn.sigmoid(gate_dsa) * y_dsa
    return merged @ w_out


def _swiglu(h, w_gate, w_up, w_down):
    return (jax.nn.silu(h @ w_gate) * (h @ w_up)) @ w_down


def setup_inputs(seed: int = 0) -> dict:
    key = jax.random.key(seed)
    ks = jax.random.split(key, 21)

    def nrm(k, shape, scale):
        return jax.random.normal(k, shape, jnp.float32) * scale

    flat = CMP_BLOCK * HEAD_DIM
    return {
        'x': nrm(ks[0], (BATCH, SEQ, D_MODEL), 1.0),
        'attn_norm': 1.0 + nrm(ks[1], (DEPTH, D_MODEL), 0.01),
        'w_in': nrm(ks[2], (DEPTH, D_MODEL, D_IN), D_MODEL ** -0.5),
        'cmpk_pos': nrm(ks[3], (DEPTH, CMP_BLOCK, HEAD_DIM), 0.1),
        'cmpk_w1': nrm(ks[4], (DEPTH, flat, CMP_HIDDEN), flat ** -0.5),
        'cmpk_b1': nrm(ks[5], (DEPTH, CMP_HIDDEN), 0.01),
        'cmpk_w2': nrm(ks[6], (DEPTH, CMP_HIDDEN, HEAD_DIM), CMP_HIDDEN ** -0.5),
        'cmpk_b2': nrm(ks[7], (DEPTH, HEAD_DIM), 0.01),
        'cmpv_pos': nrm(ks[8], (DEPTH, CMP_BLOCK, HEAD_DIM), 0.1),
        'cmpv_w1': nrm(ks[9], (DEPTH, flat, CMP_HIDDEN), flat ** -0.5),
        'cmpv_b1': nrm(ks[10], (DEPTH, CMP_HIDDEN), 0.01),
        'cmpv_w2': nrm(ks[11], (DEPTH, CMP_HIDDEN, HEAD_DIM), CMP_HIDDEN ** -0.5),
        'cmpv_b2': nrm(ks[12], (DEPTH, HEAD_DIM), 0.01),
        'w_branch_nsa': nrm(ks[13], (DEPTH, NSA_WIDTH, D_MODEL), NSA_WIDTH ** -0.5),
        'w_branch_dsa': nrm(ks[14], (DEPTH, DSA_WIDTH, D_MODEL), DSA_WIDTH ** -0.5),
        'w_out': nrm(ks[15], (DEPTH, D_MODEL, D_MODEL), D_MODEL ** -0.5),
        'ffn_norm': 1.0 + nrm(ks[16], (DEPTH, D_MODEL), 0.01),
        'w_ffn_gate': nrm(ks[17], (DEPTH, D_MODEL, D_FF), D_MODEL ** -0.5),
        'w_ffn_up': nrm(ks[18], (DEPTH, D_MODEL, D_FF), D_MODEL ** -0.5),
        'w_ffn_down': nrm(ks[19], (DEPTH, D_FF, D_MODEL), D_FF ** -0.5),
        'final_norm': 1.0 + nrm(ks[20], (D_MODEL,), 0.01),
    }


def reference(x, attn_norm, w_in,
              cmpk_pos, cmpk_w1, cmpk_b1, cmpk_w2, cmpk_b2,
              cmpv_pos, cmpv_w1, cmpv_b1, cmpv_w2, cmpv_b2,
              w_branch_nsa, w_branch_dsa, w_out,
              ffn_norm, w_ffn_gate, w_ffn_up, w_ffn_down, final_norm):
    cos, sin = _rope_tables(x.shape[1])
    for l in range(DEPTH):
        h = _rmsnorm(x, attn_norm[l])
        x = x + _hybrid_mixer(h, cos, sin, w_in[l],
                              cmpk_pos[l], cmpk_w1[l], cmpk_b1[l], cmpk_w2[l], cmpk_b2[l],
                              cmpv_pos[l], cmpv_w1[l], cmpv_b1[l], cmpv_w2[l], cmpv_b2[l],
                              w_branch_nsa[l], w_branch_dsa[l], w_out[l])
        h = _rmsnorm(x, ffn_norm[l])
        x = x + _swiglu(h, w_ffn_gate[l], w_ffn_up[l], w_ffn_down[l])
    return _rmsnorm(x, final_norm)
```

```python
import functools

import numpy as np
import jax
import jax.numpy as jnp
from jax import lax
from jax.experimental import pallas as pl
from jax.experimental.pallas import tpu as pltpu

HEAD_DIM = 64
ROPE_DIM = HEAD_DIM // 4
ROPE_THETA = 500000.0
NORM_EPS = 1e-6
ATTN_SCALE = HEAD_DIM ** -0.5
Q_HEADS = 8
NSA_KV_GROUPS = 2
NSA_GROUP_SIZE = Q_HEADS // NSA_KV_GROUPS
CMP_BLOCK = 32
CMP_STRIDE = 16
CMP_HIDDEN = 2 * HEAD_DIM
SLC_BLOCK = 64
N_SLC = 16
N_LOCAL_BLOCKS = 2
SEL_FORCE_SCORE = 1e4
WINDOW = 512
IDX_HEADS = 8
IDX_DIM = 64
IDX_WEIGHT_SCALE = (IDX_HEADS * IDX_DIM) ** -0.5
DSA_TOPK_MAX = 256
N_GATE = 3 * Q_HEADS
SMALL_W = N_GATE + IDX_HEADS

LANE = 128
V7X_VMEM_LIMIT = 56 * 1024 * 1024

NEG = -1e30
INT_MIN = -(2 ** 31)
KEY_NEG_INF = -2139095041
BF16 = jnp.bfloat16
F32 = jnp.float32

_NT = (((1,), (1,)), ((), ()))


def _dot(a, b):
    return jnp.dot(a, b, preferred_element_type=F32)


def _dot_nt(a, b):
    return lax.dot_general(a, b, _NT, preferred_element_type=F32)


def _const_spec(shape):
    nd = len(shape)
    return pl.BlockSpec(shape, lambda *_: (0,) * nd, pipeline_mode=pl.Buffered(1))


def _rmsnorm(x, g):
    ms = jnp.mean(x * x, axis=-1, keepdims=True)
    return x * lax.rsqrt(ms + NORM_EPS) * g


def _lane_tile(a, width):
    reps = width // LANE
    return a if reps == 1 else jnp.concatenate([a] * reps, axis=1)


_PROJ_SEGS = (
    (512, "ropeq", BF16),
    (128, "rope", F32),
    (128, "plain", F32),
    (256, "rope", BF16),
    (256, "plain", BF16),
    (256, "rope", BF16),
    (256, "plain", BF16),
    (512, "ropeq", BF16),
    (512, "rope", BF16),
    (512, "plain", BF16),
    (512, "rope", BF16),
    (128, "rope", BF16),
    (1024, "sig", F32),
    (1024, "sig", F32),
    (LANE, "small", F32),
)
_PROJ_W = sum(s[0] for s in _PROJ_SEGS)
_PROJ_CHUNK = 512


def _rope(y, c, sa, sb):
    pieces = []
    for j in range(y.shape[1] // LANE):
        yj = y[:, j * LANE:(j + 1) * LANE]
        pieces.append(yj * c + pltpu.roll(yj, ROPE_DIM // 2, 1) * sa
                      + pltpu.roll(yj, LANE - ROPE_DIM // 2, 1) * sb)
    return pieces[0] if len(pieces) == 1 else jnp.concatenate(pieces, axis=1)


def _proj_kernel(x_ref, g_ref, w_ref, c_ref, sa_ref, sb_ref, *out_refs):
    h = _rmsnorm(x_ref[...], g_ref[...]).astype(BF16)
    c, sa, sb = c_ref[...], sa_ref[...], sb_ref[...]
    col = 0
    for (width, kind, _), o_ref in zip(_PROJ_SEGS, out_refs):
        for c0 in range(0, width, _PROJ_CHUNK):
            cw = min(_PROJ_CHUNK, width - c0)
            y = _dot(h, w_ref[:, col + c0:col + c0 + cw])
            if kind in ("rope", "ropeq"):
                y = _rope(y, c, sa, sb)
            if kind == "ropeq":
                y = y * ATTN_SCALE
            if kind == "sig":
                y = jax.nn.sigmoid(y)
            if kind == "small":
                lane = lax.broadcasted_iota(jnp.int32, y.shape, 1)
                y = jnp.where(lane < N_GATE, jax.nn.sigmoid(y), y * IDX_WEIGHT_SCALE)
                o_ref[...] = y[:, :SMALL_W]
            else:
                o_ref[:, c0:c0 + cw] = y.astype(o_ref.dtype)
        col += width


def _permute_w_in(w):
    d = w.shape[0]
    kvw = NSA_KV_GROUPS * HEAD_DIM
    sizes = (Q_HEADS * HEAD_DIM, kvw, kvw, kvw, kvw, kvw, kvw, N_GATE,
             Q_HEADS * HEAD_DIM, Q_HEADS * HEAD_DIM, Q_HEADS * HEAD_DIM,
             IDX_HEADS * IDX_DIM, IDX_DIM, IDX_HEADS, d, d)
    assert sum(sizes) == w.shape[1]
    offs = np.concatenate([[0], np.cumsum(sizes)])
    (nsa_q, k_cmp, v_cmp, k_slc, v_slc, k_win, v_win, nsa_gate, dsa_q, dsa_k, dsa_v,
     idx_q, idx_k, idx_w, gate_nsa, gate_dsa) = [w[:, offs[i]:offs[i + 1]] for i in range(16)]

    def dup(a):
        a = a.reshape(d, NSA_KV_GROUPS, 1, HEAD_DIM)
        return jnp.broadcast_to(a, (d, NSA_KV_GROUPS, 2, HEAD_DIM)).reshape(d, 2 * kvw)

    small = jnp.concatenate([nsa_gate, idx_w, jnp.zeros((d, LANE - SMALL_W), w.dtype)], axis=1)
    cols = [nsa_q, k_cmp, v_cmp, dup(k_slc), dup(v_slc), dup(k_win), dup(v_win),
            dsa_q, dsa_k, dsa_v, idx_q, jnp.concatenate([idx_k, idx_k], axis=1),
            gate_nsa, gate_dsa, small]
    out = jnp.concatenate(cols, axis=1).astype(BF16)
    assert out.shape[1] == _PROJ_W
    return out


def _proj(xf, g, wperm, tables, seq_len, tm):
    t, d = xf.shape
    npos = seq_len // tm
    row = lambda i: (i, 0)
    tab_spec = pl.BlockSpec((tm, LANE), lambda i: (i % npos, 0))
    out_w = [SMALL_W if k == "small" else w for (w, k, _) in _PROJ_SEGS]
    return pl.pallas_call(
        _proj_kernel,
        grid=(t // tm,),
        in_specs=[pl.BlockSpec((tm, d), row), _const_spec((1, d)), _const_spec((d, _PROJ_W)),
                  tab_spec, tab_spec, tab_spec],
        out_specs=[pl.BlockSpec((tm, w), row) for w in out_w],
        out_shape=[jax.ShapeDtypeStruct((t, w), dt) for w, (_, _, dt) in zip(out_w, _PROJ_SEGS)],
        compiler_params=pltpu.CompilerParams(dimension_semantics=("parallel",),
                                             vmem_limit_bytes=V7X_VMEM_LIMIT),
        name="proj",
    )(xf, g.reshape(1, d), wperm, *tables)


def _cmp_kernel(kb_ref, vb_ref, pk_ref, w1k_ref, b1k_ref, w2k_ref, b2k_ref,
                pv_ref, w1v_ref, b1v_ref, w2v_ref, b2v_ref, kc_ref, vc_ref, *, n_cmp):
    def one(xb_ref, pos_ref, w1_ref, b1_ref, w2_ref, b2_ref, o_ref):
        x = xb_ref[0, 0]
        nb, half = x.shape
        a = _dot((x + pos_ref[:, :half]).astype(BF16), w1_ref[:half, :])
        b = _dot((x + pos_ref[:, half:]).astype(BF16), w1_ref[half:, :])
        hid = jax.nn.gelu(a + pltpu.roll(b, nb - 1, 0) + b1_ref[...])
        out = _dot(hid.astype(BF16), w2_ref[...]) + b2_ref[...]
        rows = lax.broadcasted_iota(jnp.int32, out.shape, 0)
        o_ref[0, 0] = jnp.where(rows < n_cmp, out, 0.0).astype(o_ref.dtype)

    one(kb_ref, pk_ref, w1k_ref, b1k_ref, w2k_ref, b2k_ref, kc_ref)
    one(vb_ref, pv_ref, w1v_ref, b1v_ref, w2v_ref, b2v_ref, vc_ref)


def _compress(k_cmp, v_cmp, pk, w1k, b1k, w2k, b2k, pv, w1v, b1v, w2v, b2v, bsz, seq_len):
    g, hd = NSA_KV_GROUPS, HEAD_DIM
    nb = seq_len // CMP_STRIDE
    n_cmp = (seq_len - CMP_BLOCK) // CMP_STRIDE + 1
    flat = CMP_STRIDE * hd

    def strides(a):
        return a.reshape(bsz, nb, CMP_STRIDE, g, hd).transpose(0, 3, 1, 2, 4).reshape(bsz, g, nb, flat)

    def wts(pos, w1, b1, w2, b2):
        return (pos.reshape(1, CMP_BLOCK * hd), w1.astype(BF16), b1.reshape(1, -1),
                jnp.concatenate([w2, w2], axis=1).astype(BF16),
                jnp.concatenate([b2, b2]).reshape(1, -1))

    blk = pl.BlockSpec((1, 1, nb, flat), lambda b, gi: (b, gi, 0, 0))
    wspecs = [_const_spec((1, 2 * flat)), _const_spec((2 * flat, CMP_HIDDEN)), _const_spec((1, CMP_HIDDEN)),
              _const_spec((CMP_HIDDEN, 2 * hd)), _const_spec((1, 2 * hd))]
    ospec = pl.BlockSpec((1, 1, nb, 2 * hd), lambda b, gi: (b, gi, 0, 0))
    oshape = jax.ShapeDtypeStruct((bsz, g, nb, 2 * hd), BF16)
    return pl.pallas_call(
        functools.partial(_cmp_kernel, n_cmp=n_cmp),
        grid=(bsz, g),
        in_specs=[blk, blk] + wspecs + wspecs,
        out_specs=[ospec, ospec],
        out_shape=[oshape, oshape],
        compiler_params=pltpu.CompilerParams(dimension_semantics=("parallel", "parallel"),
                                             vmem_limit_bytes=V7X_VMEM_LIMIT),
        name="compress",
    )(strides(k_cmp), strides(v_cmp), *wts(pk, w1k, b1k, w2k, b2k), *wts(pv, w1v, b1v, w2v, b2v))


def _masked_softmax(s, mask):
    s = jnp.where(mask, s, NEG)
    s = s - jnp.max(s, axis=-1, keepdims=True)
    p = jnp.where(mask, jnp.exp(s), 0.0)
    return p / jnp.maximum(jnp.sum(p, axis=-1, keepdims=True), 1e-30)


def _nsa_kernel(q_ref, kc_ref, vc_ref, ks_ref, vs_ref, kw_ref, vw_ref, sm_ref, ovl_ref, exp_ref,
                o_ref, qm_ref, m_ref, l_ref, acc_ref, *, tq, tk, seq_len):
    r_heads = NSA_GROUP_SIZE
    gi = pl.program_id(1)
    t0 = pl.program_id(2) * tq
    rows = r_heads * tq

    lane = lax.broadcasted_iota(jnp.int32, (tq, LANE), 1)
    lo = lane < HEAD_DIM
    for r in range(r_heads):
        qp = q_ref[0, :, (r // 2) * LANE:(r // 2 + 1) * LANE]
        qm_ref[r * tq:(r + 1) * tq, :] = jnp.where(lo if r % 2 == 0 else jnp.logical_not(lo), qp,
                                                   jnp.zeros_like(qp))
    qs = qm_ref[...]
    t_row = t0 + (lax.broadcasted_iota(jnp.int32, (rows, 1), 0) & (tq - 1))

    nb = kc_ref.shape[2]
    s_c = _dot_nt(qs, kc_ref[0, 0])
    c_end = lax.broadcasted_iota(jnp.int32, (1, nb), 1) * CMP_STRIDE + (CMP_BLOCK - 1)
    p_c = _masked_softmax(s_c, c_end <= t_row).astype(BF16)
    o_c = _dot(p_c, vc_ref[0, 0])
    imp4 = _dot(p_c, ovl_ref[...])
    imp = imp4[0:tq]
    for r in range(1, r_heads):
        imp = imp + imp4[r * tq:(r + 1) * tq]

    t_col = t0 + lax.broadcasted_iota(jnp.int32, (tq, 1), 0)
    blk = lane
    dist = (t_col // SLC_BLOCK) - blk
    forced = (blk == 0) | ((dist >= 0) & (dist < N_LOCAL_BLOCKS))
    score = jnp.where(forced, SEL_FORCE_SCORE, imp)
    score = jnp.where(blk * SLC_BLOCK <= t_col, score, -jnp.inf)
    n_slc = seq_len // SLC_BLOCK
    n_sel = min(N_SLC, n_slc)
    rank = jnp.zeros((tq, LANE), F32)
    for j in range(n_slc):
        col = score[:, j:j + 1]
        beats = (col > score) | ((col == score) & (blk > j))
        rank = rank + jnp.where(beats, 1.0, 0.0)
    sel = jnp.where(rank < n_sel, 1.0, 0.0).astype(BF16)

    m_ref[...] = jnp.full(m_ref.shape, NEG, F32)
    l_ref[...] = jnp.zeros(l_ref.shape, F32)
    acc_ref[...] = jnp.zeros(acc_ref.shape, F32)

    def slc_body(c, carry):
        k0 = pl.multiple_of(c * tk, tk)
        s = _dot_nt(qs, ks_ref[0, pl.ds(k0, tk), :])
        e = _dot(sel, exp_ref[c])
        e = jnp.concatenate([e] * r_heads, axis=0)
        s_pos = k0 + lax.broadcasted_iota(jnp.int32, (1, tk), 1)
        mask = (e > 0.5) & (s_pos <= t_row)
        s = jnp.where(mask, s, NEG)
        m_old = m_ref[...]
        m_new = jnp.maximum(m_old, jnp.max(s, axis=-1, keepdims=True))
        alpha = jnp.exp(m_old - m_new)
        p = jnp.where(mask, jnp.exp(s - _lane_tile(m_new, tk)), 0.0)
        l_ref[...] = alpha * l_ref[...] + jnp.sum(p, axis=-1, keepdims=True)
        acc_ref[...] = alpha * acc_ref[...] + _dot(p.astype(BF16), vs_ref[0, pl.ds(k0, tk), :])
        m_ref[...] = m_new
        return carry

    lax.fori_loop(0, (t0 + tq + tk - 1) // tk, slc_body, 0)
    o_s = acc_ref[...] / jnp.maximum(l_ref[...], 1e-30)

    wk = min(WINDOW + tq, seq_len)
    start = pl.multiple_of(jnp.maximum(t0 + tq - wk, 0), tq)
    s_w = _dot_nt(qs, kw_ref[0, pl.ds(start, wk), :])
    diff = t_row - (start + lax.broadcasted_iota(jnp.int32, (1, wk), 1))
    mask_w = (diff >= 0) & (diff < WINDOW)
    s_w = jnp.where(mask_w, s_w, NEG)
    s_w = s_w - jnp.max(s_w, axis=-1, keepdims=True)
    p_w = jnp.where(mask_w, jnp.exp(s_w), 0.0)
    l_w = jnp.maximum(jnp.sum(p_w, axis=-1, keepdims=True), 1e-30)
    o_w = _dot(p_w.astype(BF16), vw_ref[0, pl.ds(start, wk), :]) / l_w

    gates = sm_ref[0]

    def gate(branch, r):
        cols = [gates[:, branch * Q_HEADS + g * r_heads + r:branch * Q_HEADS + g * r_heads + r + 1]
                for g in range(NSA_KV_GROUPS)]
        col = cols[0]
        for g in range(1, NSA_KV_GROUPS):
            col = jnp.where(gi == g, cols[g], col)
        return col

    heads = []
    for r in range(r_heads):
        sl = slice(r * tq, (r + 1) * tq)
        heads.append(gate(0, r) * o_c[sl] + gate(1, r) * o_s[sl] + gate(2, r) * o_w[sl])
    for j in range(r_heads // 2):
        o_ref[0, :, j * LANE:(j + 1) * LANE] = jnp.where(lo, heads[2 * j], heads[2 * j + 1]).astype(o_ref.dtype)


def _nsa_constants(seq_len, tk):
    nb = seq_len // CMP_STRIDE
    n_cmp = (seq_len - CMP_BLOCK) // CMP_STRIDE + 1
    n_slc = seq_len // SLC_BLOCK
    c_start = np.arange(n_cmp)[:, None] * CMP_STRIDE
    s_start = np.arange(n_slc)[None, :] * SLC_BLOCK
    ov = np.minimum(c_start + CMP_BLOCK, s_start + SLC_BLOCK) - np.maximum(c_start, s_start)
    ovl = np.zeros((nb, LANE), np.float32)
    ovl[:n_cmp, :n_slc] = np.clip(ov, 0, None) / CMP_BLOCK
    expand = np.zeros((seq_len // tk, LANE, tk), np.float32)
    pos = np.arange(seq_len)
    expand[pos // tk, pos // SLC_BLOCK, pos % tk] = 1.0
    return jnp.asarray(ovl, BF16), jnp.asarray(expand, BF16)


def _nsa(q, kc, vc, ks, vs, kw, vw, small, bsz, seq_len, tq, tk):
    g, r = NSA_KV_GROUPS, NSA_GROUP_SIZE
    assert seq_len // SLC_BLOCK <= LANE and tq & (tq - 1) == 0
    nb = seq_len // CMP_STRIDE
    ovl, expand = _nsa_constants(seq_len, tk)
    rows = r * tq
    kv_spec = pl.BlockSpec((1, seq_len, LANE), lambda b, gi, i: (b, 0, gi))
    cmp_spec = pl.BlockSpec((1, 1, nb, LANE), lambda b, gi, i: (b, gi, 0, 0))
    q_spec = pl.BlockSpec((1, tq, r * HEAD_DIM), lambda b, gi, i: (b, i, gi))
    return pl.pallas_call(
        functools.partial(_nsa_kernel, tq=tq, tk=tk, seq_len=seq_len),
        grid=(bsz, g, seq_len // tq),
        in_specs=[q_spec, cmp_spec, cmp_spec, kv_spec, kv_spec, kv_spec, kv_spec,
                  pl.BlockSpec((1, tq, SMALL_W), lambda b, gi, i: (b, i, 0)),
                  _const_spec(ovl.shape), _const_spec(expand.shape)],
        out_specs=q_spec,
        out_shape=jax.ShapeDtypeStruct((bsz, seq_len, Q_HEADS * HEAD_DIM), BF16),
        scratch_shapes=[pltpu.VMEM((rows, LANE), BF16), pltpu.VMEM((rows, LANE), F32),
                        pltpu.VMEM((rows, LANE), F32), pltpu.VMEM((rows, LANE), F32)],
        compiler_params=pltpu.CompilerParams(dimension_semantics=("parallel", "parallel", "parallel"),
                                             vmem_limit_bytes=V7X_VMEM_LIMIT),
        name="nsa_attn",
    )(q.reshape(bsz, seq_len, -1), kc, vc, ks.reshape(bsz, seq_len, -1), vs.reshape(bsz, seq_len, -1),
      kw.reshape(bsz, seq_len, -1), vw.reshape(bsz, seq_len, -1), small.reshape(bsz, seq_len, -1),
      ovl, expand)


def _dsa_kernel(q_ref, k_ref, v_ref, iq_ref, ik_ref, sm_ref, o_ref,
                qm_ref, iqm_ref, key_ref, bias_ref, x_ref, m_ref, l_ref, acc_ref,
                *, tq, tk, seq_len, k_top):
    t0 = pl.program_id(1) * tq
    nck = (t0 + tq + tk - 1) // tk
    t_col = t0 + lax.broadcasted_iota(jnp.int32, (tq, 1), 0)
    lane = lax.broadcasted_iota(jnp.int32, (tq, LANE), 1)
    lo = lane < HEAD_DIM
    hi = jnp.logical_not(lo)

    for h in range(Q_HEADS):
        sl = slice((h // 2) * LANE, (h // 2 + 1) * LANE)
        keep = lo if h % 2 == 0 else hi
        qp, iqp = q_ref[0, :, sl], iq_ref[0, :, sl]
        qm_ref[h] = jnp.where(keep, qp, jnp.zeros_like(qp))
        iqm_ref[h] = jnp.where(keep, iqp, jnp.zeros_like(iqp))
    small = sm_ref[0]
    w_cols = [small[:, N_GATE + h:N_GATE + h + 1] for h in range(IDX_HEADS)]

    def pos_of(c):
        return pl.multiple_of(c * tk, tk) + lax.broadcasted_iota(jnp.int32, (1, tk), 1)

    def score_body(c, carry):
        ik = ik_ref[0, pl.ds(pl.multiple_of(c * tk, tk), tk), :]
        sc = jnp.zeros((tq, tk), F32)
        for h in range(IDX_HEADS):
            sc = sc + jnp.maximum(_dot_nt(iqm_ref[h], ik), 0.0) * w_cols[h]
        sc = jnp.where(pos_of(c) <= t_col, sc, -jnp.inf)
        bits = lax.bitcast_convert_type(sc, jnp.int32)
        key = jnp.where(bits < 0, bits ^ jnp.int32(0x7FFFFFFF), bits)
        key_ref[c] = jnp.where(sc == 0.0, 0, key)
        return carry

    lax.fori_loop(0, nck, score_body, 0)

    def count(pred):
        def body(c, acc):
            ind = jnp.where(pred(key_ref[c], pos_of(c)), 1.0, 0.0)
            part = ind[:, :LANE]
            for j in range(1, tk // LANE):
                part = part + ind[:, j * LANE:(j + 1) * LANE]
            return acc + part
        acc = lax.fori_loop(0, nck, body, jnp.zeros((tq, LANE), F32))
        return jnp.broadcast_to(jnp.sum(acc, axis=-1, keepdims=True), (tq, LANE))

    def count_ge(cand):
        cand_w = _lane_tile(cand, tk)
        return count(lambda key, pos: key >= cand_w)

    kf = float(k_top)
    zero = jnp.zeros((tq, LANE), jnp.int32)
    base = jnp.where(count_ge(zero) >= kf, zero, jnp.full((tq, LANE), INT_MIN, jnp.int32))

    def bit_body(b, base):
        cand = base + lax.shift_left(jnp.int32(1), 30 - b)
        return jnp.where(count_ge(cand) >= kf, cand, base)

    kth = lax.fori_loop(0, 31, bit_body, base)
    kth_w = _lane_tile(kth, tk)

    n_ge = count_ge(kth)
    n_gt = count(lambda key, pos: key > kth_w)
    tie = (n_ge > kf) & (kth > KEY_NEG_INF)
    x_ref[...] = jnp.full((tq, LANE), seq_len, jnp.int32)

    @pl.when(jnp.max(jnp.where(tie, 1.0, 0.0)) > 0.0)
    def _():
        room = kf - n_gt

        def pos_body(b, x):
            cand = x + lax.shift_left(jnp.int32(1), (seq_len.bit_length() - 1) - b)
            cand_w = _lane_tile(cand, tk)
            n = count(lambda key, pos: (key == kth_w) & (pos < cand_w))
            return jnp.where(n <= room, cand, x)

        x = lax.fori_loop(0, seq_len.bit_length(), pos_body, zero)
        x_ref[...] = jnp.where(tie, x, seq_len)

    x_w = _lane_tile(x_ref[...], tk)

    def bias_body(c, carry):
        key, pos = key_ref[c], pos_of(c)
        keep = ((key > kth_w) | ((key == kth_w) & (pos < x_w))) & (pos <= t_col)
        bias_ref[c] = jnp.where(keep, 0.0, NEG)
        return carry

    lax.fori_loop(0, nck, bias_body, 0)

    m_ref[...] = jnp.full(m_ref.shape, NEG, F32)
    l_ref[...] = jnp.zeros(l_ref.shape, F32)
    acc_ref[...] = jnp.zeros(acc_ref.shape, F32)

    def att_body(c, carry):
        k0 = pl.multiple_of(c * tk, tk)
        bias = bias_ref[c]
        for j in range(Q_HEADS // 2):
            kp = k_ref[0, pl.ds(k0, tk), j * LANE:(j + 1) * LANE]
            vp = v_ref[0, pl.ds(k0, tk), j * LANE:(j + 1) * LANE]
            for h in (2 * j, 2 * j + 1):
                s = _dot_nt(qm_ref[h], kp) + bias
                m_old = m_ref[h]
                m_new = jnp.maximum(m_old, jnp.max(s, axis=-1, keepdims=True))
                alpha = jnp.exp(m_old - m_new)
                p = jnp.exp(s - _lane_tile(m_new, tk))
                l_ref[h] = alpha * l_ref[h] + jnp.sum(p, axis=-1, keepdims=True)
                acc_ref[h] = alpha * acc_ref[h] + _dot(p.astype(BF16), vp)
                m_ref[h] = m_new
        return carry

    lax.fori_loop(0, nck, att_body, 0)
    for j in range(Q_HEADS // 2):
        o_lo = acc_ref[2 * j] / jnp.maximum(l_ref[2 * j], 1e-30)
        o_hi = acc_ref[2 * j + 1] / jnp.maximum(l_ref[2 * j + 1], 1e-30)
        o_ref[0, :, j * LANE:(j + 1) * LANE] = jnp.where(lo, o_lo, o_hi).astype(o_ref.dtype)


def _dsa(q, k, v, iq, ik, small, bsz, seq_len, tq, tk):
    k_top = min(DSA_TOPK_MAX, seq_len // 4)
    assert tk >= k_top and tk % LANE == 0
    width = Q_HEADS * HEAD_DIM
    nck = seq_len // tk
    q_spec = pl.BlockSpec((1, tq, width), lambda b, i: (b, i, 0))
    kv_spec = pl.BlockSpec((1, seq_len, width), lambda b, i: (b, 0, 0))
    return pl.pallas_call(
        functools.partial(_dsa_kernel, tq=tq, tk=tk, seq_len=seq_len, k_top=k_top),
        grid=(bsz, seq_len // tq),
        in_specs=[q_spec, kv_spec, kv_spec, q_spec,
                  pl.BlockSpec((1, seq_len, LANE), lambda b, i: (b, 0, 0)),
                  pl.BlockSpec((1, tq, SMALL_W), lambda b, i: (b, i, 0))],
        out_specs=q_spec,
        out_shape=jax.ShapeDtypeStruct((bsz, seq_len, width), BF16),
        scratch_shapes=[pltpu.VMEM((Q_HEADS, tq, LANE), BF16), pltpu.VMEM((IDX_HEADS, tq, LANE), BF16),
                        pltpu.VMEM((nck, tq, tk), jnp.int32), pltpu.VMEM((nck, tq, tk), F32),
                        pltpu.VMEM((tq, LANE), jnp.int32),
                        pltpu.VMEM((Q_HEADS, tq, LANE), F32), pltpu.VMEM((Q_HEADS, tq, LANE), F32),
                        pltpu.VMEM((Q_HEADS, tq, LANE), F32)],
        compiler_params=pltpu.CompilerParams(dimension_semantics=("parallel", "parallel"),
                                             vmem_limit_bytes=V7X_VMEM_LIMIT),
        name="dsa_attn",
    )(q.reshape(bsz, seq_len, -1), k.reshape(bsz, seq_len, -1), v.reshape(bsz, seq_len, -1),
      iq.reshape(bsz, seq_len, -1), ik.reshape(bsz, seq_len, -1), small.reshape(bsz, seq_len, -1))


def _merge_kernel(x_ref, yn_ref, yd_ref, gn_ref, gd_ref, wbn_ref, wbd_ref, wo_ref, o_ref):
    merged = gn_ref[...] * _dot(yn_ref[...], wbn_ref[...]) + gd_ref[...] * _dot(yd_ref[...], wbd_ref[...])
    o_ref[...] = x_ref[...] + _dot(merged.astype(BF16), wo_ref[...])


def _merge(xf, y_nsa, y_dsa, gate_nsa, gate_dsa, wbn, wbd, wo, tm):
    t, d = xf.shape
    row = lambda i: (i, 0)
    wn = y_nsa.shape[1]
    return pl.pallas_call(
        _merge_kernel,
        grid=(t // tm,),
        in_specs=[pl.BlockSpec((tm, d), row), pl.BlockSpec((tm, wn), row), pl.BlockSpec((tm, wn), row),
                  pl.BlockSpec((tm, d), row), pl.BlockSpec((tm, d), row),
                  _const_spec((wn, d)), _const_spec((wn, d)), _const_spec((d, d))],
        out_specs=pl.BlockSpec((tm, d), row),
        out_shape=jax.ShapeDtypeStruct((t, d), F32),
        compiler_params=pltpu.CompilerParams(dimension_semantics=("parallel",),
                                             vmem_limit_bytes=V7X_VMEM_LIMIT),
        name="merge",
    )(xf, y_nsa, y_dsa, gate_nsa, gate_dsa, wbn.astype(BF16), wbd.astype(BF16), wo.astype(BF16))


def _ffn_kernel(x_ref, g_ref, wg_ref, wu_ref, wd_ref, fg_ref, o_ref, *, final):
    x = x_ref[...]
    h = _rmsnorm(x, g_ref[...]).astype(BF16)
    z = (jax.nn.silu(_dot(h, wg_ref[...])) * _dot(h, wu_ref[...])).astype(BF16)
    y = x + _dot(z, wd_ref[...])
    o_ref[...] = _rmsnorm(y, fg_ref[...]) if final else y


def _ffn(xf, g, wg, wu, wd, final_g, final, tm):
    t, d = xf.shape
    f = wg.shape[1]
    row = lambda i: (i, 0)
    return pl.pallas_call(
        functools.partial(_ffn_kernel, final=final),
        grid=(t // tm,),
        in_specs=[pl.BlockSpec((tm, d), row), _const_spec((1, d)), _const_spec((d, f)), _const_spec((d, f)),
                  _const_spec((f, d)), _const_spec((1, d))],
        out_specs=pl.BlockSpec((tm, d), row),
        out_shape=jax.ShapeDtypeStruct((t, d), F32),
        compiler_params=pltpu.CompilerParams(dimension_semantics=("parallel",),
                                             vmem_limit_bytes=V7X_VMEM_LIMIT),
        name="ffn",
    )(xf, g.reshape(1, d), wg.astype(BF16), wu.astype(BF16), wd.astype(BF16), final_g.reshape(1, d))


def _rope_tables(seq_len):
    half = ROPE_DIM // 2
    pos = jnp.arange(seq_len, dtype=F32)
    inv_freq = ROPE_THETA ** (-jnp.arange(0, ROPE_DIM, 2, dtype=F32) / ROPE_DIM)
    ang = pos[:, None] * inv_freq[None, :]
    cos, sin = jnp.cos(ang), jnp.sin(ang)
    rest = HEAD_DIM - ROPE_DIM
    one, zero, zh = jnp.ones((seq_len, rest), F32), jnp.zeros((seq_len, rest), F32), jnp.zeros((seq_len, half), F32)
    two = lambda a: jnp.concatenate([a, a], axis=1)
    return (two(jnp.concatenate([cos, cos, one], axis=1)),
            two(jnp.concatenate([zh, sin, zero], axis=1)),
            two(jnp.concatenate([-sin, zh, zero], axis=1)))


def _tiles(seq_len):
    tm = min(256, seq_len)
    tq = min(128, seq_len)
    tk = min(256, seq_len)
    return tm, tq, tk


def kernel(x, attn_norm, w_in, cmpk_pos, cmpk_w1, cmpk_b1, cmpk_w2, cmpk_b2, cmpv_pos, cmpv_w1, cmpv_b1,
           cmpv_w2, cmpv_b2, w_branch_nsa, w_branch_dsa, w_out, ffn_norm, w_ffn_gate, w_ffn_up, w_ffn_down,
           final_norm):
    bsz, seq_len, d = x.shape
    depth = w_in.shape[0]
    tm, tq, tk = _tiles(seq_len)
    tables = _rope_tables(seq_len)
    xf = x.reshape(bsz * seq_len, d)
    for l in range(depth):
        (q_nsa, k_cmp, v_cmp, k_slc, v_slc, k_win, v_win, q_dsa, k_dsa, v_dsa, q_idx, k_idx,
         gate_nsa, gate_dsa, small) = _proj(xf, attn_norm[l], _permute_w_in(w_in[l]), tables, seq_len, tm)
        kc, vc = _compress(k_cmp, v_cmp, cmpk_pos[l], cmpk_w1[l], cmpk_b1[l], cmpk_w2[l], cmpk_b2[l],
                           cmpv_pos[l], cmpv_w1[l], cmpv_b1[l], cmpv_w2[l], cmpv_b2[l], bsz, seq_len)
        y_nsa = _nsa(q_nsa, kc, vc, k_slc, v_slc, k_win, v_win, small, bsz, seq_len, tq, tk)
        y_dsa = _dsa(q_dsa, k_dsa, v_dsa, q_idx, k_idx, small, bsz, seq_len, tq, tk)
        xf = _merge(xf, y_nsa.reshape(bsz * seq_len, -1), y_dsa.reshape(bsz * seq_len, -1),
                    gate_nsa, gate_dsa, w_branch_nsa[l], w_branch_dsa[l], w_out[l], tm)
        xf = _ffn(xf, ffn_norm[l], w_ffn_gate[l], w_ffn_up[l], w_ffn_down[l], final_norm,
                  final=(l == depth - 1), tm=tm)
    return xf.reshape(bsz, seq_len, d)
```

```python
import functools

import numpy as np
import jax
import jax.numpy as jnp
from jax import lax
from jax.experimental import pallas as pl
from jax.experimental.pallas import tpu as pltpu

HEAD_DIM = 64
ROPE_DIM = HEAD_DIM // 4
ROPE_HALF = ROPE_DIM // 2
ROPE_THETA = 500000.0
NORM_EPS = 1e-6
ATTN_SCALE = HEAD_DIM ** -0.5
Q_HEADS = 8
Q_WIDTH = Q_HEADS * HEAD_DIM
NSA_KV_GROUPS = 2
NSA_GROUP_SIZE = Q_HEADS // NSA_KV_GROUPS
CMP_BLOCK = 32
CMP_STRIDE = 16
CMP_HIDDEN = 2 * HEAD_DIM
SLC_BLOCK = 64
N_SLC = 16
N_LOCAL_BLOCKS = 2
SEL_FORCE_SCORE = 1e4
WINDOW = 512
IDX_HEADS = 8
IDX_DIM = 64
IDX_WEIGHT_SCALE = (IDX_HEADS * IDX_DIM) ** -0.5
DSA_TOPK_MAX = 256
N_GATE = 3 * Q_HEADS
SMALL_W = N_GATE + IDX_HEADS
MAX_SLC_BLOCKS = 32

LANE = 128
SUBLANE = 8
V7X_VMEM_LIMIT = 56 * 1024 * 1024

NEG = -1e30
INT_MIN = -(2 ** 31)
KEY_NEG_INF = -2139095041
BF16 = jnp.bfloat16
F32 = jnp.float32

_NT = (((1,), (1,)), ((), ()))


def _dot(a, b):
    return jnp.dot(a, b, preferred_element_type=F32)


def _dot_nt(a, b):
    return lax.dot_general(a, b, _NT, preferred_element_type=F32)


def _const_spec(shape):
    nd = len(shape)
    return pl.BlockSpec(shape, lambda *_: (0,) * nd, pipeline_mode=pl.Buffered(1))


def _rmsnorm(x, g):
    ms = jnp.mean(x * x, axis=-1, keepdims=True)
    return x * lax.rsqrt(ms + NORM_EPS) * g


def _lane_tile(a, reps):
    return a if reps == 1 else jnp.concatenate([a] * reps, axis=1)


def _params(*sem):
    return pltpu.CompilerParams(dimension_semantics=sem, vmem_limit_bytes=V7X_VMEM_LIMIT)


_PROJ_SEGS = (
    ("t", Q_WIDTH, "ropeq", BF16),
    ("n", 128, "rope", F32),
    ("n", 128, "plain", F32),
    ("n", 256, "rope_onehot", BF16),
    ("t", 128, "plain", BF16),
    ("n", 256, "rope", BF16),
    ("t", 128, "plain", BF16),
    ("t", Q_WIDTH, "ropeq", BF16),
    ("n", Q_WIDTH, "rope", BF16),
    ("t", Q_WIDTH, "plain", BF16),
    ("t", IDX_HEADS * IDX_DIM, "rope", BF16),
    ("n", 128, "rope", BF16),
    ("n", 1024, "sig", F32),
    ("n", 1024, "sig", F32),
    ("t", SMALL_W, "small", F32),
)
_PROJ_NW = sum(s[1] for s in _PROJ_SEGS if s[0] == "n")
_PROJ_TW = sum(s[1] for s in _PROJ_SEGS if s[0] == "t")
_PROJ_CHUNK = 512


def _rope_n(y, c, sa, sb):
    pieces = []
    for j in range(y.shape[1] // LANE):
        yj = y[:, j * LANE:(j + 1) * LANE]
        pieces.append(yj * c + pltpu.roll(yj, ROPE_HALF, 1) * sa + pltpu.roll(yj, LANE - ROPE_HALF, 1) * sb)
    return pieces[0] if len(pieces) == 1 else jnp.concatenate(pieces, axis=1)


def _rope_t(y, cos, sin):
    pieces = []
    for h in range(y.shape[0] // HEAD_DIM):
        r0 = h * HEAD_DIM
        x1, x2 = y[r0:r0 + ROPE_HALF], y[r0 + ROPE_HALF:r0 + ROPE_DIM]
        pieces += [x1 * cos - x2 * sin, x2 * cos + x1 * sin, y[r0 + ROPE_DIM:r0 + HEAD_DIM]]
    return jnp.concatenate(pieces, axis=0)


def _proj_kernel(x_ref, g_ref, wn_ref, wt_ref, c_ref, sa_ref, sb_ref, ct_ref, st_ref, *out_refs, npos):
    tm = x_ref.shape[0]
    h = _rmsnorm(x_ref[...], g_ref[...]).astype(BF16)
    c, sa, sb = c_ref[...], sa_ref[...], sb_ref[...]
    cos_t, sin_t = ct_ref[0], st_ref[0]
    ncol = trow = 0
    for (layout, width, kind, _), o_ref in zip(_PROJ_SEGS, out_refs):
        if layout == "t":
            y = _dot_nt(wt_ref[trow:trow + width, :], h)
            trow += width
            if kind in ("rope", "ropeq"):
                y = _rope_t(y, cos_t, sin_t)
            if kind == "ropeq":
                y = y * ATTN_SCALE
            if kind == "small":
                row = lax.broadcasted_iota(jnp.int32, y.shape, 0)
                y = jnp.where(row < N_GATE, jax.nn.sigmoid(y), y * IDX_WEIGHT_SCALE)
            o_ref[0] = y.astype(o_ref.dtype)
            continue
        for c0 in range(0, width, _PROJ_CHUNK):
            cw = min(_PROJ_CHUNK, width - c0)
            y = _dot(h, wn_ref[:, ncol + c0:ncol + c0 + cw])
            if kind in ("rope", "rope_onehot"):
                y = _rope_n(y, c, sa, sb)
            if kind == "rope_onehot":
                pos = (pl.program_id(0) % npos) * tm + lax.broadcasted_iota(jnp.int32, y.shape, 0)
                lane = lax.broadcasted_iota(jnp.int32, y.shape, 1) & (LANE - 1)
                y = jnp.where(lane == HEAD_DIM + pos // SLC_BLOCK, 1.0, y)
            if kind == "sig":
                y = jax.nn.sigmoid(y)
            o_ref[:, c0:c0 + cw] = y.astype(o_ref.dtype)
        ncol += width


def _split_w_in(w):
    d = w.shape[0]
    kvw = NSA_KV_GROUPS * HEAD_DIM
    sizes = (Q_WIDTH, kvw, kvw, kvw, kvw, kvw, kvw, N_GATE, Q_WIDTH, Q_WIDTH, Q_WIDTH,
             IDX_HEADS * IDX_DIM, IDX_DIM, IDX_HEADS, d, d)
    assert sum(sizes) == w.shape[1]
    offs = np.concatenate([[0], np.cumsum(sizes)])
    (nsa_q, k_cmp, v_cmp, k_slc, v_slc, k_win, v_win, nsa_gate, dsa_q, dsa_k, dsa_v,
     idx_q, idx_k, idx_w, gate_nsa, gate_dsa) = [w[:, offs[i]:offs[i + 1]] for i in range(16)]

    def pad_groups(a):
        a = a.reshape(d, NSA_KV_GROUPS, HEAD_DIM)
        return jnp.concatenate([a, jnp.zeros_like(a)], axis=2).reshape(d, NSA_KV_GROUPS * LANE)

    idx_k_pad = jnp.concatenate([idx_k, jnp.zeros_like(idx_k)], axis=1)
    w_n = jnp.concatenate([k_cmp, v_cmp, pad_groups(k_slc), pad_groups(k_win), dsa_k, idx_k_pad,
                           gate_nsa, gate_dsa], axis=1)
    w_t = jnp.concatenate([nsa_q, v_slc, v_win, dsa_q, dsa_v, idx_q, nsa_gate, idx_w], axis=1).T
    assert w_n.shape[1] == _PROJ_NW and w_t.shape[0] == _PROJ_TW
    return w_n.astype(BF16), w_t.astype(BF16)


def _proj(xf, g, w_n, w_t, tables, seq_len, tm):
    t, d = xf.shape
    npos = seq_len // tm
    row = lambda i: (i, 0)
    tab_n = pl.BlockSpec((tm, LANE), lambda i: (i % npos, 0))
    tab_t = pl.BlockSpec((1, ROPE_HALF, tm), lambda i: (i % npos, 0, 0))
    out_specs, out_shape = [], []
    for layout, width, _, dt in _PROJ_SEGS:
        if layout == "t":
            out_specs.append(pl.BlockSpec((1, width, tm), lambda i: (i, 0, 0)))
            out_shape.append(jax.ShapeDtypeStruct((t // tm, width, tm), dt))
        else:
            out_specs.append(pl.BlockSpec((tm, width), row))
            out_shape.append(jax.ShapeDtypeStruct((t, width), dt))
    return pl.pallas_call(
        functools.partial(_proj_kernel, npos=npos),
        grid=(t // tm,),
        in_specs=[pl.BlockSpec((tm, d), row), _const_spec((1, d)), _const_spec((d, _PROJ_NW)),
                  _const_spec((_PROJ_TW, d)), tab_n, tab_n, tab_n, tab_t, tab_t],
        out_specs=out_specs,
        out_shape=out_shape,
        compiler_params=_params("parallel"),
        name="proj",
    )(xf, g.reshape(1, d), w_n, w_t, *tables)


def _cmp_kernel(kb_ref, vb_ref, pk_ref, w1k_ref, b1k_ref, w2k_ref, b2k_ref,
                pv_ref, w1v_ref, b1v_ref, w2v_ref, b2v_ref, kc_ref, vc_ref, *, n_cmp):
    def hidden(xb_ref, pos_ref, w1_ref, b1_ref):
        x = xb_ref[0, 0]
        nb, half = x.shape
        a = _dot((x + pos_ref[:, :half]).astype(BF16), w1_ref[:half, :])
        b = _dot((x + pos_ref[:, half:]).astype(BF16), w1_ref[half:, :])
        return jax.nn.gelu(a + pltpu.roll(b, nb - 1, 0) + b1_ref[...]).astype(BF16)

    hk = hidden(kb_ref, pk_ref, w1k_ref, b1k_ref)
    kc = _dot(hk, w2k_ref[...]) + b2k_ref[...]
    rows = lax.broadcasted_iota(jnp.int32, kc.shape, 0)
    kc_ref[0, 0] = jnp.where(rows < n_cmp, kc, 0.0).astype(kc_ref.dtype)
    hv = hidden(vb_ref, pv_ref, w1v_ref, b1v_ref)
    vc = _dot_nt(w2v_ref[...], hv) + b2v_ref[...]
    cols = lax.broadcasted_iota(jnp.int32, vc.shape, 1)
    vc_ref[0, 0] = jnp.where(cols < n_cmp, vc, 0.0).astype(vc_ref.dtype)


def _compress(k_cmp, v_cmp, pk, w1k, b1k, w2k, b2k, pv, w1v, b1v, w2v, b2v, bsz, seq_len):
    g, hd = NSA_KV_GROUPS, HEAD_DIM
    nb = seq_len // CMP_STRIDE
    n_cmp = (seq_len - CMP_BLOCK) // CMP_STRIDE + 1
    flat = CMP_STRIDE * hd

    def strides(a):
        return a.reshape(bsz, nb, CMP_STRIDE, g, hd).transpose(0, 3, 1, 2, 4).reshape(bsz, g, nb, flat)

    k_w = (pk.reshape(1, CMP_BLOCK * hd), w1k.astype(BF16), b1k.reshape(1, -1),
           jnp.concatenate([w2k, jnp.zeros_like(w2k)], axis=1).astype(BF16),
           jnp.concatenate([b2k, jnp.zeros_like(b2k)]).reshape(1, -1))
    v_w = (pv.reshape(1, CMP_BLOCK * hd), w1v.astype(BF16), b1v.reshape(1, -1),
           w2v.T.astype(BF16), b2v.reshape(-1, 1))
    blk = pl.BlockSpec((1, 1, nb, flat), lambda b, gi: (b, gi, 0, 0))
    common = [_const_spec((1, 2 * flat)), _const_spec((2 * flat, CMP_HIDDEN)), _const_spec((1, CMP_HIDDEN))]
    return pl.pallas_call(
        functools.partial(_cmp_kernel, n_cmp=n_cmp),
        grid=(bsz, g),
        in_specs=[blk, blk] + common + [_const_spec((CMP_HIDDEN, LANE)), _const_spec((1, LANE))]
        + common + [_const_spec((hd, CMP_HIDDEN)), _const_spec((hd, 1))],
        out_specs=[pl.BlockSpec((1, 1, nb, LANE), lambda b, gi: (b, gi, 0, 0)),
                   pl.BlockSpec((1, 1, hd, nb), lambda b, gi: (b, gi, 0, 0))],
        out_shape=[jax.ShapeDtypeStruct((bsz, g, nb, LANE), BF16),
                   jax.ShapeDtypeStruct((bsz, g, hd, nb), BF16)],
        compiler_params=_params("parallel", "parallel"),
        name="compress",
    )(strides(k_cmp), strides(v_cmp), *k_w, *v_w)


def _fold(x, op):
    return op(x.reshape(x.shape[0] // SUBLANE, SUBLANE, x.shape[1]), axis=0)


def _nsa_kernel(q_ref, kc_ref, vc_ref, ks_ref, vs_ref, kw_ref, vw_ref, sm_ref, ovl_ref, o_ref,
                qm_ref, ss_ref, sw_ref, as_ref, *, t, seq_len):
    r_heads = NSA_GROUP_SIZE
    gi = pl.program_id(1)
    i = pl.program_id(2)
    t0 = i * t
    rows = r_heads * t
    hd = HEAD_DIM

    for r in range(r_heads):
        qm_ref[0:hd, r * t:(r + 1) * t] = q_ref[0, 0, r * hd:(r + 1) * hd, :]
    qm_ref[hd:, :] = jnp.zeros((LANE - hd, rows), BF16)
    tok = t0 + lax.broadcasted_iota(jnp.int32, (1, t), 1)
    tok_r = _lane_tile(tok, r_heads)

    nb = kc_ref.shape[2]
    s_c = _dot(kc_ref[0, 0], qm_ref[...])
    c_end = lax.broadcasted_iota(jnp.int32, (nb, 1), 0) * CMP_STRIDE + (CMP_BLOCK - 1)
    mask_c = c_end <= tok_r
    s_c = jnp.where(mask_c, s_c, NEG)
    s_c = s_c - jnp.max(s_c, axis=0, keepdims=True)
    p_c = jnp.where(mask_c, jnp.exp(s_c), 0.0)
    p_c = (p_c / jnp.maximum(jnp.sum(p_c, axis=0, keepdims=True), 1e-30)).astype(BF16)
    o_c = _dot(vc_ref[0, 0], p_c)
    imp4 = _dot(ovl_ref[...], p_c)
    imp = imp4[:, 0:t]
    for r in range(1, r_heads):
        imp = imp + imp4[:, r * t:(r + 1) * t]

    n_slc = seq_len // SLC_BLOCK
    n_sel = min(N_SLC, n_slc)
    blk = lax.broadcasted_iota(jnp.int32, (MAX_SLC_BLOCKS, 1), 0)
    dist = (tok // SLC_BLOCK) - blk
    forced = (blk == 0) | ((dist >= 0) & (dist < N_LOCAL_BLOCKS))
    score = jnp.where(forced, SEL_FORCE_SCORE, imp)
    score = jnp.where(blk * SLC_BLOCK <= tok, score, -jnp.inf)
    groups = [score[v * SUBLANE:(v + 1) * SUBLANE] for v in range(MAX_SLC_BLOCKS // SUBLANE)]
    ranks = [jnp.zeros((SUBLANE, t), F32) for _ in groups]
    sub = lax.broadcasted_iota(jnp.int32, (SUBLANE, 1), 0)
    for j in range(n_slc):
        row = score[j:j + 1]
        for v, sv in enumerate(groups):
            ge = jnp.where(row >= sv, 1.0, 0.0)
            gt = jnp.where(row > sv, 1.0, 0.0)
            if v * SUBLANE > j:
                beats = ge
            elif (v + 1) * SUBLANE <= j + 1:
                beats = gt
            else:
                beats = jnp.where(sub + v * SUBLANE > j, ge, gt)
            ranks[v] = ranks[v] + beats
    selbias = jnp.where(jnp.concatenate(ranks, axis=0) < n_sel, 0.0, NEG).astype(BF16)
    for r in range(r_heads):
        qm_ref[hd:hd + MAX_SLC_BLOCKS, r * t:(r + 1) * t] = selbias
    key_off = lax.broadcasted_iota(jnp.int32, (t, 1), 0)
    q_off = lax.broadcasted_iota(jnp.int32, (1, t), 1)
    causal = _lane_tile(jnp.where(key_off <= q_off, 0.0, NEG), r_heads)
    neg8 = jnp.full((SUBLANE, rows), NEG, F32)

    def key_tile(k_ref_, c):
        return k_ref_[0, pl.ds(pl.multiple_of(c * t, t), t), :]

    def slc_scores(c, mx):
        s = _dot(key_tile(ks_ref, c), qm_ref[...])
        ss_ref[c] = s
        return jnp.maximum(mx, _fold(s, jnp.max))

    mx = lax.fori_loop(0, i, slc_scores, neg8)
    s = _dot(key_tile(ks_ref, i), qm_ref[...]) + causal
    ss_ref[i] = s
    m_s = jnp.max(jnp.maximum(mx, _fold(s, jnp.max)), axis=0, keepdims=True)

    n_back = -(-WINDOW // t)
    win_tiles = []
    mxw = neg8
    for back in range(n_back, -1, -1):
        if back:
            lo_edge = back * t - WINDOW
            bias = _lane_tile(jnp.where(key_off - q_off > jnp.where(i >= back, lo_edge, t), 0.0, NEG), r_heads)
        else:
            bias = causal
        c = jnp.maximum(i - back, 0)
        s = _dot(key_tile(kw_ref, c), qm_ref[...]) + bias
        sw_ref[n_back - back] = s
        mxw = jnp.maximum(mxw, _fold(s, jnp.max))
        win_tiles.append(c)
    m_w = jnp.max(mxw, axis=0, keepdims=True)

    as_ref[...] = jnp.zeros(as_ref.shape, F32)

    def slc_values(c, l8):
        p = jnp.exp(ss_ref[c] - m_s)
        as_ref[...] += _dot(vs_ref[0, c], p.astype(BF16))
        return l8 + _fold(p, jnp.sum)

    l8 = lax.fori_loop(0, i + 1, slc_values, jnp.zeros((SUBLANE, rows), F32))
    o_s = as_ref[...] * (1.0 / jnp.sum(l8, axis=0, keepdims=True))

    l8 = jnp.zeros((SUBLANE, rows), F32)
    o_w = jnp.zeros((hd, rows), F32)
    for n, c in enumerate(win_tiles):
        p = jnp.exp(sw_ref[n] - m_w)
        o_w = o_w + _dot(vw_ref[0, c], p.astype(BF16))
        l8 = l8 + _fold(p, jnp.sum)
    o_w = o_w * (1.0 / jnp.sum(l8, axis=0, keepdims=True))

    gates = sm_ref[0, 0]

    def gate(branch, r):
        rows_g = [gates[branch * Q_HEADS + g * r_heads + r:branch * Q_HEADS + g * r_heads + r + 1]
                  for g in range(NSA_KV_GROUPS)]
        out = rows_g[0]
        for g in range(1, NSA_KV_GROUPS):
            out = jnp.where(gi == g, rows_g[g], out)
        return out

    heads = []
    for r in range(r_heads):
        sl = slice(r * t, (r + 1) * t)
        heads.append(gate(0, r) * o_c[:, sl] + gate(1, r) * o_s[:, sl] + gate(2, r) * o_w[:, sl])
    o_ref[0] = jnp.concatenate(heads, axis=0).T.astype(o_ref.dtype)


def _overlap_t(seq_len):
    nb = seq_len // CMP_STRIDE
    n_cmp = (seq_len - CMP_BLOCK) // CMP_STRIDE + 1
    n_slc = seq_len // SLC_BLOCK
    c_start = np.arange(n_cmp)[None, :] * CMP_STRIDE
    s_start = np.arange(n_slc)[:, None] * SLC_BLOCK
    ov = np.minimum(c_start + CMP_BLOCK, s_start + SLC_BLOCK) - np.maximum(c_start, s_start)
    ovl = np.zeros((MAX_SLC_BLOCKS, nb), np.float32)
    ovl[:n_slc, :n_cmp] = np.clip(ov, 0, None) / CMP_BLOCK
    return jnp.asarray(ovl, BF16)


def _nsa(q_t, kc, vc_t, ks, vs_t, kw, vw_t, small_t, bsz, seq_len, t):
    g, r, hd = NSA_KV_GROUPS, NSA_GROUP_SIZE, HEAD_DIM
    assert seq_len // SLC_BLOCK <= MAX_SLC_BLOCKS
    nb = seq_len // CMP_STRIDE
    nc = seq_len // t
    rows = r * t
    ovl = _overlap_t(seq_len)
    k_spec = pl.BlockSpec((1, seq_len, LANE), lambda b, gi, i: (b, 0, gi))
    v_spec = pl.BlockSpec((1, nc, hd, t), lambda b, gi, i: (b, 0, gi, 0))
    return pl.pallas_call(
        functools.partial(_nsa_kernel, t=t, seq_len=seq_len),
        grid=(bsz, g, nc),
        in_specs=[pl.BlockSpec((1, 1, r * hd, t), lambda b, gi, i: (b, i, gi, 0)),
                  pl.BlockSpec((1, 1, nb, LANE), lambda b, gi, i: (b, gi, 0, 0)),
                  pl.BlockSpec((1, 1, hd, nb), lambda b, gi, i: (b, gi, 0, 0)),
                  k_spec, v_spec, k_spec, v_spec,
                  pl.BlockSpec((1, 1, SMALL_W, t), lambda b, gi, i: (b, i, 0, 0)),
                  _const_spec(ovl.shape)],
        out_specs=pl.BlockSpec((1, t, r * hd), lambda b, gi, i: (b, i, gi)),
        out_shape=jax.ShapeDtypeStruct((bsz, seq_len, Q_WIDTH), BF16),
        scratch_shapes=[pltpu.VMEM((LANE, rows), BF16), pltpu.VMEM((nc, t, rows), F32),
                        pltpu.VMEM((-(-WINDOW // t) + 1, t, rows), F32), pltpu.VMEM((hd, rows), F32)],
        compiler_params=_params("parallel", "parallel", "parallel"),
        name="nsa_attn",
    )(q_t.reshape(bsz, nc, Q_WIDTH, t), kc, vc_t, ks.reshape(bsz, seq_len, -1), vs_t.reshape(bsz, nc, g * hd, t),
      kw.reshape(bsz, seq_len, -1), vw_t.reshape(bsz, nc, g * hd, t), small_t.reshape(bsz, nc, SMALL_W, t), ovl)


def _dsa_kernel(q_ref, k_ref, v_ref, iq_ref, ik_ref, sm_ref, o_ref,
                qm_ref, iqm_ref, key_ref, s_ref, acc_ref, *, t, seq_len, k_top):
    i = pl.program_id(1)
    t0 = i * t
    nck = i + 1
    hd = HEAD_DIM
    tok = t0 + lax.broadcasted_iota(jnp.int32, (1, t), 1)

    zeros = jnp.zeros((hd, t), BF16)
    for h in range(Q_HEADS):
        qh = q_ref[0, 0, h * hd:(h + 1) * hd, :]
        qm_ref[h] = jnp.concatenate([qh, zeros] if h % 2 == 0 else [zeros, qh], axis=0)
        iqm_ref[h] = jnp.concatenate([iq_ref[0, 0, h * hd:(h + 1) * hd, :], zeros], axis=0)
    small = sm_ref[0, 0]

    def key_pos(c):
        return pl.multiple_of(c * t, t) + lax.broadcasted_iota(jnp.int32, (t, 1), 0)

    def score_body(c, carry):
        ik = ik_ref[0, pl.ds(pl.multiple_of(c * t, t), t), :]
        sc = jnp.zeros((t, t), F32)
        for h in range(IDX_HEADS):
            sc = sc + jnp.maximum(_dot(ik, iqm_ref[h]), 0.0) * small[N_GATE + h:N_GATE + h + 1]
        sc = jnp.where(key_pos(c) <= tok, sc, -jnp.inf)
        bits = lax.bitcast_convert_type(sc, jnp.int32)
        key = jnp.where(bits < 0, bits ^ jnp.int32(0x7FFFFFFF), bits)
        key_ref[c] = jnp.where(sc == 0.0, 0, key)
        return carry

    lax.fori_loop(0, nck, score_body, 0)

    def count(pred):
        def body(c, acc):
            ind = jnp.where(pred(key_ref[c], key_pos(c)), 1.0, 0.0)
            return acc + jnp.sum(ind.reshape(t // SUBLANE, SUBLANE, t), axis=0)
        acc = lax.fori_loop(0, nck, body, jnp.zeros((SUBLANE, t), F32))
        return jnp.sum(acc, axis=0, keepdims=True)

    kf = float(k_top)
    zero = jnp.zeros((1, t), jnp.int32)
    base = jnp.where(count(lambda key, pos: key >= zero) >= kf, zero, jnp.full((1, t), INT_MIN, jnp.int32))

    def bit_body(b, base):
        cand = base + lax.shift_left(jnp.int32(1), 30 - b)
        return jnp.where(count(lambda key, pos: key >= cand) >= kf, cand, base)

    kth = lax.fori_loop(0, 31, bit_body, base)

    n_ge = count(lambda key, pos: key >= kth)
    n_gt = count(lambda key, pos: key > kth)
    tie = (n_ge > kf) & (kth > KEY_NEG_INF)

    def tie_cut():
        room = kf - n_gt
        top_bit = seq_len.bit_length() - 1

        def pos_body(b, x):
            cand = x + lax.shift_left(jnp.int32(1), top_bit - b)
            n = count(lambda key, pos: (key == kth) & (pos < cand))
            return jnp.where(n <= room, cand, x)

        x = lax.fori_loop(0, top_bit + 1, pos_body, zero)
        return jnp.where(tie, x, seq_len)

    cut = lax.cond(jnp.max(jnp.where(tie, 1.0, 0.0)) > 0.0, tie_cut,
                   lambda: jnp.full((1, t), seq_len, jnp.int32))

    def att_scores(c, mx):
        k0 = pl.multiple_of(c * t, t)
        key, pos = key_ref[c], key_pos(c)
        keep = ((key > kth) | ((key == kth) & (pos < cut))) & (pos <= tok)
        bias = jnp.where(keep, 0.0, NEG)
        out = []
        for h in range(Q_HEADS):
            s = _dot(k_ref[0, pl.ds(k0, t), (h // 2) * LANE:(h // 2 + 1) * LANE], qm_ref[h]) + bias
            s_ref[h, c] = s
            out.append(jnp.maximum(mx[h], _fold(s, jnp.max)))
        return tuple(out)

    mx = lax.fori_loop(0, nck, att_scores, (jnp.full((SUBLANE, t), NEG, F32),) * Q_HEADS)
    m = [jnp.max(mx[h], axis=0, keepdims=True) for h in range(Q_HEADS)]

    acc_ref[...] = jnp.zeros(acc_ref.shape, F32)

    def att_values(c, l8):
        out = []
        for h in range(Q_HEADS):
            p = jnp.exp(s_ref[h, c] - m[h])
            acc_ref[h] += _dot(v_ref[0, c, h * hd:(h + 1) * hd, :], p.astype(BF16))
            out.append(l8[h] + _fold(p, jnp.sum))
        return tuple(out)

    l8 = lax.fori_loop(0, nck, att_values, (jnp.zeros((SUBLANE, t), F32),) * Q_HEADS)
    out = jnp.concatenate([acc_ref[h] * (1.0 / jnp.sum(l8[h], axis=0, keepdims=True))
                           for h in range(Q_HEADS)], axis=0)
    o_ref[0] = out.T.astype(o_ref.dtype)


def _dsa(q_t, k, v_t, iq_t, ik, small_t, bsz, seq_len, t):
    k_top = min(DSA_TOPK_MAX, seq_len // 4)
    assert t >= k_top
    nc = seq_len // t
    qt_spec = pl.BlockSpec((1, 1, Q_WIDTH, t), lambda b, i: (b, i, 0, 0))
    return pl.pallas_call(
        functools.partial(_dsa_kernel, t=t, seq_len=seq_len, k_top=k_top),
        grid=(bsz, nc),
        in_specs=[qt_spec,
                  pl.BlockSpec((1, seq_len, Q_WIDTH), lambda b, i: (b, 0, 0)),
                  pl.BlockSpec((1, nc, Q_WIDTH, t), lambda b, i: (b, 0, 0, 0)),
                  qt_spec,
                  pl.BlockSpec((1, seq_len, LANE), lambda b, i: (b, 0, 0)),
                  pl.BlockSpec((1, 1, SMALL_W, t), lambda b, i: (b, i, 0, 0))],
        out_specs=pl.BlockSpec((1, t, Q_WIDTH), lambda b, i: (b, i, 0)),
        out_shape=jax.ShapeDtypeStruct((bsz, seq_len, Q_WIDTH), BF16),
        scratch_shapes=[pltpu.VMEM((Q_HEADS, LANE, t), BF16), pltpu.VMEM((IDX_HEADS, LANE, t), BF16),
                        pltpu.VMEM((nc, t, t), jnp.int32), pltpu.VMEM((Q_HEADS, nc, t, t), F32),
                        pltpu.VMEM((Q_HEADS, HEAD_DIM, t), F32)],
        compiler_params=_params("parallel", "parallel"),
        name="dsa_attn",
    )(q_t.reshape(bsz, nc, Q_WIDTH, t), k.reshape(bsz, seq_len, -1), v_t.reshape(bsz, nc, Q_WIDTH, t),
      iq_t.reshape(bsz, nc, Q_WIDTH, t), ik.reshape(bsz, seq_len, -1), small_t.reshape(bsz, nc, SMALL_W, t))


def _merge_kernel(x_ref, yn_ref, yd_ref, gn_ref, gd_ref, wbn_ref, wbd_ref, wo_ref, o_ref):
    merged = gn_ref[...] * _dot(yn_ref[...], wbn_ref[...]) + gd_ref[...] * _dot(yd_ref[...], wbd_ref[...])
    o_ref[...] = x_ref[...] + _dot(merged.astype(BF16), wo_ref[...])


def _merge(xf, y_nsa, y_dsa, gate_nsa, gate_dsa, wbn, wbd, wo, tm):
    t, d = xf.shape
    row = lambda i: (i, 0)
    wn = y_nsa.shape[1]
    return pl.pallas_call(
        _merge_kernel,
        grid=(t // tm,),
        in_specs=[pl.BlockSpec((tm, d), row), pl.BlockSpec((tm, wn), row), pl.BlockSpec((tm, wn), row),
                  pl.BlockSpec((tm, d), row), pl.BlockSpec((tm, d), row),
                  _const_spec((wn, d)), _const_spec((wn, d)), _const_spec((d, d))],
        out_specs=pl.BlockSpec((tm, d), row),
        out_shape=jax.ShapeDtypeStruct((t, d), F32),
        compiler_params=_params("parallel"),
        name="merge",
    )(xf, y_nsa, y_dsa, gate_nsa, gate_dsa, wbn.astype(BF16), wbd.astype(BF16), wo.astype(BF16))


def _ffn_kernel(x_ref, g_ref, wg_ref, wu_ref, wd_ref, fg_ref, o_ref, *, final):
    x = x_ref[...]
    h = _rmsnorm(x, g_ref[...]).astype(BF16)
    z = (jax.nn.silu(_dot(h, wg_ref[...])) * _dot(h, wu_ref[...])).astype(BF16)
    y = x + _dot(z, wd_ref[...])
    o_ref[...] = _rmsnorm(y, fg_ref[...]) if final else y


def _ffn(xf, g, wg, wu, wd, final_g, final, tm):
    t, d = xf.shape
    f = wg.shape[1]
    row = lambda i: (i, 0)
    return pl.pallas_call(
        functools.partial(_ffn_kernel, final=final),
        grid=(t // tm,),
        in_specs=[pl.BlockSpec((tm, d), row), _const_spec((1, d)), _const_spec((d, f)), _const_spec((d, f)),
                  _const_spec((f, d)), _const_spec((1, d))],
        out_specs=pl.BlockSpec((tm, d), row),
        out_shape=jax.ShapeDtypeStruct((t, d), F32),
        compiler_params=_params("parallel"),
        name="ffn",
    )(xf, g.reshape(1, d), wg.astype(BF16), wu.astype(BF16), wd.astype(BF16), final_g.reshape(1, d))


def _rope_tables(seq_len, tm):
    pos = jnp.arange(seq_len, dtype=F32)
    inv_freq = ROPE_THETA ** (-jnp.arange(0, ROPE_DIM, 2, dtype=F32) / ROPE_DIM)
    ang = pos[:, None] * inv_freq[None, :]
    cos, sin = jnp.cos(ang), jnp.sin(ang)
    rest = HEAD_DIM - ROPE_DIM
    one, zero = jnp.ones((seq_len, rest), F32), jnp.zeros((seq_len, rest), F32)
    zh = jnp.zeros((seq_len, ROPE_HALF), F32)
    two = lambda a: jnp.concatenate([a, a], axis=1)
    tile_t = lambda a: a.reshape(seq_len // tm, tm, ROPE_HALF).transpose(0, 2, 1)
    return (two(jnp.concatenate([cos, cos, one], axis=1)),
            two(jnp.concatenate([zh, sin, zero], axis=1)),
            two(jnp.concatenate([-sin, zh, zero], axis=1)),
            tile_t(cos), tile_t(sin))


def _tile(seq_len):
    return min(256, seq_len)


def kernel(x, attn_norm, w_in, cmpk_pos, cmpk_w1, cmpk_b1, cmpk_w2, cmpk_b2, cmpv_pos, cmpv_w1, cmpv_b1,
           cmpv_w2, cmpv_b2, w_branch_nsa, w_branch_dsa, w_out, ffn_norm, w_ffn_gate, w_ffn_up, w_ffn_down,
           final_norm):
    bsz, seq_len, d = x.shape
    depth = w_in.shape[0]
    t = _tile(seq_len)
    tables = _rope_tables(seq_len, t)
    xf = x.reshape(bsz * seq_len, d)
    for l in range(depth):
        (q_nsa, k_cmp, v_cmp, k_slc, v_slc, k_win, v_win, q_dsa, k_dsa, v_dsa, q_idx, k_idx,
         gate_nsa, gate_dsa, small) = _proj(xf, attn_norm[l], *_split_w_in(w_in[l]), tables, seq_len, t)
        kc, vc = _compress(k_cmp, v_cmp, cmpk_pos[l], cmpk_w1[l], cmpk_b1[l], cmpk_w2[l], cmpk_b2[l],
                           cmpv_pos[l], cmpv_w1[l], cmpv_b1[l], cmpv_w2[l], cmpv_b2[l], bsz, seq_len)
        y_nsa = _nsa(q_nsa, kc, vc, k_slc, v_slc, k_win, v_win, small, bsz, seq_len, t)
        y_dsa = _dsa(q_dsa, k_dsa, v_dsa, q_idx, k_idx, small, bsz, seq_len, t)
        xf = _merge(xf, y_nsa.reshape(bsz * seq_len, -1), y_dsa.reshape(bsz * seq_len, -1),
                    gate_nsa, gate_dsa, w_branch_nsa[l], w_branch_dsa[l], w_out[l], t)
        xf = _ffn(xf, ffn_norm[l], w_ffn_gate[l], w_ffn_up[l], w_ffn_down[l], final_norm,
                  final=(l == depth - 1), tm=t)
    return xf.reshape(bsz, seq_len, d)
```

```python
import functools

import numpy as np
import jax
import jax.numpy as jnp
from jax import lax
from jax.experimental import pallas as pl
from jax.experimental.pallas import tpu as pltpu

HEAD_DIM = 64
ROPE_DIM = HEAD_DIM // 4
ROPE_HALF = ROPE_DIM // 2
ROPE_THETA = 500000.0
NORM_EPS = 1e-6
ATTN_SCALE = HEAD_DIM ** -0.5
Q_HEADS = 8
Q_WIDTH = Q_HEADS * HEAD_DIM
NSA_KV_GROUPS = 2
NSA_GROUP_SIZE = Q_HEADS // NSA_KV_GROUPS
CMP_BLOCK = 32
CMP_STRIDE = 16
CMP_HIDDEN = 2 * HEAD_DIM
SLC_BLOCK = 64
N_SLC = 16
N_LOCAL_BLOCKS = 2
SEL_FORCE_SCORE = 1e4
WINDOW = 512
IDX_HEADS = 8
IDX_DIM = 64
IDX_WEIGHT_SCALE = (IDX_HEADS * IDX_DIM) ** -0.5
DSA_TOPK_MAX = 256
N_GATE = 3 * Q_HEADS
SMALL_W = N_GATE + IDX_HEADS
MAX_SLC_BLOCKS = 32

LANE = 128
SUBLANE = 8
V7X_VMEM_LIMIT = 56 * 1024 * 1024

NEG = -1e30
INT_MIN = -(2 ** 31)
KEY_NEG_INF = -2139095041
BF16 = jnp.bfloat16
F32 = jnp.float32

_NT = (((1,), (1,)), ((), ()))


def _dot(a, b):
    return jnp.dot(a, b, preferred_element_type=F32)


def _dot_nt(a, b):
    return lax.dot_general(a, b, _NT, preferred_element_type=F32)


def _const_spec(shape):
    nd = len(shape)
    return pl.BlockSpec(shape, lambda *_: (0,) * nd, pipeline_mode=pl.Buffered(1))


def _rmsnorm(x, g):
    ms = jnp.mean(x * x, axis=-1, keepdims=True)
    return x * lax.rsqrt(ms + NORM_EPS) * g


def _lane_tile(a, reps):
    return a if reps == 1 else jnp.concatenate([a] * reps, axis=1)


def _params(*sem):
    return pltpu.CompilerParams(dimension_semantics=sem, vmem_limit_bytes=V7X_VMEM_LIMIT)


_PROJ_SEGS = (
    ("t", Q_WIDTH, "ropeq", BF16),
    ("n", 128, "rope", F32),
    ("n", 128, "plain", F32),
    ("n", 256, "rope_onehot", BF16),
    ("t", 128, "plain", BF16),
    ("n", 256, "rope", BF16),
    ("t", 128, "plain", BF16),
    ("t", Q_WIDTH, "ropeq", BF16),
    ("n", Q_WIDTH, "rope", BF16),
    ("t", Q_WIDTH, "plain", BF16),
    ("t", IDX_HEADS * IDX_DIM, "rope", BF16),
    ("n", 128, "rope", BF16),
    ("n", 1024, "sig", F32),
    ("n", 1024, "sig", F32),
    ("t", SMALL_W, "small", F32),
)
_PROJ_NW = sum(s[1] for s in _PROJ_SEGS if s[0] == "n")
_PROJ_TW = sum(s[1] for s in _PROJ_SEGS if s[0] == "t")
_PROJ_CHUNK = 512


def _rope_n(y, c, sa, sb):
    pieces = []
    for j in range(y.shape[1] // LANE):
        yj = y[:, j * LANE:(j + 1) * LANE]
        pieces.append(yj * c + pltpu.roll(yj, ROPE_HALF, 1) * sa + pltpu.roll(yj, LANE - ROPE_HALF, 1) * sb)
    return pieces[0] if len(pieces) == 1 else jnp.concatenate(pieces, axis=1)


def _rope_t(y, cos, sin):
    pieces = []
    for h in range(y.shape[0] // HEAD_DIM):
        r0 = h * HEAD_DIM
        x1, x2 = y[r0:r0 + ROPE_HALF], y[r0 + ROPE_HALF:r0 + ROPE_DIM]
        pieces += [x1 * cos - x2 * sin, x2 * cos + x1 * sin, y[r0 + ROPE_DIM:r0 + HEAD_DIM]]
    return jnp.concatenate(pieces, axis=0)


def _proj_kernel(x_ref, g_ref, wn_ref, wt_ref, c_ref, sa_ref, sb_ref, ct_ref, st_ref, *out_refs, npos):
    tm = x_ref.shape[0]
    h = _rmsnorm(x_ref[...], g_ref[...]).astype(BF16)
    c, sa, sb = c_ref[...], sa_ref[...], sb_ref[...]
    cos_t, sin_t = ct_ref[0], st_ref[0]
    ncol = trow = 0
    for (layout, width, kind, _), o_ref in zip(_PROJ_SEGS, out_refs):
        if layout == "t":
            y = _dot_nt(wt_ref[trow:trow + width, :], h)
            trow += width
            if kind in ("rope", "ropeq"):
                y = _rope_t(y, cos_t, sin_t)
            if kind == "ropeq":
                y = y * ATTN_SCALE
            if kind == "small":
                row = lax.broadcasted_iota(jnp.int32, y.shape, 0)
                y = jnp.where(row < N_GATE, jax.nn.sigmoid(y), y * IDX_WEIGHT_SCALE)
            o_ref[0] = y.astype(o_ref.dtype)
            continue
        for c0 in range(0, width, _PROJ_CHUNK):
            cw = min(_PROJ_CHUNK, width - c0)
            y = _dot(h, wn_ref[:, ncol + c0:ncol + c0 + cw])
            if kind in ("rope", "rope_onehot"):
                y = _rope_n(y, c, sa, sb)
            if kind == "rope_onehot":
                pos = (pl.program_id(0) % npos) * tm + lax.broadcasted_iota(jnp.int32, y.shape, 0)
                lane = lax.broadcasted_iota(jnp.int32, y.shape, 1) & (LANE - 1)
                y = jnp.where(lane == HEAD_DIM + pos // SLC_BLOCK, 1.0, y)
            if kind == "sig":
                y = jax.nn.sigmoid(y)
            o_ref[:, c0:c0 + cw] = y.astype(o_ref.dtype)
        ncol += width


def _split_w_in(w):
    d = w.shape[0]
    kvw = NSA_KV_GROUPS * HEAD_DIM
    sizes = (Q_WIDTH, kvw, kvw, kvw, kvw, kvw, kvw, N_GATE, Q_WIDTH, Q_WIDTH, Q_WIDTH,
             IDX_HEADS * IDX_DIM, IDX_DIM, IDX_HEADS, d, d)
    assert sum(sizes) == w.shape[1]
    offs = np.concatenate([[0], np.cumsum(sizes)])
    (nsa_q, k_cmp, v_cmp, k_slc, v_slc, k_win, v_win, nsa_gate, dsa_q, dsa_k, dsa_v,
     idx_q, idx_k, idx_w, gate_nsa, gate_dsa) = [w[:, offs[i]:offs[i + 1]] for i in range(16)]

    def pad_groups(a):
        a = a.reshape(d, NSA_KV_GROUPS, HEAD_DIM)
        return jnp.concatenate([a, jnp.zeros_like(a)], axis=2).reshape(d, NSA_KV_GROUPS * LANE)

    idx_k_pad = jnp.concatenate([idx_k, jnp.zeros_like(idx_k)], axis=1)
    w_n = jnp.concatenate([k_cmp, v_cmp, pad_groups(k_slc), pad_groups(k_win), dsa_k, idx_k_pad,
                           gate_nsa, gate_dsa], axis=1)
    w_t = jnp.concatenate([nsa_q, v_slc, v_win, dsa_q, dsa_v, idx_q, nsa_gate, idx_w], axis=1).T
    assert w_n.shape[1] == _PROJ_NW and w_t.shape[0] == _PROJ_TW
    return w_n.astype(BF16), w_t.astype(BF16)


def _proj(xf, g, w_n, w_t, tables, seq_len, tm):
    t, d = xf.shape
    npos = seq_len // tm
    row = lambda i: (i, 0)
    tab_n = pl.BlockSpec((tm, LANE), lambda i: (i % npos, 0))
    tab_t = pl.BlockSpec((1, ROPE_HALF, tm), lambda i: (i % npos, 0, 0))
    out_specs, out_shape = [], []
    for layout, width, _, dt in _PROJ_SEGS:
        if layout == "t":
            out_specs.append(pl.BlockSpec((1, width, tm), lambda i: (i, 0, 0)))
            out_shape.append(jax.ShapeDtypeStruct((t // tm, width, tm), dt))
        else:
            out_specs.append(pl.BlockSpec((tm, width), row))
            out_shape.append(jax.ShapeDtypeStruct((t, width), dt))
    return pl.pallas_call(
        functools.partial(_proj_kernel, npos=npos),
        grid=(t // tm,),
        in_specs=[pl.BlockSpec((tm, d), row), _const_spec((1, d)), _const_spec((d, _PROJ_NW)),
                  _const_spec((_PROJ_TW, d)), tab_n, tab_n, tab_n, tab_t, tab_t],
        out_specs=out_specs,
        out_shape=out_shape,
        compiler_params=_params("parallel"),
        name="proj",
    )(xf, g.reshape(1, d), w_n, w_t, *tables)


def _cmp_kernel(kb_ref, vb_ref, pk_ref, w1k_ref, b1k_ref, w2k_ref, b2k_ref,
                pv_ref, w1v_ref, b1v_ref, w2v_ref, b2v_ref, kc_ref, vc_ref, *, n_cmp):
    def hidden(xb_ref, pos_ref, w1_ref, b1_ref):
        x = xb_ref[0, 0]
        nb, half = x.shape
        a = _dot((x + pos_ref[:, :half]).astype(BF16), w1_ref[:half, :])
        b = _dot((x + pos_ref[:, half:]).astype(BF16), w1_ref[half:, :])
        return jax.nn.gelu(a + pltpu.roll(b, nb - 1, 0) + b1_ref[...]).astype(BF16)

    hk = hidden(kb_ref, pk_ref, w1k_ref, b1k_ref)
    kc = _dot(hk, w2k_ref[...]) + b2k_ref[...]
    rows = lax.broadcasted_iota(jnp.int32, kc.shape, 0)
    kc_ref[0, 0] = jnp.where(rows < n_cmp, kc, 0.0).astype(kc_ref.dtype)
    hv = hidden(vb_ref, pv_ref, w1v_ref, b1v_ref)
    vc = _dot_nt(w2v_ref[...], hv) + b2v_ref[...]
    cols = lax.broadcasted_iota(jnp.int32, vc.shape, 1)
    vc_ref[0, 0] = jnp.where(cols < n_cmp, vc, 0.0).astype(vc_ref.dtype)


def _compress(k_cmp, v_cmp, pk, w1k, b1k, w2k, b2k, pv, w1v, b1v, w2v, b2v, bsz, seq_len):
    g, hd = NSA_KV_GROUPS, HEAD_DIM
    nb = seq_len // CMP_STRIDE
    n_cmp = (seq_len - CMP_BLOCK) // CMP_STRIDE + 1
    flat = CMP_STRIDE * hd

    def strides(a):
        return a.reshape(bsz, nb, CMP_STRIDE, g, hd).transpose(0, 3, 1, 2, 4).reshape(bsz, g, nb, flat)

    k_w = (pk.reshape(1, CMP_BLOCK * hd), w1k.astype(BF16), b1k.reshape(1, -1),
           jnp.concatenate([w2k, jnp.zeros_like(w2k)], axis=1).astype(BF16),
           jnp.concatenate([b2k, jnp.zeros_like(b2k)]).reshape(1, -1))
    v_w = (pv.reshape(1, CMP_BLOCK * hd), w1v.astype(BF16), b1v.reshape(1, -1),
           w2v.T.astype(BF16), b2v.reshape(-1, 1))
    blk = pl.BlockSpec((1, 1, nb, flat), lambda b, gi: (b, gi, 0, 0))
    common = [_const_spec((1, 2 * flat)), _const_spec((2 * flat, CMP_HIDDEN)), _const_spec((1, CMP_HIDDEN))]
    return pl.pallas_call(
        functools.partial(_cmp_kernel, n_cmp=n_cmp),
        grid=(bsz, g),
        in_specs=[blk, blk] + common + [_const_spec((CMP_HIDDEN, LANE)), _const_spec((1, LANE))]
        + common + [_const_spec((hd, CMP_HIDDEN)), _const_spec((hd, 1))],
        out_specs=[pl.BlockSpec((1, 1, nb, LANE), lambda b, gi: (b, gi, 0, 0)),
                   pl.BlockSpec((1, 1, hd, nb), lambda b, gi: (b, gi, 0, 0))],
        out_shape=[jax.ShapeDtypeStruct((bsz, g, nb, LANE), BF16),
                   jax.ShapeDtypeStruct((bsz, g, hd, nb), BF16)],
        compiler_params=_params("parallel", "parallel"),
        name="compress",
    )(strides(k_cmp), strides(v_cmp), *k_w, *v_w)


def _fold(x, op):
    parts = [x[r * SUBLANE:(r + 1) * SUBLANE] for r in range(x.shape[0] // SUBLANE)]
    while len(parts) > 1:
        pairs = [op(parts[k], parts[k + 1]) for k in range(0, len(parts) - 1, 2)]
        parts = pairs + ([parts[-1]] if len(parts) % 2 else [])
    return parts[0]


def _nsa_kernel(q_ref, kc_ref, vc_ref, ks_ref, vs_ref, kw_ref, vw_ref, sm_ref, ovl_ref, o_ref,
                qm_ref, ss_ref, sw_ref, as_ref, *, t, seq_len):
    r_heads = NSA_GROUP_SIZE
    gi = pl.program_id(1)
    i = pl.program_id(2)
    t0 = i * t
    rows = r_heads * t
    hd = HEAD_DIM

    for r in range(r_heads):
        qm_ref[0:hd, r * t:(r + 1) * t] = q_ref[0, 0, r * hd:(r + 1) * hd, :]
    qm_ref[hd:, :] = jnp.zeros((LANE - hd, rows), BF16)
    tok = t0 + lax.broadcasted_iota(jnp.int32, (1, t), 1)
    tok_r = _lane_tile(tok, r_heads)

    nb = kc_ref.shape[2]
    s_c = _dot(kc_ref[0, 0], qm_ref[...])
    c_end = lax.broadcasted_iota(jnp.int32, (nb, 1), 0) * CMP_STRIDE + (CMP_BLOCK - 1)
    mask_c = c_end <= tok_r
    s_c = jnp.where(mask_c, s_c, NEG)
    s_c = s_c - jnp.max(s_c, axis=0, keepdims=True)
    p_c = jnp.where(mask_c, jnp.exp(s_c), 0.0)
    p_c = (p_c / jnp.maximum(jnp.sum(p_c, axis=0, keepdims=True), 1e-30)).astype(BF16)
    o_c = _dot(vc_ref[0, 0], p_c)
    imp4 = _dot(ovl_ref[...], p_c)
    imp = imp4[:, 0:t]
    for r in range(1, r_heads):
        imp = imp + imp4[:, r * t:(r + 1) * t]

    n_slc = seq_len // SLC_BLOCK
    n_sel = min(N_SLC, n_slc)
    blk = lax.broadcasted_iota(jnp.int32, (MAX_SLC_BLOCKS, 1), 0)
    dist = (tok // SLC_BLOCK) - blk
    forced = (blk == 0) | ((dist >= 0) & (dist < N_LOCAL_BLOCKS))
    score = jnp.where(forced, SEL_FORCE_SCORE, imp)
    score = jnp.where(blk * SLC_BLOCK <= tok, score, -jnp.inf)
    groups = [score[v * SUBLANE:(v + 1) * SUBLANE] for v in range(MAX_SLC_BLOCKS // SUBLANE)]
    ranks = [jnp.zeros((SUBLANE, t), F32) for _ in groups]
    sub = lax.broadcasted_iota(jnp.int32, (SUBLANE, 1), 0)
    for j in range(n_slc):
        row = score[j:j + 1]
        for v, sv in enumerate(groups):
            ge = jnp.where(row >= sv, 1.0, 0.0)
            gt = jnp.where(row > sv, 1.0, 0.0)
            if v * SUBLANE > j:
                beats = ge
            elif (v + 1) * SUBLANE <= j + 1:
                beats = gt
            else:
                beats = jnp.where(sub + v * SUBLANE > j, ge, gt)
            ranks[v] = ranks[v] + beats
    selbias = jnp.where(jnp.concatenate(ranks, axis=0) < n_sel, 0.0, NEG).astype(BF16)
    for r in range(r_heads):
        qm_ref[hd:hd + MAX_SLC_BLOCKS, r * t:(r + 1) * t] = selbias
    key_off = lax.broadcasted_iota(jnp.int32, (t, 1), 0)
    q_off = lax.broadcasted_iota(jnp.int32, (1, t), 1)
    causal = _lane_tile(jnp.where(key_off <= q_off, 0.0, NEG), r_heads)
    neg8 = jnp.full((SUBLANE, rows), NEG, F32)

    def key_tile(k_ref_, c):
        return k_ref_[0, pl.ds(pl.multiple_of(c * t, t), t), :]

    def slc_scores(c, mx):
        s = _dot(key_tile(ks_ref, c), qm_ref[...])
        ss_ref[c] = s
        return jnp.maximum(mx, _fold(s, jnp.maximum))

    mx = lax.fori_loop(0, i, slc_scores, neg8)
    s = _dot(key_tile(ks_ref, i), qm_ref[...]) + causal
    ss_ref[i] = s
    m_s = jnp.max(jnp.maximum(mx, _fold(s, jnp.maximum)), axis=0, keepdims=True)

    n_back = -(-WINDOW // t)
    win_tiles = []
    mxw = neg8
    for back in range(n_back, -1, -1):
        if back:
            lo_edge = back * t - WINDOW
            bias = _lane_tile(jnp.where(key_off - q_off > jnp.where(i >= back, lo_edge, t), 0.0, NEG), r_heads)
        else:
            bias = causal
        c = jnp.maximum(i - back, 0)
        s = _dot(key_tile(kw_ref, c), qm_ref[...]) + bias
        sw_ref[n_back - back] = s
        mxw = jnp.maximum(mxw, _fold(s, jnp.maximum))
        win_tiles.append(c)
    m_w = jnp.max(mxw, axis=0, keepdims=True)

    as_ref[...] = jnp.zeros(as_ref.shape, F32)

    def slc_values(c, l8):
        p = jnp.exp(ss_ref[c] - m_s)
        as_ref[...] += _dot(vs_ref[0, c], p.astype(BF16))
        return l8 + _fold(p, jnp.add)

    l8 = lax.fori_loop(0, i + 1, slc_values, jnp.zeros((SUBLANE, rows), F32))
    o_s = as_ref[...] * (1.0 / jnp.sum(l8, axis=0, keepdims=True))

    l8 = jnp.zeros((SUBLANE, rows), F32)
    o_w = jnp.zeros((hd, rows), F32)
    for n, c in enumerate(win_tiles):
        p = jnp.exp(sw_ref[n] - m_w)
        o_w = o_w + _dot(vw_ref[0, c], p.astype(BF16))
        l8 = l8 + _fold(p, jnp.add)
    o_w = o_w * (1.0 / jnp.sum(l8, axis=0, keepdims=True))

    gates = sm_ref[0, 0]

    def gate(branch, r):
        rows_g = [gates[branch * Q_HEADS + g * r_heads + r:branch * Q_HEADS + g * r_heads + r + 1]
                  for g in range(NSA_KV_GROUPS)]
        out = rows_g[0]
        for g in range(1, NSA_KV_GROUPS):
            out = jnp.where(gi == g, rows_g[g], out)
        return out

    heads = []
    for r in range(r_heads):
        sl = slice(r * t, (r + 1) * t)
        heads.append(gate(0, r) * o_c[:, sl] + gate(1, r) * o_s[:, sl] + gate(2, r) * o_w[:, sl])
    o_ref[0] = jnp.concatenate(heads, axis=0).T.astype(o_ref.dtype)


def _overlap_t(seq_len):
    nb = seq_len // CMP_STRIDE
    n_cmp = (seq_len - CMP_BLOCK) // CMP_STRIDE + 1
    n_slc = seq_len // SLC_BLOCK
    c_start = np.arange(n_cmp)[None, :] * CMP_STRIDE
    s_start = np.arange(n_slc)[:, None] * SLC_BLOCK
    ov = np.minimum(c_start + CMP_BLOCK, s_start + SLC_BLOCK) - np.maximum(c_start, s_start)
    ovl = np.zeros((MAX_SLC_BLOCKS, nb), np.float32)
    ovl[:n_slc, :n_cmp] = np.clip(ov, 0, None) / CMP_BLOCK
    return jnp.asarray(ovl, BF16)


def _nsa(q_t, kc, vc_t, ks, vs_t, kw, vw_t, small_t, bsz, seq_len, t):
    g, r, hd = NSA_KV_GROUPS, NSA_GROUP_SIZE, HEAD_DIM
    assert seq_len // SLC_BLOCK <= MAX_SLC_BLOCKS
    nb = seq_len // CMP_STRIDE
    nc = seq_len // t
    rows = r * t
    ovl = _overlap_t(seq_len)
    k_spec = pl.BlockSpec((1, seq_len, LANE), lambda b, gi, i: (b, 0, gi))
    v_spec = pl.BlockSpec((1, nc, hd, t), lambda b, gi, i: (b, 0, gi, 0))
    return pl.pallas_call(
        functools.partial(_nsa_kernel, t=t, seq_len=seq_len),
        grid=(bsz, g, nc),
        in_specs=[pl.BlockSpec((1, 1, r * hd, t), lambda b, gi, i: (b, i, gi, 0)),
                  pl.BlockSpec((1, 1, nb, LANE), lambda b, gi, i: (b, gi, 0, 0)),
                  pl.BlockSpec((1, 1, hd, nb), lambda b, gi, i: (b, gi, 0, 0)),
                  k_spec, v_spec, k_spec, v_spec,
                  pl.BlockSpec((1, 1, SMALL_W, t), lambda b, gi, i: (b, i, 0, 0)),
                  _const_spec(ovl.shape)],
        out_specs=pl.BlockSpec((1, t, r * hd), lambda b, gi, i: (b, i, gi)),
        out_shape=jax.ShapeDtypeStruct((bsz, seq_len, Q_WIDTH), BF16),
        scratch_shapes=[pltpu.VMEM((LANE, rows), BF16), pltpu.VMEM((nc, t, rows), F32),
                        pltpu.VMEM((-(-WINDOW // t) + 1, t, rows), F32), pltpu.VMEM((hd, rows), F32)],
        compiler_params=_params("parallel", "parallel", "parallel"),
        name="nsa_attn",
    )(q_t.reshape(bsz, nc, Q_WIDTH, t), kc, vc_t, ks.reshape(bsz, seq_len, -1), vs_t.reshape(bsz, nc, g * hd, t),
      kw.reshape(bsz, seq_len, -1), vw_t.reshape(bsz, nc, g * hd, t), small_t.reshape(bsz, nc, SMALL_W, t), ovl)


def _dsa_kernel(q_ref, k_ref, v_ref, iq_ref, ik_ref, sm_ref, o_ref,
                qm_ref, iqm_ref, sc_ref, s_ref, acc_ref, *, t, seq_len, k_top):
    i = pl.program_id(1)
    t0 = i * t
    nck = i + 1
    hd = HEAD_DIM
    tok = t0 + lax.broadcasted_iota(jnp.int32, (1, t), 1)

    zeros = jnp.zeros((hd, t), BF16)
    for h in range(Q_HEADS):
        qh = q_ref[0, 0, h * hd:(h + 1) * hd, :]
        qm_ref[h] = jnp.concatenate([qh, zeros] if h % 2 == 0 else [zeros, qh], axis=0)
        iqm_ref[h] = jnp.concatenate([iq_ref[0, 0, h * hd:(h + 1) * hd, :], zeros], axis=0)
    small = sm_ref[0, 0]

    def key_pos(c):
        return pl.multiple_of(c * t, t) + lax.broadcasted_iota(jnp.int32, (t, 1), 0)

    def score_body(c, carry):
        ik = ik_ref[0, pl.ds(pl.multiple_of(c * t, t), t), :]
        sc = jnp.zeros((t, t), F32)
        for h in range(IDX_HEADS):
            sc = sc + jnp.maximum(_dot(ik, iqm_ref[h]), 0.0) * small[N_GATE + h:N_GATE + h + 1]
        sc_ref[c] = jnp.where(key_pos(c) <= tok, sc, -jnp.inf)
        return carry

    lax.fori_loop(0, nck, score_body, 0)

    def count(indicator):
        def body(c, acc):
            return acc + _fold(indicator(sc_ref[c], key_pos(c)), jnp.add)
        acc = lax.fori_loop(0, nck, body, jnp.zeros((SUBLANE, t), F32))
        return jnp.sum(acc, axis=0, keepdims=True)

    kf = float(k_top)

    def decode(code):
        return lax.bitcast_convert_type(jnp.where(code < 0, code ^ jnp.int32(0x7FFFFFFF), code), F32)

    def reached(code):
        val = decode(code)
        return (count(lambda sc, pos: jnp.where(sc >= val, 1.0, 0.0)) >= kf) | (code <= KEY_NEG_INF)

    zero = jnp.zeros((1, t), jnp.int32)
    base = jnp.where(reached(zero), zero, jnp.full((1, t), INT_MIN, jnp.int32))

    def bit_body(b, base):
        cand = base + lax.shift_left(jnp.int32(1), 30 - b)
        return jnp.where(reached(cand), cand, base)

    kth_code = lax.fori_loop(0, 31, bit_body, base)
    kth = decode(jnp.maximum(kth_code, KEY_NEG_INF))

    n_ge = count(lambda sc, pos: jnp.where(sc >= kth, 1.0, 0.0))
    n_gt = count(lambda sc, pos: jnp.where(sc > kth, 1.0, 0.0))
    tie = (n_ge > kf) & (kth > -jnp.inf)

    def tie_cut():
        room = kf - n_gt
        top_bit = seq_len.bit_length() - 1

        def pos_body(b, x):
            cand = x + lax.shift_left(jnp.int32(1), top_bit - b)
            n = count(lambda sc, pos: jnp.where(sc == kth, jnp.where(pos < cand, 1.0, 0.0), 0.0))
            return jnp.where(n <= room, cand, x)

        x = lax.fori_loop(0, top_bit + 1, pos_body, zero)
        return jnp.where(tie, x, seq_len)

    cut = lax.cond(jnp.max(jnp.where(tie, 1.0, 0.0)) > 0.0, tie_cut,
                   lambda: jnp.full((1, t), seq_len, jnp.int32))
    cut = jnp.where(kth > -jnp.inf, cut, 0)

    def att_scores(c, mx):
        k0 = pl.multiple_of(c * t, t)
        sc = sc_ref[c]
        bias = jnp.where(sc > kth, 0.0, jnp.where(sc == kth, jnp.where(key_pos(c) < cut, 0.0, NEG), NEG))
        out = []
        for h in range(Q_HEADS):
            s = _dot(k_ref[0, pl.ds(k0, t), (h // 2) * LANE:(h // 2 + 1) * LANE], qm_ref[h]) + bias
            s_ref[h, c] = s
            out.append(jnp.maximum(mx[h], _fold(s, jnp.maximum)))
        return tuple(out)

    mx = lax.fori_loop(0, nck, att_scores, (jnp.full((SUBLANE, t), NEG, F32),) * Q_HEADS)
    m = [jnp.max(mx[h], axis=0, keepdims=True) for h in range(Q_HEADS)]

    acc_ref[...] = jnp.zeros(acc_ref.shape, F32)

    def att_values(c, l8):
        out = []
        for h in range(Q_HEADS):
            p = jnp.exp(s_ref[h, c] - m[h])
            acc_ref[h] += _dot(v_ref[0, c, h * hd:(h + 1) * hd, :], p.astype(BF16))
            out.append(l8[h] + _fold(p, jnp.add))
        return tuple(out)

    l8 = lax.fori_loop(0, nck, att_values, (jnp.zeros((SUBLANE, t), F32),) * Q_HEADS)
    out = jnp.concatenate([acc_ref[h] * (1.0 / jnp.sum(l8[h], axis=0, keepdims=True))
                           for h in range(Q_HEADS)], axis=0)
    o_ref[0] = out.T.astype(o_ref.dtype)


def _dsa(q_t, k, v_t, iq_t, ik, small_t, bsz, seq_len, t):
    k_top = min(DSA_TOPK_MAX, seq_len // 4)
    assert t >= k_top
    nc = seq_len // t
    qt_spec = pl.BlockSpec((1, 1, Q_WIDTH, t), lambda b, i: (b, i, 0, 0))
    return pl.pallas_call(
        functools.partial(_dsa_kernel, t=t, seq_len=seq_len, k_top=k_top),
        grid=(bsz, nc),
        in_specs=[qt_spec,
                  pl.BlockSpec((1, seq_len, Q_WIDTH), lambda b, i: (b, 0, 0)),
                  pl.BlockSpec((1, nc, Q_WIDTH, t), lambda b, i: (b, 0, 0, 0)),
                  qt_spec,
                  pl.BlockSpec((1, seq_len, LANE), lambda b, i: (b, 0, 0)),
                  pl.BlockSpec((1, 1, SMALL_W, t), lambda b, i: (b, i, 0, 0))],
        out_specs=pl.BlockSpec((1, t, Q_WIDTH), lambda b, i: (b, i, 0)),
        out_shape=jax.ShapeDtypeStruct((bsz, seq_len, Q_WIDTH), BF16),
        scratch_shapes=[pltpu.VMEM((Q_HEADS, LANE, t), BF16), pltpu.VMEM((IDX_HEADS, LANE, t), BF16),
                        pltpu.VMEM((nc, t, t), F32), pltpu.VMEM((Q_HEADS, nc, t, t), F32),
                        pltpu.VMEM((Q_HEADS, HEAD_DIM, t), F32)],
        compiler_params=_params("parallel", "parallel"),
        name="dsa_attn",
    )(q_t.reshape(bsz, nc, Q_WIDTH, t), k.reshape(bsz, seq_len, -1), v_t.reshape(bsz, nc, Q_WIDTH, t),
      iq_t.reshape(bsz, nc, Q_WIDTH, t), ik.reshape(bsz, seq_len, -1), small_t.reshape(bsz, nc, SMALL_W, t))


def _merge_kernel(x_ref, yn_ref, yd_ref, gn_ref, gd_ref, wbn_ref, wbd_ref, wo_ref, o_ref):
    merged = gn_ref[...] * _dot(yn_ref[...], wbn_ref[...]) + gd_ref[...] * _dot(yd_ref[...], wbd_ref[...])
    o_ref[...] = x_ref[...] + _dot(merged.astype(BF16), wo_ref[...])


def _merge(xf, y_nsa, y_dsa, gate_nsa, gate_dsa, wbn, wbd, wo, tm):
    t, d = xf.shape
    row = lambda i: (i, 0)
    wn = y_nsa.shape[1]
    return pl.pallas_call(
        _merge_kernel,
        grid=(t // tm,),
        in_specs=[pl.BlockSpec((tm, d), row), pl.BlockSpec((tm, wn), row), pl.BlockSpec((tm, wn), row),
                  pl.BlockSpec((tm, d), row), pl.BlockSpec((tm, d), row),
                  _const_spec((wn, d)), _const_spec((wn, d)), _const_spec((d, d))],
        out_specs=pl.BlockSpec((tm, d), row),
        out_shape=jax.ShapeDtypeStruct((t, d), F32),
        compiler_params=_params("parallel"),
        name="merge",
    )(xf, y_nsa, y_dsa, gate_nsa, gate_dsa, wbn.astype(BF16), wbd.astype(BF16), wo.astype(BF16))


def _ffn_kernel(x_ref, g_ref, wg_ref, wu_ref, wd_ref, fg_ref, o_ref, *, final):
    x = x_ref[...]
    h = _rmsnorm(x, g_ref[...]).astype(BF16)
    z = (jax.nn.silu(_dot(h, wg_ref[...])) * _dot(h, wu_ref[...])).astype(BF16)
    y = x + _dot(z, wd_ref[...])
    o_ref[...] = _rmsnorm(y, fg_ref[...]) if final else y


def _ffn(xf, g, wg, wu, wd, final_g, final, tm):
    t, d = xf.shape
    f = wg.shape[1]
    row = lambda i: (i, 0)
    return pl.pallas_call(
        functools.partial(_ffn_kernel, final=final),
        grid=(t // tm,),
        in_specs=[pl.BlockSpec((tm, d), row), _const_spec((1, d)), _const_spec((d, f)), _const_spec((d, f)),
                  _const_spec((f, d)), _const_spec((1, d))],
        out_specs=pl.BlockSpec((tm, d), row),
        out_shape=jax.ShapeDtypeStruct((t, d), F32),
        compiler_params=_params("parallel"),
        name="ffn",
    )(xf, g.reshape(1, d), wg.astype(BF16), wu.astype(BF16), wd.astype(BF16), final_g.reshape(1, d))


def _rope_tables(seq_len, tm):
    pos = jnp.arange(seq_len, dtype=F32)
    inv_freq = ROPE_THETA ** (-jnp.arange(0, ROPE_DIM, 2, dtype=F32) / ROPE_DIM)
    ang = pos[:, None] * inv_freq[None, :]
    cos, sin = jnp.cos(ang), jnp.sin(ang)
    rest = HEAD_DIM - ROPE_DIM
    one, zero = jnp.ones((seq_len, rest), F32), jnp.zeros((seq_len, rest), F32)
    zh = jnp.zeros((seq_len, ROPE_HALF), F32)
    two = lambda a: jnp.concatenate([a, a], axis=1)
    tile_t = lambda a: a.reshape(seq_len // tm, tm, ROPE_HALF).transpose(0, 2, 1)
    return (two(jnp.concatenate([cos, cos, one], axis=1)),
            two(jnp.concatenate([zh, sin, zero], axis=1)),
            two(jnp.concatenate([-sin, zh, zero], axis=1)),
            tile_t(cos), tile_t(sin))


def _tile(seq_len):
    return min(256, seq_len)


def kernel(x, attn_norm, w_in, cmpk_pos, cmpk_w1, cmpk_b1, cmpk_w2, cmpk_b2, cmpv_pos, cmpv_w1, cmpv_b1,
           cmpv_w2, cmpv_b2, w_branch_nsa, w_branch_dsa, w_out, ffn_norm, w_ffn_gate, w_ffn_up, w_ffn_down,
           final_norm):
    bsz, seq_len, d = x.shape
    depth = w_in.shape[0]
    t = _tile(seq_len)
    tables = _rope_tables(seq_len, t)
    xf = x.reshape(bsz * seq_len, d)
    for l in range(depth):
        (q_nsa, k_cmp, v_cmp, k_slc, v_slc, k_win, v_win, q_dsa, k_dsa, v_dsa, q_idx, k_idx,
         gate_nsa, gate_dsa, small) = _proj(xf, attn_norm[l], *_split_w_in(w_in[l]), tables, seq_len, t)
        kc, vc = _compress(k_cmp, v_cmp, cmpk_pos[l], cmpk_w1[l], cmpk_b1[l], cmpk_w2[l], cmpk_b2[l],
                           cmpv_pos[l], cmpv_w1[l], cmpv_b1[l], cmpv_w2[l], cmpv_b2[l], bsz, seq_len)
        y_nsa = _nsa(q_nsa, kc, vc, k_slc, v_slc, k_win, v_win, small, bsz, seq_len, t)
        y_dsa = _dsa(q_dsa, k_dsa, v_dsa, q_idx, k_idx, small, bsz, seq_len, t)
        xf = _merge(xf, y_nsa.reshape(bsz * seq_len, -1), y_dsa.reshape(bsz * seq_len, -1),
                    gate_nsa, gate_dsa, w_branch_nsa[l], w_branch_dsa[l], w_out[l], t)
        xf = _ffn(xf, ffn_norm[l], w_ffn_gate[l], w_ffn_up[l], w_ffn_down[l], final_norm,
                  final=(l == depth - 1), tm=t)
    return xf.reshape(bsz, seq_len, d)
```

```python
import functools

import numpy as np
import jax
import jax.numpy as jnp
from jax import lax
from jax.experimental import pallas as pl
from jax.experimental.pallas import tpu as pltpu

HEAD_DIM = 64
ROPE_DIM = HEAD_DIM // 4
ROPE_HALF = ROPE_DIM // 2
ROPE_THETA = 500000.0
NORM_EPS = 1e-6
ATTN_SCALE = HEAD_DIM ** -0.5
Q_HEADS = 8
Q_WIDTH = Q_HEADS * HEAD_DIM
NSA_KV_GROUPS = 2
NSA_GROUP_SIZE = Q_HEADS // NSA_KV_GROUPS
CMP_BLOCK = 32
CMP_STRIDE = 16
CMP_HIDDEN = 2 * HEAD_DIM
SLC_BLOCK = 64
N_SLC = 16
N_LOCAL_BLOCKS = 2
SEL_FORCE_SCORE = 1e4
WINDOW = 512
IDX_HEADS = 8
IDX_DIM = 64
IDX_WEIGHT_SCALE = (IDX_HEADS * IDX_DIM) ** -0.5
DSA_TOPK_MAX = 256
N_GATE = 3 * Q_HEADS
SMALL_W = N_GATE + IDX_HEADS
MAX_SLC_BLOCKS = 32

LANE = 128
SUBLANE = 8
V7X_VMEM_LIMIT = 56 * 1024 * 1024

NEG = -1e30
INT_MIN = -(2 ** 31)
KEY_NEG_INF = -2139095041
BF16 = jnp.bfloat16
F32 = jnp.float32

_NT = (((1,), (1,)), ((), ()))


def _dot(a, b):
    return jnp.dot(a, b, preferred_element_type=F32)


def _dot_nt(a, b):
    return lax.dot_general(a, b, _NT, preferred_element_type=F32)


def _const_spec(shape):
    nd = len(shape)
    return pl.BlockSpec(shape, lambda *_: (0,) * nd, pipeline_mode=pl.Buffered(1))


def _rmsnorm(x, g):
    ms = jnp.mean(x * x, axis=-1, keepdims=True)
    return x * lax.rsqrt(ms + NORM_EPS) * g


def _lane_tile(a, reps):
    return a if reps == 1 else jnp.concatenate([a] * reps, axis=1)


def _params(*sem):
    return pltpu.CompilerParams(dimension_semantics=sem, vmem_limit_bytes=V7X_VMEM_LIMIT)


_PROJ_SEGS = (
    ("t", Q_WIDTH, "ropeq", BF16),
    ("n", 128, "rope", F32),
    ("n", 128, "plain", F32),
    ("n", 256, "rope_onehot", BF16),
    ("t", 128, "plain", BF16),
    ("n", 256, "rope", BF16),
    ("t", 128, "plain", BF16),
    ("t", Q_WIDTH, "ropeq", BF16),
    ("n", Q_WIDTH, "rope", BF16),
    ("t", Q_WIDTH, "plain", BF16),
    ("t", IDX_HEADS * IDX_DIM, "rope", BF16),
    ("n", 128, "rope", BF16),
    ("n", 1024, "sig", F32),
    ("n", 1024, "sig", F32),
    ("t", SMALL_W, "small", F32),
)
_PROJ_NW = sum(s[1] for s in _PROJ_SEGS if s[0] == "n")
_PROJ_TW = sum(s[1] for s in _PROJ_SEGS if s[0] == "t")
_PROJ_CHUNK = 512


def _rope_n(y, c, sa, sb):
    pieces = []
    for j in range(y.shape[1] // LANE):
        yj = y[:, j * LANE:(j + 1) * LANE]
        pieces.append(yj * c + pltpu.roll(yj, ROPE_HALF, 1) * sa + pltpu.roll(yj, LANE - ROPE_HALF, 1) * sb)
    return pieces[0] if len(pieces) == 1 else jnp.concatenate(pieces, axis=1)


def _rope_t(y, cos, sin):
    pieces = []
    for h in range(y.shape[0] // HEAD_DIM):
        r0 = h * HEAD_DIM
        x1, x2 = y[r0:r0 + ROPE_HALF], y[r0 + ROPE_HALF:r0 + ROPE_DIM]
        pieces += [x1 * cos - x2 * sin, x2 * cos + x1 * sin, y[r0 + ROPE_DIM:r0 + HEAD_DIM]]
    return jnp.concatenate(pieces, axis=0)


def _proj_kernel(x_ref, g_ref, wn_ref, wt_ref, c_ref, sa_ref, sb_ref, ct_ref, st_ref, *out_refs, npos):
    tm = x_ref.shape[0]
    h = _rmsnorm(x_ref[...], g_ref[...]).astype(BF16)
    c, sa, sb = c_ref[...], sa_ref[...], sb_ref[...]
    cos_t, sin_t = ct_ref[0], st_ref[0]
    ncol = trow = 0
    for (layout, width, kind, _), o_ref in zip(_PROJ_SEGS, out_refs):
        if layout == "t":
            y = _dot_nt(wt_ref[trow:trow + width, :], h)
            trow += width
            if kind in ("rope", "ropeq"):
                y = _rope_t(y, cos_t, sin_t)
            if kind == "ropeq":
                y = y * ATTN_SCALE
            if kind == "small":
                row = lax.broadcasted_iota(jnp.int32, y.shape, 0)
                y = jnp.where(row < N_GATE, jax.nn.sigmoid(y), y * IDX_WEIGHT_SCALE)
            o_ref[0] = y.astype(o_ref.dtype)
            continue
        for c0 in range(0, width, _PROJ_CHUNK):
            cw = min(_PROJ_CHUNK, width - c0)
            y = _dot(h, wn_ref[:, ncol + c0:ncol + c0 + cw])
            if kind in ("rope", "rope_onehot"):
                y = _rope_n(y, c, sa, sb)
            if kind == "rope_onehot":
                pos = (pl.program_id(0) % npos) * tm + lax.broadcasted_iota(jnp.int32, y.shape, 0)
                lane = lax.broadcasted_iota(jnp.int32, y.shape, 1) & (LANE - 1)
                y = jnp.where(lane == HEAD_DIM + pos // SLC_BLOCK, 1.0, y)
            if kind == "sig":
                y = jax.nn.sigmoid(y)
            o_ref[:, c0:c0 + cw] = y.astype(o_ref.dtype)
        ncol += width


def _split_w_in(w):
    d = w.shape[0]
    kvw = NSA_KV_GROUPS * HEAD_DIM
    sizes = (Q_WIDTH, kvw, kvw, kvw, kvw, kvw, kvw, N_GATE, Q_WIDTH, Q_WIDTH, Q_WIDTH,
             IDX_HEADS * IDX_DIM, IDX_DIM, IDX_HEADS, d, d)
    assert sum(sizes) == w.shape[1]
    offs = np.concatenate([[0], np.cumsum(sizes)])
    (nsa_q, k_cmp, v_cmp, k_slc, v_slc, k_win, v_win, nsa_gate, dsa_q, dsa_k, dsa_v,
     idx_q, idx_k, idx_w, gate_nsa, gate_dsa) = [w[:, offs[i]:offs[i + 1]] for i in range(16)]

    def pad_groups(a):
        a = a.reshape(d, NSA_KV_GROUPS, HEAD_DIM)
        return jnp.concatenate([a, jnp.zeros_like(a)], axis=2).reshape(d, NSA_KV_GROUPS * LANE)

    idx_k_pad = jnp.concatenate([idx_k, jnp.zeros_like(idx_k)], axis=1)
    w_n = jnp.concatenate([k_cmp, v_cmp, pad_groups(k_slc), pad_groups(k_win), dsa_k, idx_k_pad,
                           gate_nsa, gate_dsa], axis=1)
    w_t = jnp.concatenate([nsa_q, v_slc, v_win, dsa_q, dsa_v, idx_q, nsa_gate, idx_w], axis=1).T
    assert w_n.shape[1] == _PROJ_NW and w_t.shape[0] == _PROJ_TW
    return w_n.astype(BF16), w_t.astype(BF16)


def _proj(xf, g, w_n, w_t, tables, seq_len, tm):
    t, d = xf.shape
    npos = seq_len // tm
    row = lambda i: (i, 0)
    tab_n = pl.BlockSpec((tm, LANE), lambda i: (i % npos, 0))
    tab_t = pl.BlockSpec((1, ROPE_HALF, tm), lambda i: (i % npos, 0, 0))
    out_specs, out_shape = [], []
    for layout, width, _, dt in _PROJ_SEGS:
        if layout == "t":
            out_specs.append(pl.BlockSpec((1, width, tm), lambda i: (i, 0, 0)))
            out_shape.append(jax.ShapeDtypeStruct((t // tm, width, tm), dt))
        else:
            out_specs.append(pl.BlockSpec((tm, width), row))
            out_shape.append(jax.ShapeDtypeStruct((t, width), dt))
    return pl.pallas_call(
        functools.partial(_proj_kernel, npos=npos),
        grid=(t // tm,),
        in_specs=[pl.BlockSpec((tm, d), row), _const_spec((1, d)), _const_spec((d, _PROJ_NW)),
                  _const_spec((_PROJ_TW, d)), tab_n, tab_n, tab_n, tab_t, tab_t],
        out_specs=out_specs,
        out_shape=out_shape,
        compiler_params=_params("parallel"),
        name="proj",
    )(xf, g.reshape(1, d), w_n, w_t, *tables)


def _cmp_kernel(k_ref, v_ref, pk_ref, w1k_ref, b1k_ref, w2k_ref, b2k_ref,
                pv_ref, w1v_ref, b1v_ref, w2v_ref, b2v_ref, kc_ref, vc_ref, *, n_cmp):
    nb = kc_ref.shape[1]

    def hidden(x_ref, pos_ref, w1_ref, b1_ref):
        halves = []
        for half in range(CMP_BLOCK // CMP_STRIDE):
            acc = None
            for l in range(CMP_STRIDE):
                j = half * CMP_STRIDE + l
                x = x_ref[0, pl.ds(l, nb, stride=CMP_STRIDE), :] + pos_ref[j:j + 1, :]
                part = _dot(x.astype(BF16), w1_ref[j])
                acc = part if acc is None else acc + part
            halves.append(acc)
        return jax.nn.gelu(halves[0] + pltpu.roll(halves[1], nb - 1, 0) + b1_ref[...]).astype(BF16)

    kc = _dot(hidden(k_ref, pk_ref, w1k_ref, b1k_ref), w2k_ref[...]) + b2k_ref[...]
    rows = lax.broadcasted_iota(jnp.int32, kc.shape, 0)
    kc_ref[0] = jnp.where(rows < n_cmp, kc, 0.0).astype(kc_ref.dtype)
    vc = _dot_nt(w2v_ref[...], hidden(v_ref, pv_ref, w1v_ref, b1v_ref)) + b2v_ref[...]
    cols = lax.broadcasted_iota(jnp.int32, vc.shape, 1)
    vc_ref[0] = jnp.where(cols < n_cmp, vc, 0.0).astype(vc_ref.dtype)


def _compress(k_cmp, v_cmp, pk, w1k, b1k, w2k, b2k, pv, w1v, b1v, w2v, b2v, bsz, seq_len):
    g, hd = NSA_KV_GROUPS, HEAD_DIM
    assert CMP_BLOCK == 2 * CMP_STRIDE
    nb = seq_len // CMP_STRIDE
    n_cmp = (seq_len - CMP_BLOCK) // CMP_STRIDE + 1
    eye = jnp.eye(g, dtype=F32)
    per_group = lambda m: jnp.kron(eye, m)
    tile_g = lambda a: jnp.concatenate([a] * g, axis=-1)

    def first_layer(pos, w1, b1):
        w1_bd = jax.vmap(per_group)(w1.reshape(CMP_BLOCK, hd, CMP_HIDDEN)).astype(BF16)
        return tile_g(pos), w1_bd, tile_g(b1).reshape(1, -1)

    w2k_pad = jnp.concatenate([w2k, jnp.zeros_like(w2k)], axis=1)
    b2k_pad = jnp.concatenate([b2k, jnp.zeros_like(b2k)])
    k_w = first_layer(pk, w1k, b1k) + (per_group(w2k_pad).astype(BF16), tile_g(b2k_pad).reshape(1, -1))
    v_w = first_layer(pv, w1v, b1v) + (per_group(w2v.T).astype(BF16), tile_g(b2v).reshape(-1, 1))
    seq_spec = pl.BlockSpec((1, seq_len, g * hd), lambda b: (b, 0, 0))
    first = [_const_spec((CMP_BLOCK, g * hd)), _const_spec((CMP_BLOCK, g * hd, g * CMP_HIDDEN)),
             _const_spec((1, g * CMP_HIDDEN))]
    return pl.pallas_call(
        functools.partial(_cmp_kernel, n_cmp=n_cmp),
        grid=(bsz,),
        in_specs=[seq_spec, seq_spec] + first + [_const_spec((g * CMP_HIDDEN, g * LANE)), _const_spec((1, g * LANE))]
        + first + [_const_spec((g * hd, g * CMP_HIDDEN)), _const_spec((g * hd, 1))],
        out_specs=[pl.BlockSpec((1, nb, g * LANE), lambda b: (b, 0, 0)),
                   pl.BlockSpec((1, g * hd, nb), lambda b: (b, 0, 0))],
        out_shape=[jax.ShapeDtypeStruct((bsz, nb, g * LANE), BF16),
                   jax.ShapeDtypeStruct((bsz, g * hd, nb), BF16)],
        compiler_params=_params("parallel"),
        name="compress",
    )(k_cmp.reshape(bsz, seq_len, g * hd), v_cmp.reshape(bsz, seq_len, g * hd), *k_w, *v_w)


def _fold(x, op):
    parts = [x[r * SUBLANE:(r + 1) * SUBLANE] for r in range(x.shape[0] // SUBLANE)]
    while len(parts) > 1:
        pairs = [op(parts[k], parts[k + 1]) for k in range(0, len(parts) - 1, 2)]
        parts = pairs + ([parts[-1]] if len(parts) % 2 else [])
    return parts[0]


def _nsa_kernel(q_ref, kc_ref, vc_ref, ks_ref, vs_ref, kw_ref, vw_ref, sm_ref, ovl_ref, o_ref,
                qm_ref, ss_ref, sw_ref, as_ref, *, t, seq_len):
    r_heads = NSA_GROUP_SIZE
    gi = pl.program_id(1)
    i = pl.program_id(2)
    t0 = i * t
    rows = r_heads * t
    hd = HEAD_DIM

    for r in range(r_heads):
        qm_ref[0:hd, r * t:(r + 1) * t] = q_ref[0, 0, r * hd:(r + 1) * hd, :]
    qm_ref[hd:, :] = jnp.zeros((LANE - hd, rows), BF16)
    tok = t0 + lax.broadcasted_iota(jnp.int32, (1, t), 1)
    tok_r = _lane_tile(tok, r_heads)

    nb = kc_ref.shape[1]
    s_c = _dot(kc_ref[0], qm_ref[...])
    c_end = lax.broadcasted_iota(jnp.int32, (nb, 1), 0) * CMP_STRIDE + (CMP_BLOCK - 1)
    mask_c = c_end <= tok_r
    s_c = jnp.where(mask_c, s_c, NEG)
    s_c = s_c - jnp.max(s_c, axis=0, keepdims=True)
    p_c = jnp.where(mask_c, jnp.exp(s_c), 0.0)
    p_c = (p_c / jnp.maximum(jnp.sum(p_c, axis=0, keepdims=True), 1e-30)).astype(BF16)
    o_c = _dot(vc_ref[0], p_c)
    imp4 = _dot(ovl_ref[...], p_c)
    imp = imp4[:, 0:t]
    for r in range(1, r_heads):
        imp = imp + imp4[:, r * t:(r + 1) * t]

    n_slc = seq_len // SLC_BLOCK
    n_sel = min(N_SLC, n_slc)
    blk = lax.broadcasted_iota(jnp.int32, (MAX_SLC_BLOCKS, 1), 0)
    dist = (tok // SLC_BLOCK) - blk
    forced = (blk == 0) | ((dist >= 0) & (dist < N_LOCAL_BLOCKS))
    score = jnp.where(forced, SEL_FORCE_SCORE, imp)
    score = jnp.where(blk * SLC_BLOCK <= tok, score, -jnp.inf)
    groups = [score[v * SUBLANE:(v + 1) * SUBLANE] for v in range(MAX_SLC_BLOCKS // SUBLANE)]
    ranks = [jnp.zeros((SUBLANE, t), F32) for _ in groups]
    sub = lax.broadcasted_iota(jnp.int32, (SUBLANE, 1), 0)
    for j in range(n_slc):
        row = score[j:j + 1]
        for v, sv in enumerate(groups):
            ge = jnp.where(row >= sv, 1.0, 0.0)
            gt = jnp.where(row > sv, 1.0, 0.0)
            if v * SUBLANE > j:
                beats = ge
            elif (v + 1) * SUBLANE <= j + 1:
                beats = gt
            else:
                beats = jnp.where(sub + v * SUBLANE > j, ge, gt)
            ranks[v] = ranks[v] + beats
    selbias = jnp.where(jnp.concatenate(ranks, axis=0) < n_sel, 0.0, NEG).astype(BF16)
    for r in range(r_heads):
        qm_ref[hd:hd + MAX_SLC_BLOCKS, r * t:(r + 1) * t] = selbias
    key_off = lax.broadcasted_iota(jnp.int32, (t, 1), 0)
    q_off = lax.broadcasted_iota(jnp.int32, (1, t), 1)
    causal = _lane_tile(jnp.where(key_off <= q_off, 0.0, NEG), r_heads)
    neg8 = jnp.full((SUBLANE, rows), NEG, F32)

    def key_tile(k_ref_, c):
        return k_ref_[0, pl.ds(pl.multiple_of(c * t, t), t), :]

    def slc_scores(c, mx):
        s = _dot(key_tile(ks_ref, c), qm_ref[...])
        ss_ref[c] = s
        return jnp.maximum(mx, _fold(s, jnp.maximum))

    mx = lax.fori_loop(0, i, slc_scores, neg8)
    s = _dot(key_tile(ks_ref, i), qm_ref[...]) + causal
    ss_ref[i] = s
    m_s = jnp.max(jnp.maximum(mx, _fold(s, jnp.maximum)), axis=0, keepdims=True)

    n_back = -(-WINDOW // t)
    win_tiles = []
    mxw = neg8
    for back in range(n_back, -1, -1):
        if back:
            lo_edge = back * t - WINDOW
            bias = _lane_tile(jnp.where(key_off - q_off > jnp.where(i >= back, lo_edge, t), 0.0, NEG), r_heads)
        else:
            bias = causal
        c = jnp.maximum(i - back, 0)
        s = _dot(key_tile(kw_ref, c), qm_ref[...]) + bias
        sw_ref[n_back - back] = s
        mxw = jnp.maximum(mxw, _fold(s, jnp.maximum))
        win_tiles.append(c)
    m_w = jnp.max(mxw, axis=0, keepdims=True)

    as_ref[...] = jnp.zeros(as_ref.shape, F32)

    def slc_values(c, l8):
        p = jnp.exp(ss_ref[c] - m_s)
        as_ref[...] += _dot(vs_ref[0, c], p.astype(BF16))
        return l8 + _fold(p, jnp.add)

    l8 = lax.fori_loop(0, i + 1, slc_values, jnp.zeros((SUBLANE, rows), F32))
    o_s = as_ref[...] * (1.0 / jnp.sum(l8, axis=0, keepdims=True))

    l8 = jnp.zeros((SUBLANE, rows), F32)
    o_w = jnp.zeros((hd, rows), F32)
    for n, c in enumerate(win_tiles):
        p = jnp.exp(sw_ref[n] - m_w)
        o_w = o_w + _dot(vw_ref[0, c], p.astype(BF16))
        l8 = l8 + _fold(p, jnp.add)
    o_w = o_w * (1.0 / jnp.sum(l8, axis=0, keepdims=True))

    gates = sm_ref[0, 0]

    def gate(branch, r):
        rows_g = [gates[branch * Q_HEADS + g * r_heads + r:branch * Q_HEADS + g * r_heads + r + 1]
                  for g in range(NSA_KV_GROUPS)]
        out = rows_g[0]
        for g in range(1, NSA_KV_GROUPS):
            out = jnp.where(gi == g, rows_g[g], out)
        return out

    heads = []
    for r in range(r_heads):
        sl = slice(r * t, (r + 1) * t)
        heads.append(gate(0, r) * o_c[:, sl] + gate(1, r) * o_s[:, sl] + gate(2, r) * o_w[:, sl])
    o_ref[0] = jnp.concatenate(heads, axis=0).T.astype(o_ref.dtype)


def _overlap_t(seq_len):
    nb = seq_len // CMP_STRIDE
    n_cmp = (seq_len - CMP_BLOCK) // CMP_STRIDE + 1
    n_slc = seq_len // SLC_BLOCK
    c_start = np.arange(n_cmp)[None, :] * CMP_STRIDE
    s_start = np.arange(n_slc)[:, None] * SLC_BLOCK
    ov = np.minimum(c_start + CMP_BLOCK, s_start + SLC_BLOCK) - np.maximum(c_start, s_start)
    ovl = np.zeros((MAX_SLC_BLOCKS, nb), np.float32)
    ovl[:n_slc, :n_cmp] = np.clip(ov, 0, None) / CMP_BLOCK
    return jnp.asarray(ovl, BF16)


def _nsa(q_t, kc, vc_t, ks, vs_t, kw, vw_t, small_t, bsz, seq_len, t):
    g, r, hd = NSA_KV_GROUPS, NSA_GROUP_SIZE, HEAD_DIM
    assert seq_len // SLC_BLOCK <= MAX_SLC_BLOCKS
    nb = seq_len // CMP_STRIDE
    nc = seq_len // t
    rows = r * t
    ovl = _overlap_t(seq_len)
    k_spec = pl.BlockSpec((1, seq_len, LANE), lambda b, gi, i: (b, 0, gi))
    v_spec = pl.BlockSpec((1, nc, hd, t), lambda b, gi, i: (b, 0, gi, 0))
    return pl.pallas_call(
        functools.partial(_nsa_kernel, t=t, seq_len=seq_len),
        grid=(bsz, g, nc),
        in_specs=[pl.BlockSpec((1, 1, r * hd, t), lambda b, gi, i: (b, i, gi, 0)),
                  pl.BlockSpec((1, nb, LANE), lambda b, gi, i: (b, 0, gi)),
                  pl.BlockSpec((1, hd, nb), lambda b, gi, i: (b, gi, 0)),
                  k_spec, v_spec, k_spec, v_spec,
                  pl.BlockSpec((1, 1, SMALL_W, t), lambda b, gi, i: (b, i, 0, 0)),
                  _const_spec(ovl.shape)],
        out_specs=pl.BlockSpec((1, t, r * hd), lambda b, gi, i: (b, i, gi)),
        out_shape=jax.ShapeDtypeStruct((bsz, seq_len, Q_WIDTH), BF16),
        scratch_shapes=[pltpu.VMEM((LANE, rows), BF16), pltpu.VMEM((nc, t, rows), F32),
                        pltpu.VMEM((-(-WINDOW // t) + 1, t, rows), F32), pltpu.VMEM((hd, rows), F32)],
        compiler_params=_params("parallel", "parallel", "parallel"),
        name="nsa_attn",
    )(q_t.reshape(bsz, nc, Q_WIDTH, t), kc, vc_t, ks.reshape(bsz, seq_len, -1), vs_t.reshape(bsz, nc, g * hd, t),
      kw.reshape(bsz, seq_len, -1), vw_t.reshape(bsz, nc, g * hd, t), small_t.reshape(bsz, nc, SMALL_W, t), ovl)


def _dsa_kernel(q_ref, k_ref, v_ref, iq_ref, ik_ref, sm_ref, o_ref,
                qm_ref, iqm_ref, sc_ref, s_ref, acc_ref, *, t, seq_len, k_top):
    i = pl.program_id(1)
    t0 = i * t
    nck = i + 1
    hd = HEAD_DIM
    tok = t0 + lax.broadcasted_iota(jnp.int32, (1, t), 1)

    zeros = jnp.zeros((hd, t), BF16)
    for h in range(Q_HEADS):
        qh = q_ref[0, 0, h * hd:(h + 1) * hd, :]
        qm_ref[h] = jnp.concatenate([qh, zeros] if h % 2 == 0 else [zeros, qh], axis=0)
        iqm_ref[h] = jnp.concatenate([iq_ref[0, 0, h * hd:(h + 1) * hd, :], zeros], axis=0)
    small = sm_ref[0, 0]

    def key_pos(c):
        return pl.multiple_of(c * t, t) + lax.broadcasted_iota(jnp.int32, (t, 1), 0)

    def score_body(c, carry):
        ik = ik_ref[0, pl.ds(pl.multiple_of(c * t, t), t), :]
        sc = jnp.zeros((t, t), F32)
        for h in range(IDX_HEADS):
            sc = sc + jnp.maximum(_dot(ik, iqm_ref[h]), 0.0) * small[N_GATE + h:N_GATE + h + 1]
        sc_ref[c] = jnp.where(key_pos(c) <= tok, sc, -jnp.inf)
        return carry

    lax.fori_loop(0, nck, score_body, 0)

    def count(indicator):
        def body(c, acc):
            return acc + _fold(indicator(sc_ref[c], key_pos(c)), jnp.add)
        acc = lax.fori_loop(0, nck, body, jnp.zeros((SUBLANE, t), F32))
        return jnp.sum(acc, axis=0, keepdims=True)

    kf = float(k_top)

    def decode(code):
        return lax.bitcast_convert_type(jnp.where(code < 0, code ^ jnp.int32(0x7FFFFFFF), code), F32)

    def reached(code):
        val = decode(code)
        return (count(lambda sc, pos: jnp.where(sc >= val, 1.0, 0.0)) >= kf) | (code <= KEY_NEG_INF)

    zero = jnp.zeros((1, t), jnp.int32)

    def bisect():
        base = jnp.where(reached(zero), zero, jnp.full((1, t), INT_MIN, jnp.int32))

        def bit_body(b, base):
            cand = base + lax.shift_left(jnp.int32(1), 30 - b)
            return jnp.where(reached(cand), cand, base)

        return lax.fori_loop(0, 31, bit_body, base)

    kth_code = lax.cond(nck * t <= k_top, lambda: jnp.full((1, t), KEY_NEG_INF, jnp.int32), bisect)
    kth = decode(jnp.maximum(kth_code, KEY_NEG_INF))

    n_ge = count(lambda sc, pos: jnp.where(sc >= kth, 1.0, 0.0))
    n_gt = count(lambda sc, pos: jnp.where(sc > kth, 1.0, 0.0))
    tie = (n_ge > kf) & (kth > -jnp.inf)

    def tie_cut():
        room = kf - n_gt
        top_bit = seq_len.bit_length() - 1

        def pos_body(b, x):
            cand = x + lax.shift_left(jnp.int32(1), top_bit - b)
            n = count(lambda sc, pos: jnp.where(sc == kth, jnp.where(pos < cand, 1.0, 0.0), 0.0))
            return jnp.where(n <= room, cand, x)

        x = lax.fori_loop(0, top_bit + 1, pos_body, zero)
        return jnp.where(tie, x, seq_len)

    cut = lax.cond(jnp.max(jnp.where(tie, 1.0, 0.0)) > 0.0, tie_cut,
                   lambda: jnp.full((1, t), seq_len, jnp.int32))
    cut = jnp.where(kth > -jnp.inf, cut, 0)

    def att_scores(c, mx):
        k0 = pl.multiple_of(c * t, t)
        sc = sc_ref[c]
        bias = jnp.where(sc > kth, 0.0, jnp.where(sc == kth, jnp.where(key_pos(c) < cut, 0.0, NEG), NEG))
        out = []
        for h in range(Q_HEADS):
            s = _dot(k_ref[0, pl.ds(k0, t), (h // 2) * LANE:(h // 2 + 1) * LANE], qm_ref[h]) + bias
            s_ref[h, c] = s
            out.append(jnp.maximum(mx[h], _fold(s, jnp.maximum)))
        return tuple(out)

    mx = lax.fori_loop(0, nck, att_scores, (jnp.full((SUBLANE, t), NEG, F32),) * Q_HEADS)
    m = [jnp.max(mx[h], axis=0, keepdims=True) for h in range(Q_HEADS)]

    acc_ref[...] = jnp.zeros(acc_ref.shape, F32)

    def att_values(c, l8):
        out = []
        for h in range(Q_HEADS):
            p = jnp.exp(s_ref[h, c] - m[h])
            acc_ref[h] += _dot(v_ref[0, c, h * hd:(h + 1) * hd, :], p.astype(BF16))
            out.append(l8[h] + _fold(p, jnp.add))
        return tuple(out)

    l8 = lax.fori_loop(0, nck, att_values, (jnp.zeros((SUBLANE, t), F32),) * Q_HEADS)
    out = jnp.concatenate([acc_ref[h] * (1.0 / jnp.sum(l8[h], axis=0, keepdims=True))
                           for h in range(Q_HEADS)], axis=0)
    o_ref[0] = out.T.astype(o_ref.dtype)


def _dsa(q_t, k, v_t, iq_t, ik, small_t, bsz, seq_len, t):
    k_top = min(DSA_TOPK_MAX, seq_len // 4)
    assert t >= k_top
    nc = seq_len // t
    qt_spec = pl.BlockSpec((1, 1, Q_WIDTH, t), lambda b, i: (b, i, 0, 0))
    return pl.pallas_call(
        functools.partial(_dsa_kernel, t=t, seq_len=seq_len, k_top=k_top),
        grid=(bsz, nc),
        in_specs=[qt_spec,
                  pl.BlockSpec((1, seq_len, Q_WIDTH), lambda b, i: (b, 0, 0)),
                  pl.BlockSpec((1, nc, Q_WIDTH, t), lambda b, i: (b, 0, 0, 0)),
                  qt_spec,
                  pl.BlockSpec((1, seq_len, LANE), lambda b, i: (b, 0, 0)),
                  pl.BlockSpec((1, 1, SMALL_W, t), lambda b, i: (b, i, 0, 0))],
        out_specs=pl.BlockSpec((1, t, Q_WIDTH), lambda b, i: (b, i, 0)),
        out_shape=jax.ShapeDtypeStruct((bsz, seq_len, Q_WIDTH), BF16),
        scratch_shapes=[pltpu.VMEM((Q_HEADS, LANE, t), BF16), pltpu.VMEM((IDX_HEADS, LANE, t), BF16),
                        pltpu.VMEM((nc, t, t), F32), pltpu.VMEM((Q_HEADS, nc, t, t), F32),
                        pltpu.VMEM((Q_HEADS, HEAD_DIM, t), F32)],
        compiler_params=_params("parallel", "parallel"),
        name="dsa_attn",
    )(q_t.reshape(bsz, nc, Q_WIDTH, t), k.reshape(bsz, seq_len, -1), v_t.reshape(bsz, nc, Q_WIDTH, t),
      iq_t.reshape(bsz, nc, Q_WIDTH, t), ik.reshape(bsz, seq_len, -1), small_t.reshape(bsz, nc, SMALL_W, t))


def _merge_kernel(x_ref, yn_ref, yd_ref, gn_ref, gd_ref, wbn_ref, wbd_ref, wo_ref, o_ref):
    merged = gn_ref[...] * _dot(yn_ref[...], wbn_ref[...]) + gd_ref[...] * _dot(yd_ref[...], wbd_ref[...])
    o_ref[...] = x_ref[...] + _dot(merged.astype(BF16), wo_ref[...])


def _merge(xf, y_nsa, y_dsa, gate_nsa, gate_dsa, wbn, wbd, wo, tm):
    t, d = xf.shape
    row = lambda i: (i, 0)
    wn = y_nsa.shape[1]
    return pl.pallas_call(
        _merge_kernel,
        grid=(t // tm,),
        in_specs=[pl.BlockSpec((tm, d), row), pl.BlockSpec((tm, wn), row), pl.BlockSpec((tm, wn), row),
                  pl.BlockSpec((tm, d), row), pl.BlockSpec((tm, d), row),
                  _const_spec((wn, d)), _const_spec((wn, d)), _const_spec((d, d))],
        out_specs=pl.BlockSpec((tm, d), row),
        out_shape=jax.ShapeDtypeStruct((t, d), F32),
        compiler_params=_params("parallel"),
        name="merge",
    )(xf, y_nsa, y_dsa, gate_nsa, gate_dsa, wbn.astype(BF16), wbd.astype(BF16), wo.astype(BF16))


def _ffn_kernel(x_ref, g_ref, wg_ref, wu_ref, wd_ref, fg_ref, o_ref, *, final):
    x = x_ref[...]
    h = _rmsnorm(x, g_ref[...]).astype(BF16)
    z = (jax.nn.silu(_dot(h, wg_ref[...])) * _dot(h, wu_ref[...])).astype(BF16)
    y = x + _dot(z, wd_ref[...])
    o_ref[...] = _rmsnorm(y, fg_ref[...]) if final else y


def _ffn(xf, g, wg, wu, wd, final_g, final, tm):
    t, d = xf.shape
    f = wg.shape[1]
    row = lambda i: (i, 0)
    return pl.pallas_call(
        functools.partial(_ffn_kernel, final=final),
        grid=(t // tm,),
        in_specs=[pl.BlockSpec((tm, d), row), _const_spec((1, d)), _const_spec((d, f)), _const_spec((d, f)),
                  _const_spec((f, d)), _const_spec((1, d))],
        out_specs=pl.BlockSpec((tm, d), row),
        out_shape=jax.ShapeDtypeStruct((t, d), F32),
        compiler_params=_params("parallel"),
        name="ffn",
    )(xf, g.reshape(1, d), wg.astype(BF16), wu.astype(BF16), wd.astype(BF16), final_g.reshape(1, d))


def _rope_tables(seq_len, tm):
    pos = jnp.arange(seq_len, dtype=F32)
    inv_freq = ROPE_THETA ** (-jnp.arange(0, ROPE_DIM, 2, dtype=F32) / ROPE_DIM)
    ang = pos[:, None] * inv_freq[None, :]
    cos, sin = jnp.cos(ang), jnp.sin(ang)
    rest = HEAD_DIM - ROPE_DIM
    one, zero = jnp.ones((seq_len, rest), F32), jnp.zeros((seq_len, rest), F32)
    zh = jnp.zeros((seq_len, ROPE_HALF), F32)
    two = lambda a: jnp.concatenate([a, a], axis=1)
    tile_t = lambda a: a.reshape(seq_len // tm, tm, ROPE_HALF).transpose(0, 2, 1)
    return (two(jnp.concatenate([cos, cos, one], axis=1)),
            two(jnp.concatenate([zh, sin, zero], axis=1)),
            two(jnp.concatenate([-sin, zh, zero], axis=1)),
            tile_t(cos), tile_t(sin))


def _tile(seq_len):
    return min(256, seq_len)


def kernel(x, attn_norm, w_in, cmpk_pos, cmpk_w1, cmpk_b1, cmpk_w2, cmpk_b2, cmpv_pos, cmpv_w1, cmpv_b1,
           cmpv_w2, cmpv_b2, w_branch_nsa, w_branch_dsa, w_out, ffn_norm, w_ffn_gate, w_ffn_up, w_ffn_down,
           final_norm):
    bsz, seq_len, d = x.shape
    depth = w_in.shape[0]
    t = _tile(seq_len)
    tables = _rope_tables(seq_len, t)
    xf = x.reshape(bsz * seq_len, d)
    for l in range(depth):
        (q_nsa, k_cmp, v_cmp, k_slc, v_slc, k_win, v_win, q_dsa, k_dsa, v_dsa, q_idx, k_idx,
         gate_nsa, gate_dsa, small) = _proj(xf, attn_norm[l], *_split_w_in(w_in[l]), tables, seq_len, t)
        kc, vc = _compress(k_cmp, v_cmp, cmpk_pos[l], cmpk_w1[l], cmpk_b1[l], cmpk_w2[l], cmpk_b2[l],
                           cmpv_pos[l], cmpv_w1[l], cmpv_b1[l], cmpv_w2[l], cmpv_b2[l], bsz, seq_len)
        y_nsa = _nsa(q_nsa, kc, vc, k_slc, v_slc, k_win, v_win, small, bsz, seq_len, t)
        y_dsa = _dsa(q_dsa, k_dsa, v_dsa, q_idx, k_idx, small, bsz, seq_len, t)
        xf = _merge(xf, y_nsa.reshape(bsz * seq_len, -1), y_dsa.reshape(bsz * seq_len, -1),
                    gate_nsa, gate_dsa, w_branch_nsa[l], w_branch_dsa[l], w_out[l], t)
        xf = _ffn(xf, ffn_norm[l], w_ffn_gate[l], w_ffn_up[l], w_ffn_down[l], final_norm,
                  final=(l == depth - 1), tm=t)
    return xf.reshape(bsz, seq_len, d)
```

```python
import functools

import numpy as np
import jax
import jax.numpy as jnp
from jax import lax
from jax.experimental import pallas as pl
from jax.experimental.pallas import tpu as pltpu

HEAD_DIM = 64
ROPE_DIM = HEAD_DIM // 4
ROPE_HALF = ROPE_DIM // 2
ROPE_THETA = 500000.0
NORM_EPS = 1e-6
ATTN_SCALE = HEAD_DIM ** -0.5
Q_SCALE = ATTN_SCALE * float(np.log2(np.e))
Q_HEADS = 8
Q_WIDTH = Q_HEADS * HEAD_DIM
NSA_KV_GROUPS = 2
NSA_GROUP_SIZE = Q_HEADS // NSA_KV_GROUPS
CMP_BLOCK = 32
CMP_STRIDE = 16
CMP_HIDDEN = 2 * HEAD_DIM
SLC_BLOCK = 64
N_SLC = 16
N_LOCAL_BLOCKS = 2
SEL_FORCE_SCORE = 1e4
WINDOW = 512
IDX_HEADS = 8
IDX_DIM = 64
IDX_WEIGHT_SCALE = (IDX_HEADS * IDX_DIM) ** -0.5
DSA_TOPK_MAX = 256
N_GATE = 3 * Q_HEADS
SMALL_W = N_GATE + IDX_HEADS
MAX_SLC_BLOCKS = 32

LANE = 128
SUBLANE = 8
ONES_ROWS = 16
V7X_VMEM_LIMIT = 56 * 1024 * 1024

NEG = -1e30
INT_MIN = -(2 ** 31)
KEY_NEG_INF = -2139095041
BF16 = jnp.bfloat16
F32 = jnp.float32

_NT = (((1,), (1,)), ((), ()))


def _dot(a, b):
    return jnp.dot(a, b, preferred_element_type=F32)


def _dot_nt(a, b):
    return lax.dot_general(a, b, _NT, preferred_element_type=F32)


def _const_spec(shape):
    nd = len(shape)
    return pl.BlockSpec(shape, lambda *_: (0,) * nd, pipeline_mode=pl.Buffered(1))


def _rmsnorm(x, g):
    ms = jnp.mean(x * x, axis=-1, keepdims=True)
    return x * lax.rsqrt(ms + NORM_EPS) * g


def _lane_tile(a, reps):
    return a if reps == 1 else jnp.concatenate([a] * reps, axis=1)


def _params(*sem):
    return pltpu.CompilerParams(dimension_semantics=sem, vmem_limit_bytes=V7X_VMEM_LIMIT)


_PROJ_SEGS = (
    ("t", Q_WIDTH, "ropeq", BF16),
    ("n", 128, "rope", F32),
    ("n", 128, "plain", F32),
    ("n", 256, "rope_onehot", BF16),
    ("t", 128, "plain", BF16),
    ("n", 256, "rope", BF16),
    ("t", 128, "plain", BF16),
    ("t", Q_WIDTH, "ropeq", BF16),
    ("n", Q_WIDTH, "rope", BF16),
    ("t", Q_WIDTH, "plain", BF16),
    ("t", IDX_HEADS * IDX_DIM, "rope", BF16),
    ("n", 128, "rope", BF16),
    ("n", 1024, "sig", F32),
    ("n", 1024, "sig", F32),
    ("t", SMALL_W, "small", F32),
)
_PROJ_NW = sum(s[1] for s in _PROJ_SEGS if s[0] == "n")
_PROJ_TW = sum(s[1] for s in _PROJ_SEGS if s[0] == "t")
_PROJ_CHUNK = 512


def _rope_n(y, c, sa, sb):
    pieces = []
    for j in range(y.shape[1] // LANE):
        yj = y[:, j * LANE:(j + 1) * LANE]
        pieces.append(yj * c + pltpu.roll(yj, ROPE_HALF, 1) * sa + pltpu.roll(yj, LANE - ROPE_HALF, 1) * sb)
    return pieces[0] if len(pieces) == 1 else jnp.concatenate(pieces, axis=1)


def _rope_t(y, cos, sin):
    pieces = []
    for h in range(y.shape[0] // HEAD_DIM):
        r0 = h * HEAD_DIM
        x1, x2 = y[r0:r0 + ROPE_HALF], y[r0 + ROPE_HALF:r0 + ROPE_DIM]
        pieces += [x1 * cos - x2 * sin, x2 * cos + x1 * sin, y[r0 + ROPE_DIM:r0 + HEAD_DIM]]
    return jnp.concatenate(pieces, axis=0)


def _proj_kernel(x_ref, g_ref, wn_ref, wt_ref, c_ref, sa_ref, sb_ref, ct_ref, st_ref, *out_refs, npos):
    tm = x_ref.shape[0]
    h = _rmsnorm(x_ref[...], g_ref[...]).astype(BF16)
    c, sa, sb = c_ref[...], sa_ref[...], sb_ref[...]
    cos_t, sin_t = ct_ref[0], st_ref[0]
    ncol = trow = 0
    for (layout, width, kind, _), o_ref in zip(_PROJ_SEGS, out_refs):
        if layout == "t":
            y = _dot_nt(wt_ref[trow:trow + width, :], h)
            trow += width
            if kind in ("rope", "ropeq"):
                y = _rope_t(y, cos_t, sin_t)
            if kind == "ropeq":
                y = y * Q_SCALE
            if kind == "small":
                row = lax.broadcasted_iota(jnp.int32, y.shape, 0)
                y = jnp.where(row < N_GATE, jax.nn.sigmoid(y), y * IDX_WEIGHT_SCALE)
            o_ref[0] = y.astype(o_ref.dtype)
            continue
        for c0 in range(0, width, _PROJ_CHUNK):
            cw = min(_PROJ_CHUNK, width - c0)
            y = _dot(h, wn_ref[:, ncol + c0:ncol + c0 + cw])
            if kind in ("rope", "rope_onehot"):
                y = _rope_n(y, c, sa, sb)
            if kind == "rope_onehot":
                pos = (pl.program_id(0) % npos) * tm + lax.broadcasted_iota(jnp.int32, y.shape, 0)
                lane = lax.broadcasted_iota(jnp.int32, y.shape, 1) & (LANE - 1)
                y = jnp.where(lane == HEAD_DIM + pos // SLC_BLOCK, 1.0, y)
            if kind == "sig":
                y = jax.nn.sigmoid(y)
            o_ref[:, c0:c0 + cw] = y.astype(o_ref.dtype)
        ncol += width


def _split_w_in(w):
    d = w.shape[0]
    kvw = NSA_KV_GROUPS * HEAD_DIM
    sizes = (Q_WIDTH, kvw, kvw, kvw, kvw, kvw, kvw, N_GATE, Q_WIDTH, Q_WIDTH, Q_WIDTH,
             IDX_HEADS * IDX_DIM, IDX_DIM, IDX_HEADS, d, d)
    assert sum(sizes) == w.shape[1]
    offs = np.concatenate([[0], np.cumsum(sizes)])
    (nsa_q, k_cmp, v_cmp, k_slc, v_slc, k_win, v_win, nsa_gate, dsa_q, dsa_k, dsa_v,
     idx_q, idx_k, idx_w, gate_nsa, gate_dsa) = [w[:, offs[i]:offs[i + 1]] for i in range(16)]

    def pad_groups(a):
        a = a.reshape(d, NSA_KV_GROUPS, HEAD_DIM)
        return jnp.concatenate([a, jnp.zeros_like(a)], axis=2).reshape(d, NSA_KV_GROUPS * LANE)

    idx_k_pad = jnp.concatenate([idx_k, jnp.zeros_like(idx_k)], axis=1)
    w_n = jnp.concatenate([k_cmp, v_cmp, pad_groups(k_slc), pad_groups(k_win), dsa_k, idx_k_pad,
                           gate_nsa, gate_dsa], axis=1)
    w_t = jnp.concatenate([nsa_q, v_slc, v_win, dsa_q, dsa_v, idx_q, nsa_gate, idx_w], axis=1).T
    assert w_n.shape[1] == _PROJ_NW and w_t.shape[0] == _PROJ_TW
    return w_n.astype(BF16), w_t.astype(BF16)


def _proj(xf, g, w_n, w_t, tables, seq_len, tm):
    t, d = xf.shape
    npos = seq_len // tm
    row = lambda i: (i, 0)
    tab_n = pl.BlockSpec((tm, LANE), lambda i: (i % npos, 0))
    tab_t = pl.BlockSpec((1, ROPE_HALF, tm), lambda i: (i % npos, 0, 0))
    out_specs, out_shape = [], []
    for layout, width, _, dt in _PROJ_SEGS:
        if layout == "t":
            out_specs.append(pl.BlockSpec((1, width, tm), lambda i: (i, 0, 0)))
            out_shape.append(jax.ShapeDtypeStruct((t // tm, width, tm), dt))
        else:
            out_specs.append(pl.BlockSpec((tm, width), row))
            out_shape.append(jax.ShapeDtypeStruct((t, width), dt))
    return pl.pallas_call(
        functools.partial(_proj_kernel, npos=npos),
        grid=(t // tm,),
        in_specs=[pl.BlockSpec((tm, d), row), _const_spec((1, d)), _const_spec((d, _PROJ_NW)),
                  _const_spec((_PROJ_TW, d)), tab_n, tab_n, tab_n, tab_t, tab_t],
        out_specs=out_specs,
        out_shape=out_shape,
        compiler_params=_params("parallel"),
        name="proj",
    )(xf, g.reshape(1, d), w_n, w_t, *tables)


def _cmp_kernel(k_ref, v_ref, pk_ref, w1k_ref, b1k_ref, w2k_ref, b2k_ref,
                pv_ref, w1v_ref, b1v_ref, w2v_ref, b2v_ref, kc_ref, vc_ref, *, n_cmp):
    nb = kc_ref.shape[1]

    def hidden(x_ref, pos_ref, w1_ref, b1_ref):
        halves = []
        for half in range(CMP_BLOCK // CMP_STRIDE):
            acc = None
            for l in range(CMP_STRIDE):
                j = half * CMP_STRIDE + l
                x = x_ref[0, pl.ds(l, nb, stride=CMP_STRIDE), :] + pos_ref[j:j + 1, :]
                part = _dot(x.astype(BF16), w1_ref[j])
                acc = part if acc is None else acc + part
            halves.append(acc)
        return jax.nn.gelu(halves[0] + pltpu.roll(halves[1], nb - 1, 0) + b1_ref[...]).astype(BF16)

    kc = _dot(hidden(k_ref, pk_ref, w1k_ref, b1k_ref), w2k_ref[...]) + b2k_ref[...]
    rows = lax.broadcasted_iota(jnp.int32, kc.shape, 0)
    kc_ref[0] = jnp.where(rows < n_cmp, kc, 0.0).astype(kc_ref.dtype)
    vc = _dot_nt(w2v_ref[...], hidden(v_ref, pv_ref, w1v_ref, b1v_ref)) + b2v_ref[...]
    cols = lax.broadcasted_iota(jnp.int32, vc.shape, 1)
    vc_ref[0] = jnp.where(cols < n_cmp, vc, 0.0).astype(vc_ref.dtype)


def _compress(k_cmp, v_cmp, pk, w1k, b1k, w2k, b2k, pv, w1v, b1v, w2v, b2v, bsz, seq_len):
    g, hd = NSA_KV_GROUPS, HEAD_DIM
    assert CMP_BLOCK == 2 * CMP_STRIDE
    nb = seq_len // CMP_STRIDE
    n_cmp = (seq_len - CMP_BLOCK) // CMP_STRIDE + 1
    eye = jnp.eye(g, dtype=F32)
    per_group = lambda m: jnp.kron(eye, m)
    tile_g = lambda a: jnp.concatenate([a] * g, axis=-1)

    def first_layer(pos, w1, b1):
        w1_bd = jax.vmap(per_group)(w1.reshape(CMP_BLOCK, hd, CMP_HIDDEN)).astype(BF16)
        return tile_g(pos), w1_bd, tile_g(b1).reshape(1, -1)

    w2k_pad = jnp.concatenate([w2k, jnp.zeros_like(w2k)], axis=1)
    b2k_pad = jnp.concatenate([b2k, jnp.zeros_like(b2k)])
    k_w = first_layer(pk, w1k, b1k) + (per_group(w2k_pad).astype(BF16), tile_g(b2k_pad).reshape(1, -1))
    v_w = first_layer(pv, w1v, b1v) + (per_group(w2v.T).astype(BF16), tile_g(b2v).reshape(-1, 1))
    seq_spec = pl.BlockSpec((1, seq_len, g * hd), lambda b: (b, 0, 0))
    first = [_const_spec((CMP_BLOCK, g * hd)), _const_spec((CMP_BLOCK, g * hd, g * CMP_HIDDEN)),
             _const_spec((1, g * CMP_HIDDEN))]
    return pl.pallas_call(
        functools.partial(_cmp_kernel, n_cmp=n_cmp),
        grid=(bsz,),
        in_specs=[seq_spec, seq_spec] + first + [_const_spec((g * CMP_HIDDEN, g * LANE)), _const_spec((1, g * LANE))]
        + first + [_const_spec((g * hd, g * CMP_HIDDEN)), _const_spec((g * hd, 1))],
        out_specs=[pl.BlockSpec((1, nb, g * LANE), lambda b: (b, 0, 0)),
                   pl.BlockSpec((1, g * hd, nb), lambda b: (b, 0, 0))],
        out_shape=[jax.ShapeDtypeStruct((bsz, nb, g * LANE), BF16),
                   jax.ShapeDtypeStruct((bsz, g * hd, nb), BF16)],
        compiler_params=_params("parallel"),
        name="compress",
    )(k_cmp.reshape(bsz, seq_len, g * hd), v_cmp.reshape(bsz, seq_len, g * hd), *k_w, *v_w)


def _fold(x, op):
    parts = [x[r * SUBLANE:(r + 1) * SUBLANE] for r in range(x.shape[0] // SUBLANE)]
    while len(parts) > 1:
        pairs = [op(parts[k], parts[k + 1]) for k in range(0, len(parts) - 1, 2)]
        parts = pairs + ([parts[-1]] if len(parts) % 2 else [])
    return parts[0]


def _loop_pairs(n, body, init):
    carry = lax.fori_loop(0, n // 2, lambda j, c: body(2 * j + 1, body(2 * j, c)), init)
    return lax.cond(n % 2 == 1, lambda c: body(n - 1, c), lambda c: c, carry)


def _weights(s, m):
    return jnp.exp2((s - m).astype(BF16))


def _with_ones(v_t):
    return jnp.concatenate([v_t, jnp.ones((ONES_ROWS, v_t.shape[1]), v_t.dtype)], axis=0)


def _nsa_kernel(q_ref, kc_ref, vc_ref, ks_ref, vs_ref, kw_ref, vw_ref, sm_ref, ovl_ref, o_ref,
                qm_ref, ss_ref, sw_ref, as_ref, *, t, seq_len):
    r_heads = NSA_GROUP_SIZE
    gi = pl.program_id(1)
    i = pl.program_id(2)
    t0 = i * t
    rows = r_heads * t
    hd = HEAD_DIM

    for r in range(r_heads):
        qm_ref[0:hd, r * t:(r + 1) * t] = q_ref[0, 0, r * hd:(r + 1) * hd, :]
    qm_ref[hd:, :] = jnp.zeros((LANE - hd, rows), BF16)
    tok = t0 + lax.broadcasted_iota(jnp.int32, (1, t), 1)
    tok_r = _lane_tile(tok, r_heads)

    nb = kc_ref.shape[1]
    s_c = _dot(kc_ref[0], qm_ref[...])
    c_end = lax.broadcasted_iota(jnp.int32, (nb, 1), 0) * CMP_STRIDE + (CMP_BLOCK - 1)
    mask_c = c_end <= tok_r
    s_c = jnp.where(mask_c, s_c, NEG)
    s_c = s_c - jnp.max(s_c, axis=0, keepdims=True)
    p_c = jnp.where(mask_c, jnp.exp2(s_c), 0.0)
    p_c = (p_c / jnp.maximum(jnp.sum(p_c, axis=0, keepdims=True), 1e-30)).astype(BF16)
    o_c = _dot(vc_ref[0], p_c)
    imp4 = _dot(ovl_ref[...], p_c)
    imp = imp4[:, 0:t]
    for r in range(1, r_heads):
        imp = imp + imp4[:, r * t:(r + 1) * t]

    n_slc = seq_len // SLC_BLOCK
    n_sel = min(N_SLC, n_slc)
    blk = lax.broadcasted_iota(jnp.int32, (MAX_SLC_BLOCKS, 1), 0)
    dist = (tok // SLC_BLOCK) - blk
    forced = (blk == 0) | ((dist >= 0) & (dist < N_LOCAL_BLOCKS))
    score = jnp.where(forced, SEL_FORCE_SCORE, imp)
    score = jnp.where(blk * SLC_BLOCK <= tok, score, -jnp.inf)
    groups = [score[v * SUBLANE:(v + 1) * SUBLANE] for v in range(MAX_SLC_BLOCKS // SUBLANE)]
    ranks = [jnp.zeros((SUBLANE, t), F32) for _ in groups]
    sub = lax.broadcasted_iota(jnp.int32, (SUBLANE, 1), 0)
    for j in range(n_slc):
        row = score[j:j + 1]
        for v, sv in enumerate(groups):
            ge = jnp.where(row >= sv, 1.0, 0.0)
            gt = jnp.where(row > sv, 1.0, 0.0)
            if v * SUBLANE > j:
                beats = ge
            elif (v + 1) * SUBLANE <= j + 1:
                beats = gt
            else:
                beats = jnp.where(sub + v * SUBLANE > j, ge, gt)
            ranks[v] = ranks[v] + beats
    selbias = jnp.where(jnp.concatenate(ranks, axis=0) < n_sel, 0.0, NEG).astype(BF16)
    for r in range(r_heads):
        qm_ref[hd:hd + MAX_SLC_BLOCKS, r * t:(r + 1) * t] = selbias
    key_off = lax.broadcasted_iota(jnp.int32, (t, 1), 0)
    q_off = lax.broadcasted_iota(jnp.int32, (1, t), 1)
    causal = _lane_tile(jnp.where(key_off <= q_off, 0.0, NEG), r_heads)
    neg8 = jnp.full((SUBLANE, rows), NEG, F32)

    def key_tile(k_ref_, c):
        return k_ref_[0, pl.ds(pl.multiple_of(c * t, t), t), :]

    def slc_scores(c, mx):
        s = _dot(key_tile(ks_ref, c), qm_ref[...])
        ss_ref[c] = s
        return jnp.maximum(mx, _fold(s, jnp.maximum))

    mx = _loop_pairs(i, slc_scores, neg8)
    s = _dot(key_tile(ks_ref, i), qm_ref[...]) + causal
    ss_ref[i] = s
    m_s = jnp.max(jnp.maximum(mx, _fold(s, jnp.maximum)), axis=0, keepdims=True)

    n_back = -(-WINDOW // t)
    win_tiles = []
    mxw = neg8
    for back in range(n_back, -1, -1):
        if back:
            lo_edge = back * t - WINDOW
            bias = _lane_tile(jnp.where(key_off - q_off > jnp.where(i >= back, lo_edge, t), 0.0, NEG), r_heads)
        else:
            bias = causal
        c = jnp.maximum(i - back, 0)
        s = _dot(key_tile(kw_ref, c), qm_ref[...]) + bias
        sw_ref[n_back - back] = s
        mxw = jnp.maximum(mxw, _fold(s, jnp.maximum))
        win_tiles.append(c)
    m_w = jnp.max(mxw, axis=0, keepdims=True)

    as_ref[...] = jnp.zeros(as_ref.shape, F32)

    def slc_values(c, carry):
        as_ref[...] += _dot(_with_ones(vs_ref[0, c]), _weights(ss_ref[c], m_s))
        return carry

    _loop_pairs(i + 1, slc_values, 0)
    o_s = as_ref[:hd] * (1.0 / as_ref[hd:hd + 1])

    o_w = jnp.zeros(as_ref.shape, F32)
    for n, c in enumerate(win_tiles):
        o_w = o_w + _dot(_with_ones(vw_ref[0, c]), _weights(sw_ref[n], m_w))
    o_w = o_w[:hd] * (1.0 / o_w[hd:hd + 1])

    gates = sm_ref[0, 0]

    def gate(branch, r):
        rows_g = [gates[branch * Q_HEADS + g * r_heads + r:branch * Q_HEADS + g * r_heads + r + 1]
                  for g in range(NSA_KV_GROUPS)]
        out = rows_g[0]
        for g in range(1, NSA_KV_GROUPS):
            out = jnp.where(gi == g, rows_g[g], out)
        return out

    heads = []
    for r in range(r_heads):
        sl = slice(r * t, (r + 1) * t)
        heads.append(gate(0, r) * o_c[:, sl] + gate(1, r) * o_s[:, sl] + gate(2, r) * o_w[:, sl])
    o_ref[0] = jnp.concatenate(heads, axis=0).T.astype(o_ref.dtype)


def _overlap_t(seq_len):
    nb = seq_len // CMP_STRIDE
    n_cmp = (seq_len - CMP_BLOCK) // CMP_STRIDE + 1
    n_slc = seq_len // SLC_BLOCK
    c_start = np.arange(n_cmp)[None, :] * CMP_STRIDE
    s_start = np.arange(n_slc)[:, None] * SLC_BLOCK
    ov = np.minimum(c_start + CMP_BLOCK, s_start + SLC_BLOCK) - np.maximum(c_start, s_start)
    ovl = np.zeros((MAX_SLC_BLOCKS, nb), np.float32)
    ovl[:n_slc, :n_cmp] = np.clip(ov, 0, None) / CMP_BLOCK
    return jnp.asarray(ovl, BF16)


def _nsa(q_t, kc, vc_t, ks, vs_t, kw, vw_t, small_t, bsz, seq_len, t):
    g, r, hd = NSA_KV_GROUPS, NSA_GROUP_SIZE, HEAD_DIM
    assert seq_len // SLC_BLOCK <= MAX_SLC_BLOCKS
    nb = seq_len // CMP_STRIDE
    nc = seq_len // t
    rows = r * t
    ovl = _overlap_t(seq_len)
    k_spec = pl.BlockSpec((1, seq_len, LANE), lambda b, gi, i: (b, 0, gi))
    v_spec = pl.BlockSpec((1, nc, hd, t), lambda b, gi, i: (b, 0, gi, 0))
    return pl.pallas_call(
        functools.partial(_nsa_kernel, t=t, seq_len=seq_len),
        grid=(bsz, g, nc),
        in_specs=[pl.BlockSpec((1, 1, r * hd, t), lambda b, gi, i: (b, i, gi, 0)),
                  pl.BlockSpec((1, nb, LANE), lambda b, gi, i: (b, 0, gi)),
                  pl.BlockSpec((1, hd, nb), lambda b, gi, i: (b, gi, 0)),
                  k_spec, v_spec, k_spec, v_spec,
                  pl.BlockSpec((1, 1, SMALL_W, t), lambda b, gi, i: (b, i, 0, 0)),
                  _const_spec(ovl.shape)],
        out_specs=pl.BlockSpec((1, t, r * hd), lambda b, gi, i: (b, i, gi)),
        out_shape=jax.ShapeDtypeStruct((bsz, seq_len, Q_WIDTH), BF16),
        scratch_shapes=[pltpu.VMEM((LANE, rows), BF16), pltpu.VMEM((nc, t, rows), F32),
                        pltpu.VMEM((-(-WINDOW // t) + 1, t, rows), F32),
                        pltpu.VMEM((hd + ONES_ROWS, rows), F32)],
        compiler_params=_params("parallel", "parallel", "parallel"),
        name="nsa_attn",
    )(q_t.reshape(bsz, nc, Q_WIDTH, t), kc, vc_t, ks.reshape(bsz, seq_len, -1), vs_t.reshape(bsz, nc, g * hd, t),
      kw.reshape(bsz, seq_len, -1), vw_t.reshape(bsz, nc, g * hd, t), small_t.reshape(bsz, nc, SMALL_W, t), ovl)


def _dsa_kernel(q_ref, k_ref, v_ref, iq_ref, ik_ref, sm_ref, o_ref,
                qm_ref, iqm_ref, sc_ref, s_ref, acc_ref, *, t, seq_len, k_top):
    i = pl.program_id(1)
    t0 = i * t
    nck = i + 1
    hd = HEAD_DIM
    tok = t0 + lax.broadcasted_iota(jnp.int32, (1, t), 1)

    zeros = jnp.zeros((hd, t), BF16)
    for h in range(Q_HEADS):
        qh = q_ref[0, 0, h * hd:(h + 1) * hd, :]
        qm_ref[h] = jnp.concatenate([qh, zeros] if h % 2 == 0 else [zeros, qh], axis=0)
        iqm_ref[h] = jnp.concatenate([iq_ref[0, 0, h * hd:(h + 1) * hd, :], zeros], axis=0)
    small = sm_ref[0, 0]

    def key_pos(c):
        return pl.multiple_of(c * t, t) + lax.broadcasted_iota(jnp.int32, (t, 1), 0)

    def score_body(c, carry):
        ik = ik_ref[0, pl.ds(pl.multiple_of(c * t, t), t), :]
        sc = jnp.zeros((t, t), F32)
        for h in range(IDX_HEADS):
            sc = sc + jnp.maximum(_dot(ik, iqm_ref[h]), 0.0) * small[N_GATE + h:N_GATE + h + 1]
        sc_ref[c] = jnp.where(key_pos(c) <= tok, sc, -jnp.inf)
        return carry

    _loop_pairs(nck, score_body, 0)

    def count(indicator):
        def body(c, acc):
            return acc + _fold(indicator(sc_ref[c], key_pos(c)), jnp.add)
        acc = lax.fori_loop(0, nck, body, jnp.zeros((SUBLANE, t), F32))
        return jnp.sum(acc, axis=0, keepdims=True)

    kf = float(k_top)

    def decode(code):
        return lax.bitcast_convert_type(jnp.where(code < 0, code ^ jnp.int32(0x7FFFFFFF), code), F32)

    def reached(code):
        val = decode(code)
        return (count(lambda sc, pos: jnp.where(sc >= val, 1.0, 0.0)) >= kf) | (code <= KEY_NEG_INF)

    zero = jnp.zeros((1, t), jnp.int32)

    def bisect():
        base = jnp.where(reached(zero), zero, jnp.full((1, t), INT_MIN, jnp.int32))

        def bit_body(b, base):
            cand = base + lax.shift_left(jnp.int32(1), 30 - b)
            return jnp.where(reached(cand), cand, base)

        return lax.fori_loop(0, 31, bit_body, base)

    kth_code = lax.cond(nck * t <= k_top, lambda: jnp.full((1, t), KEY_NEG_INF, jnp.int32), bisect)
    kth = decode(jnp.maximum(kth_code, KEY_NEG_INF))

    n_ge = count(lambda sc, pos: jnp.where(sc >= kth, 1.0, 0.0))
    n_gt = count(lambda sc, pos: jnp.where(sc > kth, 1.0, 0.0))
    tie = (n_ge > kf) & (kth > -jnp.inf)

    def tie_cut():
        room = kf - n_gt
        top_bit = seq_len.bit_length() - 1

        def pos_body(b, x):
            cand = x + lax.shift_left(jnp.int32(1), top_bit - b)
            n = count(lambda sc, pos: jnp.where(sc == kth, jnp.where(pos < cand, 1.0, 0.0), 0.0))
            return jnp.where(n <= room, cand, x)

        x = lax.fori_loop(0, top_bit + 1, pos_body, zero)
        return jnp.where(tie, x, seq_len)

    cut = lax.cond(jnp.max(jnp.where(tie, 1.0, 0.0)) > 0.0, tie_cut,
                   lambda: jnp.full((1, t), seq_len, jnp.int32))
    cut = jnp.where(kth > -jnp.inf, cut, 0)

    def att_scores(c, mx):
        k0 = pl.multiple_of(c * t, t)
        sc = sc_ref[c]
        bias = jnp.where(sc > kth, 0.0, jnp.where(sc == kth, jnp.where(key_pos(c) < cut, 0.0, NEG), NEG))
        out = []
        for h in range(Q_HEADS):
            s = _dot(k_ref[0, pl.ds(k0, t), (h // 2) * LANE:(h // 2 + 1) * LANE], qm_ref[h]) + bias
            s_ref[h, c] = s
            out.append(jnp.maximum(mx[h], _fold(s, jnp.maximum)))
        return tuple(out)

    mx = _loop_pairs(nck, att_scores, (jnp.full((SUBLANE, t), NEG, F32),) * Q_HEADS)
    m = [jnp.max(mx[h], axis=0, keepdims=True) for h in range(Q_HEADS)]

    acc_ref[...] = jnp.zeros(acc_ref.shape, F32)

    def att_values(c, carry):
        for h in range(Q_HEADS):
            acc_ref[h] += _dot(_with_ones(v_ref[0, c, h * hd:(h + 1) * hd, :]), _weights(s_ref[h, c], m[h]))
        return carry

    _loop_pairs(nck, att_values, 0)
    out = jnp.concatenate([acc_ref[h, :hd] * (1.0 / acc_ref[h, hd:hd + 1])
                           for h in range(Q_HEADS)], axis=0)
    o_ref[0] = out.T.astype(o_ref.dtype)


def _dsa(q_t, k, v_t, iq_t, ik, small_t, bsz, seq_len, t):
    k_top = min(DSA_TOPK_MAX, seq_len // 4)
    assert t >= k_top
    nc = seq_len // t
    qt_spec = pl.BlockSpec((1, 1, Q_WIDTH, t), lambda b, i: (b, i, 0, 0))
    return pl.pallas_call(
        functools.partial(_dsa_kernel, t=t, seq_len=seq_len, k_top=k_top),
        grid=(bsz, nc),
        in_specs=[qt_spec,
                  pl.BlockSpec((1, seq_len, Q_WIDTH), lambda b, i: (b, 0, 0)),
                  pl.BlockSpec((1, nc, Q_WIDTH, t), lambda b, i: (b, 0, 0, 0)),
                  qt_spec,
                  pl.BlockSpec((1, seq_len, LANE), lambda b, i: (b, 0, 0)),
                  pl.BlockSpec((1, 1, SMALL_W, t), lambda b, i: (b, i, 0, 0))],
        out_specs=pl.BlockSpec((1, t, Q_WIDTH), lambda b, i: (b, i, 0)),
        out_shape=jax.ShapeDtypeStruct((bsz, seq_len, Q_WIDTH), BF16),
        scratch_shapes=[pltpu.VMEM((Q_HEADS, LANE, t), BF16), pltpu.VMEM((IDX_HEADS, LANE, t), BF16),
                        pltpu.VMEM((nc, t, t), F32), pltpu.VMEM((Q_HEADS, nc, t, t), F32),
                        pltpu.VMEM((Q_HEADS, HEAD_DIM + ONES_ROWS, t), F32)],
        compiler_params=_params("parallel", "parallel"),
        name="dsa_attn",
    )(q_t.reshape(bsz, nc, Q_WIDTH, t), k.reshape(bsz, seq_len, -1), v_t.reshape(bsz, nc, Q_WIDTH, t),
      iq_t.reshape(bsz, nc, Q_WIDTH, t), ik.reshape(bsz, seq_len, -1), small_t.reshape(bsz, nc, SMALL_W, t))


def _merge_kernel(x_ref, yn_ref, yd_ref, gn_ref, gd_ref, wbn_ref, wbd_ref, wo_ref, o_ref):
    merged = gn_ref[...] * _dot(yn_ref[...], wbn_ref[...]) + gd_ref[...] * _dot(yd_ref[...], wbd_ref[...])
    o_ref[...] = x_ref[...] + _dot(merged.astype(BF16), wo_ref[...])


def _merge(xf, y_nsa, y_dsa, gate_nsa, gate_dsa, wbn, wbd, wo, tm):
    t, d = xf.shape
    row = lambda i: (i, 0)
    wn = y_nsa.shape[1]
    return pl.pallas_call(
        _merge_kernel,
        grid=(t // tm,),
        in_specs=[pl.BlockSpec((tm, d), row), pl.BlockSpec((tm, wn), row), pl.BlockSpec((tm, wn), row),
                  pl.BlockSpec((tm, d), row), pl.BlockSpec((tm, d), row),
                  _const_spec((wn, d)), _const_spec((wn, d)), _const_spec((d, d))],
        out_specs=pl.BlockSpec((tm, d), row),
        out_shape=jax.ShapeDtypeStruct((t, d), F32),
        compiler_params=_params("parallel"),
        name="merge",
    )(xf, y_nsa, y_dsa, gate_nsa, gate_dsa, wbn.astype(BF16), wbd.astype(BF16), wo.astype(BF16))


def _ffn_kernel(x_ref, g_ref, wg_ref, wu_ref, wd_ref, fg_ref, o_ref, *, final):
    x = x_ref[...]
    h = _rmsnorm(x, g_ref[...]).astype(BF16)
    z = (jax.nn.silu(_dot(h, wg_ref[...])) * _dot(h, wu_ref[...])).astype(BF16)
    y = x + _dot(z, wd_ref[...])
    o_ref[...] = _rmsnorm(y, fg_ref[...]) if final else y


def _ffn(xf, g, wg, wu, wd, final_g, final, tm):
    t, d = xf.shape
    f = wg.shape[1]
    row = lambda i: (i, 0)
    return pl.pallas_call(
        functools.partial(_ffn_kernel, final=final),
        grid=(t // tm,),
        in_specs=[pl.BlockSpec((tm, d), row), _const_spec((1, d)), _const_spec((d, f)), _const_spec((d, f)),
                  _const_spec((f, d)), _const_spec((1, d))],
        out_specs=pl.BlockSpec((tm, d), row),
        out_shape=jax.ShapeDtypeStruct((t, d), F32),
        compiler_params=_params("parallel"),
        name="ffn",
    )(xf, g.reshape(1, d), wg.astype(BF16), wu.astype(BF16), wd.astype(BF16), final_g.reshape(1, d))


def _rope_tables(seq_len, tm):
    pos = jnp.arange(seq_len, dtype=F32)
    inv_freq = ROPE_THETA ** (-jnp.arange(0, ROPE_DIM, 2, dtype=F32) / ROPE_DIM)
    ang = pos[:, None] * inv_freq[None, :]
    cos, sin = jnp.cos(ang), jnp.sin(ang)
    rest = HEAD_DIM - ROPE_DIM
    one, zero = jnp.ones((seq_len, rest), F32), jnp.zeros((seq_len, rest), F32)
    zh = jnp.zeros((seq_len, ROPE_HALF), F32)
    two = lambda a: jnp.concatenate([a, a], axis=1)
    tile_t = lambda a: a.reshape(seq_len // tm, tm, ROPE_HALF).transpose(0, 2, 1)
    return (two(jnp.concatenate([cos, cos, one], axis=1)),
            two(jnp.concatenate([zh, sin, zero], axis=1)),
            two(jnp.concatenate([-sin, zh, zero], axis=1)),
            tile_t(cos), tile_t(sin))


def _tile(seq_len):
    return min(256, seq_len)


def kernel(x, attn_norm, w_in, cmpk_pos, cmpk_w1, cmpk_b1, cmpk_w2, cmpk_b2, cmpv_pos, cmpv_w1, cmpv_b1,
           cmpv_w2, cmpv_b2, w_branch_nsa, w_branch_dsa, w_out, ffn_norm, w_ffn_gate, w_ffn_up, w_ffn_down,
           final_norm):
    bsz, seq_len, d = x.shape
    depth = w_in.shape[0]
    t = _tile(seq_len)
    tables = _rope_tables(seq_len, t)
    xf = x.reshape(bsz * seq_len, d)
    for l in range(depth):
        (q_nsa, k_cmp, v_cmp, k_slc, v_slc, k_win, v_win, q_dsa, k_dsa, v_dsa, q_idx, k_idx,
         gate_nsa, gate_dsa, small) = _proj(xf, attn_norm[l], *_split_w_in(w_in[l]), tables, seq_len, t)
        kc, vc = _compress(k_cmp, v_cmp, cmpk_pos[l], cmpk_w1[l], cmpk_b1[l], cmpk_w2[l], cmpk_b2[l],
                           cmpv_pos[l], cmpv_w1[l], cmpv_b1[l], cmpv_w2[l], cmpv_b2[l], bsz, seq_len)
        y_nsa = _nsa(q_nsa, kc, vc, k_slc, v_slc, k_win, v_win, small, bsz, seq_len, t)
        y_dsa = _dsa(q_dsa, k_dsa, v_dsa, q_idx, k_idx, small, bsz, seq_len, t)
        xf = _merge(xf, y_nsa.reshape(bsz * seq_len, -1), y_dsa.reshape(bsz * seq_len, -1),
                    gate_nsa, gate_dsa, w_branch_nsa[l], w_branch_dsa[l], w_out[l], t)
        xf = _ffn(xf, ffn_norm[l], w_ffn_gate[l], w_ffn_up[l], w_ffn_down[l], final_norm,
                  final=(l == depth - 1), tm=t)
    return xf.reshape(bsz, seq_len, d)
```

```python
import functools

import numpy as np
import jax
import jax.numpy as jnp
from jax import lax
from jax.experimental import pallas as pl
from jax.experimental.pallas import tpu as pltpu

HEAD_DIM = 64
ROPE_DIM = HEAD_DIM // 4
ROPE_HALF = ROPE_DIM // 2
ROPE_THETA = 500000.0
NORM_EPS = 1e-6
ATTN_SCALE = HEAD_DIM ** -0.5
Q_SCALE = ATTN_SCALE * float(np.log2(np.e))
Q_HEADS = 8
Q_WIDTH = Q_HEADS * HEAD_DIM
NSA_KV_GROUPS = 2
NSA_GROUP_SIZE = Q_HEADS // NSA_KV_GROUPS
CMP_BLOCK = 32
CMP_STRIDE = 16
CMP_HIDDEN = 2 * HEAD_DIM
SLC_BLOCK = 64
N_SLC = 16
N_LOCAL_BLOCKS = 2
SEL_FORCE_SCORE = 1e4
WINDOW = 512
IDX_HEADS = 8
IDX_DIM = 64
IDX_WEIGHT_SCALE = (IDX_HEADS * IDX_DIM) ** -0.5
DSA_TOPK_MAX = 256
N_GATE = 3 * Q_HEADS
SMALL_W = N_GATE + IDX_HEADS
MAX_SLC_BLOCKS = 32

LANE = 128
SUBLANE = 8
ONES_ROWS = 16
V7X_VMEM_LIMIT = 56 * 1024 * 1024

NEG = -1e30
INT_MIN = -(2 ** 31)
SEARCH_GROUP = 4
KEY_NEG_INF = -2139095041
BF16 = jnp.bfloat16
F32 = jnp.float32

_NT = (((1,), (1,)), ((), ()))


def _dot(a, b):
    return jnp.dot(a, b, preferred_element_type=F32)


def _dot_nt(a, b):
    return lax.dot_general(a, b, _NT, preferred_element_type=F32)


def _const_spec(shape):
    nd = len(shape)
    return pl.BlockSpec(shape, lambda *_: (0,) * nd, pipeline_mode=pl.Buffered(1))


def _rmsnorm(x, g):
    ms = jnp.mean(x * x, axis=-1, keepdims=True)
    return x * lax.rsqrt(ms + NORM_EPS) * g


def _lane_tile(a, reps):
    return a if reps == 1 else jnp.concatenate([a] * reps, axis=1)


def _params(*sem):
    return pltpu.CompilerParams(dimension_semantics=sem, vmem_limit_bytes=V7X_VMEM_LIMIT)


_PROJ_SEGS = (
    ("t", Q_WIDTH, "ropeq", BF16),
    ("n", 128, "rope", F32),
    ("n", 128, "plain", F32),
    ("n", 256, "rope_onehot", BF16),
    ("t", 128, "plain", BF16),
    ("n", 256, "rope", BF16),
    ("t", 128, "plain", BF16),
    ("t", Q_WIDTH, "ropeq", BF16),
    ("n", Q_WIDTH, "rope", BF16),
    ("t", Q_WIDTH, "plain", BF16),
    ("t", IDX_HEADS * IDX_DIM, "rope", BF16),
    ("n", 128, "rope", BF16),
    ("n", 1024, "sig", F32),
    ("n", 1024, "sig", F32),
    ("t", SMALL_W, "small", F32),
)
_PROJ_NW = sum(s[1] for s in _PROJ_SEGS if s[0] == "n")
_PROJ_TW = sum(s[1] for s in _PROJ_SEGS if s[0] == "t")
_PROJ_CHUNK = 512


def _rope_n(y, c, sa, sb):
    pieces = []
    for j in range(y.shape[1] // LANE):
        yj = y[:, j * LANE:(j + 1) * LANE]
        pieces.append(yj * c + pltpu.roll(yj, ROPE_HALF, 1) * sa + pltpu.roll(yj, LANE - ROPE_HALF, 1) * sb)
    return pieces[0] if len(pieces) == 1 else jnp.concatenate(pieces, axis=1)


def _rope_t(y, cos, sin):
    pieces = []
    for h in range(y.shape[0] // HEAD_DIM):
        r0 = h * HEAD_DIM
        x1, x2 = y[r0:r0 + ROPE_HALF], y[r0 + ROPE_HALF:r0 + ROPE_DIM]
        pieces += [x1 * cos - x2 * sin, x2 * cos + x1 * sin, y[r0 + ROPE_DIM:r0 + HEAD_DIM]]
    return jnp.concatenate(pieces, axis=0)


def _proj_kernel(x_ref, g_ref, wn_ref, wt_ref, c_ref, sa_ref, sb_ref, ct_ref, st_ref, *out_refs, npos):
    tm = x_ref.shape[0]
    h = _rmsnorm(x_ref[...], g_ref[...]).astype(BF16)
    c, sa, sb = c_ref[...], sa_ref[...], sb_ref[...]
    cos_t, sin_t = ct_ref[0], st_ref[0]
    ncol = trow = 0
    for (layout, width, kind, _), o_ref in zip(_PROJ_SEGS, out_refs):
        if layout == "t":
            y = _dot_nt(wt_ref[trow:trow + width, :], h)
            trow += width
            if kind in ("rope", "ropeq"):
                y = _rope_t(y, cos_t, sin_t)
            if kind == "ropeq":
                y = y * Q_SCALE
            if kind == "small":
                row = lax.broadcasted_iota(jnp.int32, y.shape, 0)
                y = jnp.where(row < N_GATE, jax.nn.sigmoid(y), y * IDX_WEIGHT_SCALE)
            o_ref[0] = y.astype(o_ref.dtype)
            continue
        for c0 in range(0, width, _PROJ_CHUNK):
            cw = min(_PROJ_CHUNK, width - c0)
            y = _dot(h, wn_ref[:, ncol + c0:ncol + c0 + cw])
            if kind in ("rope", "rope_onehot"):
                y = _rope_n(y, c, sa, sb)
            if kind == "rope_onehot":
                pos = (pl.program_id(0) % npos) * tm + lax.broadcasted_iota(jnp.int32, y.shape, 0)
                lane = lax.broadcasted_iota(jnp.int32, y.shape, 1) & (LANE - 1)
                y = jnp.where(lane == HEAD_DIM + pos // SLC_BLOCK, 1.0, y)
            if kind == "sig":
                y = jax.nn.sigmoid(y)
            o_ref[:, c0:c0 + cw] = y.astype(o_ref.dtype)
        ncol += width


def _split_w_in(w):
    d = w.shape[0]
    kvw = NSA_KV_GROUPS * HEAD_DIM
    sizes = (Q_WIDTH, kvw, kvw, kvw, kvw, kvw, kvw, N_GATE, Q_WIDTH, Q_WIDTH, Q_WIDTH,
             IDX_HEADS * IDX_DIM, IDX_DIM, IDX_HEADS, d, d)
    assert sum(sizes) == w.shape[1]
    offs = np.concatenate([[0], np.cumsum(sizes)])
    (nsa_q, k_cmp, v_cmp, k_slc, v_slc, k_win, v_win, nsa_gate, dsa_q, dsa_k, dsa_v,
     idx_q, idx_k, idx_w, gate_nsa, gate_dsa) = [w[:, offs[i]:offs[i + 1]] for i in range(16)]

    def pad_groups(a):
        a = a.reshape(d, NSA_KV_GROUPS, HEAD_DIM)
        return jnp.concatenate([a, jnp.zeros_like(a)], axis=2).reshape(d, NSA_KV_GROUPS * LANE)

    idx_k_pad = jnp.concatenate([idx_k, jnp.zeros_like(idx_k)], axis=1)
    w_n = jnp.concatenate([k_cmp, v_cmp, pad_groups(k_slc), pad_groups(k_win), dsa_k, idx_k_pad,
                           gate_nsa, gate_dsa], axis=1)
    w_t = jnp.concatenate([nsa_q, v_slc, v_win, dsa_q, dsa_v, idx_q, nsa_gate, idx_w], axis=1).T
    assert w_n.shape[1] == _PROJ_NW and w_t.shape[0] == _PROJ_TW
    return w_n.astype(BF16), w_t.astype(BF16)


def _proj(xf, g, w_n, w_t, tables, seq_len, tm):
    t, d = xf.shape
    npos = seq_len // tm
    row = lambda i: (i, 0)
    tab_n = pl.BlockSpec((tm, LANE), lambda i: (i % npos, 0))
    tab_t = pl.BlockSpec((1, ROPE_HALF, tm), lambda i: (i % npos, 0, 0))
    out_specs, out_shape = [], []
    for layout, width, _, dt in _PROJ_SEGS:
        if layout == "t":
            out_specs.append(pl.BlockSpec((1, width, tm), lambda i: (i, 0, 0)))
            out_shape.append(jax.ShapeDtypeStruct((t // tm, width, tm), dt))
        else:
            out_specs.append(pl.BlockSpec((tm, width), row))
            out_shape.append(jax.ShapeDtypeStruct((t, width), dt))
    return pl.pallas_call(
        functools.partial(_proj_kernel, npos=npos),
        grid=(t // tm,),
        in_specs=[pl.BlockSpec((tm, d), row), _const_spec((1, d)), _const_spec((d, _PROJ_NW)),
                  _const_spec((_PROJ_TW, d)), tab_n, tab_n, tab_n, tab_t, tab_t],
        out_specs=out_specs,
        out_shape=out_shape,
        compiler_params=_params("parallel"),
        name="proj",
    )(xf, g.reshape(1, d), w_n, w_t, *tables)


def _cmp_kernel(k_ref, v_ref, pk_ref, w1k_ref, b1k_ref, w2k_ref, b2k_ref,
                pv_ref, w1v_ref, b1v_ref, w2v_ref, b2v_ref, kc_ref, vc_ref, *, n_cmp):
    nb = kc_ref.shape[1]

    def hidden(x_ref, pos_ref, w1_ref, b1_ref):
        halves = []
        for half in range(CMP_BLOCK // CMP_STRIDE):
            acc = None
            for l in range(CMP_STRIDE):
                j = half * CMP_STRIDE + l
                x = x_ref[0, pl.ds(l, nb, stride=CMP_STRIDE), :] + pos_ref[j:j + 1, :]
                part = _dot(x.astype(BF16), w1_ref[j])
                acc = part if acc is None else acc + part
            halves.append(acc)
        return jax.nn.gelu(halves[0] + pltpu.roll(halves[1], nb - 1, 0) + b1_ref[...]).astype(BF16)

    kc = _dot(hidden(k_ref, pk_ref, w1k_ref, b1k_ref), w2k_ref[...]) + b2k_ref[...]
    rows = lax.broadcasted_iota(jnp.int32, kc.shape, 0)
    kc_ref[0] = jnp.where(rows < n_cmp, kc, 0.0).astype(kc_ref.dtype)
    vc = _dot_nt(w2v_ref[...], hidden(v_ref, pv_ref, w1v_ref, b1v_ref)) + b2v_ref[...]
    cols = lax.broadcasted_iota(jnp.int32, vc.shape, 1)
    vc_ref[0] = jnp.where(cols < n_cmp, vc, 0.0).astype(vc_ref.dtype)


def _compress(k_cmp, v_cmp, pk, w1k, b1k, w2k, b2k, pv, w1v, b1v, w2v, b2v, bsz, seq_len):
    g, hd = NSA_KV_GROUPS, HEAD_DIM
    assert CMP_BLOCK == 2 * CMP_STRIDE
    nb = seq_len // CMP_STRIDE
    n_cmp = (seq_len - CMP_BLOCK) // CMP_STRIDE + 1
    eye = jnp.eye(g, dtype=F32)
    per_group = lambda m: jnp.kron(eye, m)
    tile_g = lambda a: jnp.concatenate([a] * g, axis=-1)

    def first_layer(pos, w1, b1):
        w1_bd = jax.vmap(per_group)(w1.reshape(CMP_BLOCK, hd, CMP_HIDDEN)).astype(BF16)
        return tile_g(pos), w1_bd, tile_g(b1).reshape(1, -1)

    w2k_pad = jnp.concatenate([w2k, jnp.zeros_like(w2k)], axis=1)
    b2k_pad = jnp.concatenate([b2k, jnp.zeros_like(b2k)])
    k_w = first_layer(pk, w1k, b1k) + (per_group(w2k_pad).astype(BF16), tile_g(b2k_pad).reshape(1, -1))
    v_w = first_layer(pv, w1v, b1v) + (per_group(w2v.T).astype(BF16), tile_g(b2v).reshape(-1, 1))
    seq_spec = pl.BlockSpec((1, seq_len, g * hd), lambda b: (b, 0, 0))
    first = [_const_spec((CMP_BLOCK, g * hd)), _const_spec((CMP_BLOCK, g * hd, g * CMP_HIDDEN)),
             _const_spec((1, g * CMP_HIDDEN))]
    return pl.pallas_call(
        functools.partial(_cmp_kernel, n_cmp=n_cmp),
        grid=(bsz,),
        in_specs=[seq_spec, seq_spec] + first + [_const_spec((g * CMP_HIDDEN, g * LANE)), _const_spec((1, g * LANE))]
        + first + [_const_spec((g * hd, g * CMP_HIDDEN)), _const_spec((g * hd, 1))],
        out_specs=[pl.BlockSpec((1, nb, g * LANE), lambda b: (b, 0, 0)),
                   pl.BlockSpec((1, g * hd, nb), lambda b: (b, 0, 0))],
        out_shape=[jax.ShapeDtypeStruct((bsz, nb, g * LANE), BF16),
                   jax.ShapeDtypeStruct((bsz, g * hd, nb), BF16)],
        compiler_params=_params("parallel"),
        name="compress",
    )(k_cmp.reshape(bsz, seq_len, g * hd), v_cmp.reshape(bsz, seq_len, g * hd), *k_w, *v_w)


def _fold(x, op):
    parts = [x[r * SUBLANE:(r + 1) * SUBLANE] for r in range(x.shape[0] // SUBLANE)]
    while len(parts) > 1:
        pairs = [op(parts[k], parts[k + 1]) for k in range(0, len(parts) - 1, 2)]
        parts = pairs + ([parts[-1]] if len(parts) % 2 else [])
    return parts[0]


def _loop_pairs(n, body, init):
    carry = lax.fori_loop(0, n // 2, lambda j, c: body(2 * j + 1, body(2 * j, c)), init)
    return lax.cond(n % 2 == 1, lambda c: body(n - 1, c), lambda c: c, carry)


def _weights(s, m):
    return jnp.exp2((s - m).astype(BF16))


def _with_ones(v_t):
    return jnp.concatenate([v_t, jnp.ones((ONES_ROWS, v_t.shape[1]), v_t.dtype)], axis=0)


def _nsa_kernel(q_ref, kc_ref, vc_ref, ks_ref, vs_ref, kw_ref, vw_ref, sm_ref, ovl_ref, o_ref,
                qm_ref, ss_ref, sw_ref, as_ref, *, t, seq_len):
    r_heads = NSA_GROUP_SIZE
    gi = pl.program_id(1)
    i = pl.program_id(2)
    t0 = i * t
    rows = r_heads * t
    hd = HEAD_DIM

    for r in range(r_heads):
        qm_ref[0:hd, r * t:(r + 1) * t] = q_ref[0, 0, r * hd:(r + 1) * hd, :]
    qm_ref[hd:, :] = jnp.zeros((LANE - hd, rows), BF16)
    tok = t0 + lax.broadcasted_iota(jnp.int32, (1, t), 1)
    tok_r = _lane_tile(tok, r_heads)

    nb = kc_ref.shape[1]
    s_c = _dot(kc_ref[0], qm_ref[...])
    c_end = lax.broadcasted_iota(jnp.int32, (nb, 1), 0) * CMP_STRIDE + (CMP_BLOCK - 1)
    mask_c = c_end <= tok_r
    s_c = jnp.where(mask_c, s_c, NEG)
    s_c = s_c - jnp.max(s_c, axis=0, keepdims=True)
    p_c = jnp.where(mask_c, jnp.exp2(s_c), 0.0)
    p_c = (p_c / jnp.maximum(jnp.sum(p_c, axis=0, keepdims=True), 1e-30)).astype(BF16)
    o_c = _dot(vc_ref[0], p_c)
    imp4 = _dot(ovl_ref[...], p_c)
    imp = imp4[:, 0:t]
    for r in range(1, r_heads):
        imp = imp + imp4[:, r * t:(r + 1) * t]

    n_slc = seq_len // SLC_BLOCK
    n_sel = min(N_SLC, n_slc)
    blk = lax.broadcasted_iota(jnp.int32, (MAX_SLC_BLOCKS, 1), 0)
    dist = (tok // SLC_BLOCK) - blk
    forced = (blk == 0) | ((dist >= 0) & (dist < N_LOCAL_BLOCKS))
    score = jnp.where(forced, SEL_FORCE_SCORE, imp)
    score = jnp.where(blk * SLC_BLOCK <= tok, score, -jnp.inf)
    groups = [score[v * SUBLANE:(v + 1) * SUBLANE] for v in range(MAX_SLC_BLOCKS // SUBLANE)]
    ranks = [jnp.zeros((SUBLANE, t), F32) for _ in groups]
    sub = lax.broadcasted_iota(jnp.int32, (SUBLANE, 1), 0)
    for j in range(n_slc):
        row = score[j:j + 1]
        for v, sv in enumerate(groups):
            ge = jnp.where(row >= sv, 1.0, 0.0)
            gt = jnp.where(row > sv, 1.0, 0.0)
            if v * SUBLANE > j:
                beats = ge
            elif (v + 1) * SUBLANE <= j + 1:
                beats = gt
            else:
                beats = jnp.where(sub + v * SUBLANE > j, ge, gt)
            ranks[v] = ranks[v] + beats
    selbias = jnp.where(jnp.concatenate(ranks, axis=0) < n_sel, 0.0, NEG).astype(BF16)
    for r in range(r_heads):
        qm_ref[hd:hd + MAX_SLC_BLOCKS, r * t:(r + 1) * t] = selbias
    key_off = lax.broadcasted_iota(jnp.int32, (t, 1), 0)
    q_off = lax.broadcasted_iota(jnp.int32, (1, t), 1)
    causal = _lane_tile(jnp.where(key_off <= q_off, 0.0, NEG), r_heads)
    neg8 = jnp.full((SUBLANE, rows), NEG, F32)

    def key_tile(k_ref_, c):
        return k_ref_[0, pl.ds(pl.multiple_of(c * t, t), t), :]

    def slc_scores(c, mx):
        s = _dot(key_tile(ks_ref, c), qm_ref[...])
        ss_ref[c] = s
        return jnp.maximum(mx, _fold(s, jnp.maximum))

    mx = _loop_pairs(i, slc_scores, neg8)
    s = _dot(key_tile(ks_ref, i), qm_ref[...]) + causal
    ss_ref[i] = s
    m_s = jnp.max(jnp.maximum(mx, _fold(s, jnp.maximum)), axis=0, keepdims=True)

    n_back = -(-WINDOW // t)
    win_tiles = []
    mxw = neg8
    for back in range(n_back, -1, -1):
        if back:
            lo_edge = back * t - WINDOW
            bias = _lane_tile(jnp.where(key_off - q_off > jnp.where(i >= back, lo_edge, t), 0.0, NEG), r_heads)
        else:
            bias = causal
        c = jnp.maximum(i - back, 0)
        s = _dot(key_tile(kw_ref, c), qm_ref[...]) + bias
        sw_ref[n_back - back] = s
        mxw = jnp.maximum(mxw, _fold(s, jnp.maximum))
        win_tiles.append(c)
    m_w = jnp.max(mxw, axis=0, keepdims=True)

    as_ref[...] = jnp.zeros(as_ref.shape, F32)

    def slc_values(c, carry):
        as_ref[...] += _dot(_with_ones(vs_ref[0, c]), _weights(ss_ref[c], m_s))
        return carry

    _loop_pairs(i + 1, slc_values, 0)
    o_s = as_ref[:hd] * (1.0 / as_ref[hd:hd + 1])

    o_w = jnp.zeros(as_ref.shape, F32)
    for n, c in enumerate(win_tiles):
        o_w = o_w + _dot(_with_ones(vw_ref[0, c]), _weights(sw_ref[n], m_w))
    o_w = o_w[:hd] * (1.0 / o_w[hd:hd + 1])

    gates = sm_ref[0, 0]

    def gate(branch, r):
        rows_g = [gates[branch * Q_HEADS + g * r_heads + r:branch * Q_HEADS + g * r_heads + r + 1]
                  for g in range(NSA_KV_GROUPS)]
        out = rows_g[0]
        for g in range(1, NSA_KV_GROUPS):
            out = jnp.where(gi == g, rows_g[g], out)
        return out

    heads = []
    for r in range(r_heads):
        sl = slice(r * t, (r + 1) * t)
        heads.append(gate(0, r) * o_c[:, sl] + gate(1, r) * o_s[:, sl] + gate(2, r) * o_w[:, sl])
    o_ref[0] = jnp.concatenate(heads, axis=0).T.astype(o_ref.dtype)


def _overlap_t(seq_len):
    nb = seq_len // CMP_STRIDE
    n_cmp = (seq_len - CMP_BLOCK) // CMP_STRIDE + 1
    n_slc = seq_len // SLC_BLOCK
    c_start = np.arange(n_cmp)[None, :] * CMP_STRIDE
    s_start = np.arange(n_slc)[:, None] * SLC_BLOCK
    ov = np.minimum(c_start + CMP_BLOCK, s_start + SLC_BLOCK) - np.maximum(c_start, s_start)
    ovl = np.zeros((MAX_SLC_BLOCKS, nb), np.float32)
    ovl[:n_slc, :n_cmp] = np.clip(ov, 0, None) / CMP_BLOCK
    return jnp.asarray(ovl, BF16)


def _nsa(q_t, kc, vc_t, ks, vs_t, kw, vw_t, small_t, bsz, seq_len, t):
    g, r, hd = NSA_KV_GROUPS, NSA_GROUP_SIZE, HEAD_DIM
    assert seq_len // SLC_BLOCK <= MAX_SLC_BLOCKS
    nb = seq_len // CMP_STRIDE
    nc = seq_len // t
    rows = r * t
    ovl = _overlap_t(seq_len)
    k_spec = pl.BlockSpec((1, seq_len, LANE), lambda b, gi, i: (b, 0, gi))
    v_spec = pl.BlockSpec((1, nc, hd, t), lambda b, gi, i: (b, 0, gi, 0))
    return pl.pallas_call(
        functools.partial(_nsa_kernel, t=t, seq_len=seq_len),
        grid=(bsz, g, nc),
        in_specs=[pl.BlockSpec((1, 1, r * hd, t), lambda b, gi, i: (b, i, gi, 0)),
                  pl.BlockSpec((1, nb, LANE), lambda b, gi, i: (b, 0, gi)),
                  pl.BlockSpec((1, hd, nb), lambda b, gi, i: (b, gi, 0)),
                  k_spec, v_spec, k_spec, v_spec,
                  pl.BlockSpec((1, 1, SMALL_W, t), lambda b, gi, i: (b, i, 0, 0)),
                  _const_spec(ovl.shape)],
        out_specs=pl.BlockSpec((1, t, r * hd), lambda b, gi, i: (b, i, gi)),
        out_shape=jax.ShapeDtypeStruct((bsz, seq_len, Q_WIDTH), BF16),
        scratch_shapes=[pltpu.VMEM((LANE, rows), BF16), pltpu.VMEM((nc, t, rows), F32),
                        pltpu.VMEM((-(-WINDOW // t) + 1, t, rows), F32),
                        pltpu.VMEM((hd + ONES_ROWS, rows), F32)],
        compiler_params=_params("parallel", "parallel", "parallel"),
        name="nsa_attn",
    )(q_t.reshape(bsz, nc, Q_WIDTH, t), kc, vc_t, ks.reshape(bsz, seq_len, -1), vs_t.reshape(bsz, nc, g * hd, t),
      kw.reshape(bsz, seq_len, -1), vw_t.reshape(bsz, nc, g * hd, t), small_t.reshape(bsz, nc, SMALL_W, t), ovl)


def _dsa_kernel(q_ref, k_ref, v_ref, iq_ref, ik_ref, sm_ref, o_ref,
                qm_ref, iqm_ref, sc_ref, s_ref, acc_ref, *, t, seq_len, k_top):
    i = pl.program_id(1)
    t0 = i * t
    nck = i + 1
    hd = HEAD_DIM
    tok = t0 + lax.broadcasted_iota(jnp.int32, (1, t), 1)

    zeros = jnp.zeros((hd, t), BF16)
    for h in range(Q_HEADS):
        qh = q_ref[0, 0, h * hd:(h + 1) * hd, :]
        qm_ref[h] = jnp.concatenate([qh, zeros] if h % 2 == 0 else [zeros, qh], axis=0)
        iqm_ref[h] = jnp.concatenate([iq_ref[0, 0, h * hd:(h + 1) * hd, :], zeros], axis=0)
    small = sm_ref[0, 0]

    def key_pos(c):
        return pl.multiple_of(c * t, t) + lax.broadcasted_iota(jnp.int32, (t, 1), 0)

    def score_body(c, carry):
        ik = ik_ref[0, pl.ds(pl.multiple_of(c * t, t), t), :]
        sc = jnp.zeros((t, t), F32)
        for h in range(IDX_HEADS):
            sc = sc + jnp.maximum(_dot(ik, iqm_ref[h]), 0.0) * small[N_GATE + h:N_GATE + h + 1]
        sc_ref[c] = jnp.where(key_pos(c) <= tok, sc, -jnp.inf)
        return carry

    _loop_pairs(nck, score_body, 0)

    def count(indicator):
        def body(c, acc):
            return acc + _fold(indicator(sc_ref[c], key_pos(c)), jnp.add)
        acc = lax.fori_loop(0, nck, body, jnp.zeros((SUBLANE, t), F32))
        return jnp.sum(acc, axis=0, keepdims=True)

    kf = float(k_top)
    n_keys = (nck * t).astype(F32)
    zero = jnp.zeros((1, t), jnp.int32)

    def decode(code):
        return lax.bitcast_convert_type(jnp.where(code < 0, code ^ jnp.int32(0x7FFFFFFF), code), F32)

    def count_ge(code):
        val = decode(code)
        return jnp.where(code <= KEY_NEG_INF, n_keys, count(lambda sc, pos: jnp.where(sc >= val, 1.0, 0.0)))

    def bisect():
        n0 = count_ge(zero)
        start = (jnp.int32(0), jnp.where(n0 >= kf, zero, INT_MIN), jnp.where(n0 >= kf, n0, n_keys))

        def group(state):
            g, base, n_base = state
            for k in range(SEARCH_GROUP):
                shift = 30 - (g * SEARCH_GROUP + k)
                cand = base + jnp.where(shift >= 0, lax.shift_left(jnp.int32(1), jnp.maximum(shift, 0)), 0)
                n = count_ge(cand)
                base, n_base = jnp.where(n >= kf, cand, base), jnp.where(n >= kf, n, n_base)
            return g + 1, base, n_base

        def unsettled(state):
            g, _, n_base = state
            return (g * SEARCH_GROUP < 31) & (jnp.max(jnp.where(n_base != kf, 1.0, 0.0)) > 0.0)

        _, base, n_base = lax.while_loop(unsettled, group, start)
        return base, n_base

    kth_code, n_ge = lax.cond(nck * t <= k_top,
                              lambda: (jnp.full((1, t), KEY_NEG_INF, jnp.int32), jnp.full((1, t), kf, F32)),
                              bisect)
    kth = decode(jnp.maximum(kth_code, KEY_NEG_INF))

    tie = (n_ge > kf) & (kth > -jnp.inf)

    def tie_cut():
        room = kf - count(lambda sc, pos: jnp.where(sc > kth, 1.0, 0.0))
        top_bit = seq_len.bit_length() - 1

        def pos_body(b, x):
            cand = x + lax.shift_left(jnp.int32(1), top_bit - b)
            n = count(lambda sc, pos: jnp.where(sc == kth, jnp.where(pos < cand, 1.0, 0.0), 0.0))
            return jnp.where(n <= room, cand, x)

        x = lax.fori_loop(0, top_bit + 1, pos_body, zero)
        return jnp.where(tie, x, seq_len)

    cut = lax.cond(jnp.max(jnp.where(tie, 1.0, 0.0)) > 0.0, tie_cut,
                   lambda: jnp.full((1, t), seq_len, jnp.int32))
    cut = jnp.where(kth > -jnp.inf, cut, 0)

    def att_scores(c, mx):
        k0 = pl.multiple_of(c * t, t)
        sc = sc_ref[c]
        bias = jnp.where(sc > kth, 0.0, jnp.where(sc == kth, jnp.where(key_pos(c) < cut, 0.0, NEG), NEG))
        out = []
        for h in range(Q_HEADS):
            s = _dot(k_ref[0, pl.ds(k0, t), (h // 2) * LANE:(h // 2 + 1) * LANE], qm_ref[h]) + bias
            s_ref[h, c] = s
            out.append(jnp.maximum(mx[h], _fold(s, jnp.maximum)))
        return tuple(out)

    mx = _loop_pairs(nck, att_scores, (jnp.full((SUBLANE, t), NEG, F32),) * Q_HEADS)
    m = [jnp.max(mx[h], axis=0, keepdims=True) for h in range(Q_HEADS)]

    acc_ref[...] = jnp.zeros(acc_ref.shape, F32)

    def att_values(c, carry):
        for h in range(Q_HEADS):
            acc_ref[h] += _dot(_with_ones(v_ref[0, c, h * hd:(h + 1) * hd, :]), _weights(s_ref[h, c], m[h]))
        return carry

    _loop_pairs(nck, att_values, 0)
    out = jnp.concatenate([acc_ref[h, :hd] * (1.0 / acc_ref[h, hd:hd + 1])
                           for h in range(Q_HEADS)], axis=0)
    o_ref[0] = out.T.astype(o_ref.dtype)


def _dsa(q_t, k, v_t, iq_t, ik, small_t, bsz, seq_len, t):
    k_top = min(DSA_TOPK_MAX, seq_len // 4)
    assert t >= k_top
    nc = seq_len // t
    qt_spec = pl.BlockSpec((1, 1, Q_WIDTH, t), lambda b, i: (b, i, 0, 0))
    return pl.pallas_call(
        functools.partial(_dsa_kernel, t=t, seq_len=seq_len, k_top=k_top),
        grid=(bsz, nc),
        in_specs=[qt_spec,
                  pl.BlockSpec((1, seq_len, Q_WIDTH), lambda b, i: (b, 0, 0)),
                  pl.BlockSpec((1, nc, Q_WIDTH, t), lambda b, i: (b, 0, 0, 0)),
                  qt_spec,
                  pl.BlockSpec((1, seq_len, LANE), lambda b, i: (b, 0, 0)),
                  pl.BlockSpec((1, 1, SMALL_W, t), lambda b, i: (b, i, 0, 0))],
        out_specs=pl.BlockSpec((1, t, Q_WIDTH), lambda b, i: (b, i, 0)),
        out_shape=jax.ShapeDtypeStruct((bsz, seq_len, Q_WIDTH), BF16),
        scratch_shapes=[pltpu.VMEM((Q_HEADS, LANE, t), BF16), pltpu.VMEM((IDX_HEADS, LANE, t), BF16),
                        pltpu.VMEM((nc, t, t), F32), pltpu.VMEM((Q_HEADS, nc, t, t), F32),
                        pltpu.VMEM((Q_HEADS, HEAD_DIM + ONES_ROWS, t), F32)],
        compiler_params=_params("parallel", "parallel"),
        name="dsa_attn",
    )(q_t.reshape(bsz, nc, Q_WIDTH, t), k.reshape(bsz, seq_len, -1), v_t.reshape(bsz, nc, Q_WIDTH, t),
      iq_t.reshape(bsz, nc, Q_WIDTH, t), ik.reshape(bsz, seq_len, -1), small_t.reshape(bsz, nc, SMALL_W, t))


def _merge_kernel(x_ref, yn_ref, yd_ref, gn_ref, gd_ref, wbn_ref, wbd_ref, wo_ref, o_ref):
    merged = gn_ref[...] * _dot(yn_ref[...], wbn_ref[...]) + gd_ref[...] * _dot(yd_ref[...], wbd_ref[...])
    o_ref[...] = x_ref[...] + _dot(merged.astype(BF16), wo_ref[...])


def _merge(xf, y_nsa, y_dsa, gate_nsa, gate_dsa, wbn, wbd, wo, tm):
    t, d = xf.shape
    row = lambda i: (i, 0)
    wn = y_nsa.shape[1]
    return pl.pallas_call(
        _merge_kernel,
        grid=(t // tm,),
        in_specs=[pl.BlockSpec((tm, d), row), pl.BlockSpec((tm, wn), row), pl.BlockSpec((tm, wn), row),
                  pl.BlockSpec((tm, d), row), pl.BlockSpec((tm, d), row),
                  _const_spec((wn, d)), _const_spec((wn, d)), _const_spec((d, d))],
        out_specs=pl.BlockSpec((tm, d), row),
        out_shape=jax.ShapeDtypeStruct((t, d), F32),
        compiler_params=_params("parallel"),
        name="merge",
    )(xf, y_nsa, y_dsa, gate_nsa, gate_dsa, wbn.astype(BF16), wbd.astype(BF16), wo.astype(BF16))


def _ffn_kernel(x_ref, g_ref, wg_ref, wu_ref, wd_ref, fg_ref, o_ref, *, final):
    x = x_ref[...]
    h = _rmsnorm(x, g_ref[...]).astype(BF16)
    z = (jax.nn.silu(_dot(h, wg_ref[...])) * _dot(h, wu_ref[...])).astype(BF16)
    y = x + _dot(z, wd_ref[...])
    o_ref[...] = _rmsnorm(y, fg_ref[...]) if final else y


def _ffn(xf, g, wg, wu, wd, final_g, final, tm):
    t, d = xf.shape
    f = wg.shape[1]
    row = lambda i: (i, 0)
    return pl.pallas_call(
        functools.partial(_ffn_kernel, final=final),
        grid=(t // tm,),
        in_specs=[pl.BlockSpec((tm, d), row), _const_spec((1, d)), _const_spec((d, f)), _const_spec((d, f)),
                  _const_spec((f, d)), _const_spec((1, d))],
        out_specs=pl.BlockSpec((tm, d), row),
        out_shape=jax.ShapeDtypeStruct((t, d), F32),
        compiler_params=_params("parallel"),
        name="ffn",
    )(xf, g.reshape(1, d), wg.astype(BF16), wu.astype(BF16), wd.astype(BF16), final_g.reshape(1, d))


def _rope_tables(seq_len, tm):
    pos = jnp.arange(seq_len, dtype=F32)
    inv_freq = ROPE_THETA ** (-jnp.arange(0, ROPE_DIM, 2, dtype=F32) / ROPE_DIM)
    ang = pos[:, None] * inv_freq[None, :]
    cos, sin = jnp.cos(ang), jnp.sin(ang)
    rest = HEAD_DIM - ROPE_DIM
    one, zero = jnp.ones((seq_len, rest), F32), jnp.zeros((seq_len, rest), F32)
    zh = jnp.zeros((seq_len, ROPE_HALF), F32)
    two = lambda a: jnp.concatenate([a, a], axis=1)
    tile_t = lambda a: a.reshape(seq_len // tm, tm, ROPE_HALF).transpose(0, 2, 1)
    return (two(jnp.concatenate([cos, cos, one], axis=1)),
            two(jnp.concatenate([zh, sin, zero], axis=1)),
            two(jnp.concatenate([-sin, zh, zero], axis=1)),
            tile_t(cos), tile_t(sin))


def _tile(seq_len):
    return min(256, seq_len)


def kernel(x, attn_norm, w_in, cmpk_pos, cmpk_w1, cmpk_b1, cmpk_w2, cmpk_b2, cmpv_pos, cmpv_w1, cmpv_b1,
           cmpv_w2, cmpv_b2, w_branch_nsa, w_branch_dsa, w_out, ffn_norm, w_ffn_gate, w_ffn_up, w_ffn_down,
           final_norm):
    bsz, seq_len, d = x.shape
    depth = w_in.shape[0]
    t = _tile(seq_len)
    tables = _rope_tables(seq_len, t)
    xf = x.reshape(bsz * seq_len, d)
    for l in range(depth):
        (q_nsa, k_cmp, v_cmp, k_slc, v_slc, k_win, v_win, q_dsa, k_dsa, v_dsa, q_idx, k_idx,
         gate_nsa, gate_dsa, small) = _proj(xf, attn_norm[l], *_split_w_in(w_in[l]), tables, seq_len, t)
        kc, vc = _compress(k_cmp, v_cmp, cmpk_pos[l], cmpk_w1[l], cmpk_b1[l], cmpk_w2[l], cmpk_b2[l],
                           cmpv_pos[l], cmpv_w1[l], cmpv_b1[l], cmpv_w2[l], cmpv_b2[l], bsz, seq_len)
        y_nsa = _nsa(q_nsa, kc, vc, k_slc, v_slc, k_win, v_win, small, bsz, seq_len, t)
        y_dsa = _dsa(q_dsa, k_dsa, v_dsa, q_idx, k_idx, small, bsz, seq_len, t)
        xf = _merge(xf, y_nsa.reshape(bsz * seq_len, -1), y_dsa.reshape(bsz * seq_len, -1),
                    gate_nsa, gate_dsa, w_branch_nsa[l], w_branch_dsa[l], w_out[l], t)
        xf = _ffn(xf, ffn_norm[l], w_ffn_gate[l], w_ffn_up[l], w_ffn_down[l], final_norm,
                  final=(l == depth - 1), tm=t)
    return xf.reshape(bsz, seq_len, d)
```

```python
import functools

import numpy as np
import jax
import jax.numpy as jnp
from jax import lax
from jax.experimental import pallas as pl
from jax.experimental.pallas import tpu as pltpu

HEAD_DIM = 64
ROPE_DIM = HEAD_DIM // 4
ROPE_HALF = ROPE_DIM // 2
ROPE_THETA = 500000.0
NORM_EPS = 1e-6
ATTN_SCALE = HEAD_DIM ** -0.5
Q_SCALE = ATTN_SCALE * float(np.log2(np.e))
Q_HEADS = 8
Q_WIDTH = Q_HEADS * HEAD_DIM
NSA_KV_GROUPS = 2
NSA_GROUP_SIZE = Q_HEADS // NSA_KV_GROUPS
CMP_BLOCK = 32
CMP_STRIDE = 16
CMP_HIDDEN = 2 * HEAD_DIM
SLC_BLOCK = 64
N_SLC = 16
N_LOCAL_BLOCKS = 2
SEL_FORCE_SCORE = 1e4
WINDOW = 512
IDX_HEADS = 8
IDX_DIM = 64
IDX_WEIGHT_SCALE = (IDX_HEADS * IDX_DIM) ** -0.5
DSA_TOPK_MAX = 256
N_GATE = 3 * Q_HEADS
SMALL_W = N_GATE + IDX_HEADS
MAX_SLC_BLOCKS = 32

LANE = 128
SUBLANE = 8
ONES_ROWS = 16
V7X_VMEM_LIMIT = 56 * 1024 * 1024

NEG = -1e30
INT_MIN = -(2 ** 31)
KEY_NEG_INF = -2139095041
BF16 = jnp.bfloat16
F32 = jnp.float32

_NT = (((1,), (1,)), ((), ()))


def _dot(a, b):
    return jnp.dot(a, b, preferred_element_type=F32)


def _dot_nt(a, b):
    return lax.dot_general(a, b, _NT, preferred_element_type=F32)


def _const_spec(shape):
    nd = len(shape)
    return pl.BlockSpec(shape, lambda *_: (0,) * nd, pipeline_mode=pl.Buffered(1))


def _rmsnorm(x, g):
    ms = jnp.mean(x * x, axis=-1, keepdims=True)
    return x * lax.rsqrt(ms + NORM_EPS) * g


def _lane_tile(a, reps):
    return a if reps == 1 else jnp.concatenate([a] * reps, axis=1)


def _params(*sem):
    return pltpu.CompilerParams(dimension_semantics=sem, vmem_limit_bytes=V7X_VMEM_LIMIT)


_PROJ_SEGS = (
    ("t", Q_WIDTH, "ropeq", BF16),
    ("n", 128, "rope", F32),
    ("n", 128, "plain", F32),
    ("n", 256, "rope_onehot", BF16),
    ("t", 128, "plain", BF16),
    ("n", 256, "rope", BF16),
    ("t", 128, "plain", BF16),
    ("t", Q_WIDTH, "ropeq", BF16),
    ("n", Q_WIDTH, "rope", BF16),
    ("t", Q_WIDTH, "plain", BF16),
    ("t", IDX_HEADS * IDX_DIM, "rope", BF16),
    ("n", 128, "rope", BF16),
    ("n", 1024, "sig", F32),
    ("n", 1024, "sig", F32),
    ("t", SMALL_W, "small", F32),
)
_PROJ_NW = sum(s[1] for s in _PROJ_SEGS if s[0] == "n")
_PROJ_TW = sum(s[1] for s in _PROJ_SEGS if s[0] == "t")
_PROJ_CHUNK = 512


def _rope_n(y, c, sa, sb):
    pieces = []
    for j in range(y.shape[1] // LANE):
        yj = y[:, j * LANE:(j + 1) * LANE]
        pieces.append(yj * c + pltpu.roll(yj, ROPE_HALF, 1) * sa + pltpu.roll(yj, LANE - ROPE_HALF, 1) * sb)
    return pieces[0] if len(pieces) == 1 else jnp.concatenate(pieces, axis=1)


def _rope_t(y, cos, sin):
    pieces = []
    for h in range(y.shape[0] // HEAD_DIM):
        r0 = h * HEAD_DIM
        x1, x2 = y[r0:r0 + ROPE_HALF], y[r0 + ROPE_HALF:r0 + ROPE_DIM]
        pieces += [x1 * cos - x2 * sin, x2 * cos + x1 * sin, y[r0 + ROPE_DIM:r0 + HEAD_DIM]]
    return jnp.concatenate(pieces, axis=0)


def _proj_kernel(x_ref, g_ref, wn_ref, wt_ref, c_ref, sa_ref, sb_ref, ct_ref, st_ref, *out_refs, npos):
    tm = x_ref.shape[0]
    h = _rmsnorm(x_ref[...], g_ref[...]).astype(BF16)
    c, sa, sb = c_ref[...], sa_ref[...], sb_ref[...]
    cos_t, sin_t = ct_ref[0], st_ref[0]
    ncol = trow = 0
    for (layout, width, kind, _), o_ref in zip(_PROJ_SEGS, out_refs):
        if layout == "t":
            y = _dot_nt(wt_ref[trow:trow + width, :], h)
            trow += width
            if kind in ("rope", "ropeq"):
                y = _rope_t(y, cos_t, sin_t)
            if kind == "ropeq":
                y = y * Q_SCALE
            if kind == "small":
                row = lax.broadcasted_iota(jnp.int32, y.shape, 0)
                y = jnp.where(row < N_GATE, jax.nn.sigmoid(y), y * IDX_WEIGHT_SCALE)
            o_ref[0] = y.astype(o_ref.dtype)
            continue
        for c0 in range(0, width, _PROJ_CHUNK):
            cw = min(_PROJ_CHUNK, width - c0)
            y = _dot(h, wn_ref[:, ncol + c0:ncol + c0 + cw])
            if kind in ("rope", "rope_onehot"):
                y = _rope_n(y, c, sa, sb)
            if kind == "rope_onehot":
                pos = (pl.program_id(0) % npos) * tm + lax.broadcasted_iota(jnp.int32, y.shape, 0)
                lane = lax.broadcasted_iota(jnp.int32, y.shape, 1) & (LANE - 1)
                y = jnp.where(lane == HEAD_DIM + pos // SLC_BLOCK, 1.0, y)
            if kind == "sig":
                y = jax.nn.sigmoid(y)
            o_ref[:, c0:c0 + cw] = y.astype(o_ref.dtype)
        ncol += width


def _split_w_in(w):
    d = w.shape[0]
    kvw = NSA_KV_GROUPS * HEAD_DIM
    sizes = (Q_WIDTH, kvw, kvw, kvw, kvw, kvw, kvw, N_GATE, Q_WIDTH, Q_WIDTH, Q_WIDTH,
             IDX_HEADS * IDX_DIM, IDX_DIM, IDX_HEADS, d, d)
    assert sum(sizes) == w.shape[1]
    offs = np.concatenate([[0], np.cumsum(sizes)])
    (nsa_q, k_cmp, v_cmp, k_slc, v_slc, k_win, v_win, nsa_gate, dsa_q, dsa_k, dsa_v,
     idx_q, idx_k, idx_w, gate_nsa, gate_dsa) = [w[:, offs[i]:offs[i + 1]] for i in range(16)]

    def pad_groups(a):
        a = a.reshape(d, NSA_KV_GROUPS, HEAD_DIM)
        return jnp.concatenate([a, jnp.zeros_like(a)], axis=2).reshape(d, NSA_KV_GROUPS * LANE)

    idx_k_pad = jnp.concatenate([idx_k, jnp.zeros_like(idx_k)], axis=1)
    w_n = jnp.concatenate([k_cmp, v_cmp, pad_groups(k_slc), pad_groups(k_win), dsa_k, idx_k_pad,
                           gate_nsa, gate_dsa], axis=1)
    w_t = jnp.concatenate([nsa_q, v_slc, v_win, dsa_q, dsa_v, idx_q, nsa_gate, idx_w], axis=1).T
    assert w_n.shape[1] == _PROJ_NW and w_t.shape[0] == _PROJ_TW
    return w_n.astype(BF16), w_t.astype(BF16)


def _proj(xf, g, w_n, w_t, tables, seq_len, tm):
    t, d = xf.shape
    npos = seq_len // tm
    row = lambda i: (i, 0)
    tab_n = pl.BlockSpec((tm, LANE), lambda i: (i % npos, 0))
    tab_t = pl.BlockSpec((1, ROPE_HALF, tm), lambda i: (i % npos, 0, 0))
    out_specs, out_shape = [], []
    for layout, width, _, dt in _PROJ_SEGS:
        if layout == "t":
            out_specs.append(pl.BlockSpec((1, width, tm), lambda i: (i, 0, 0)))
            out_shape.append(jax.ShapeDtypeStruct((t // tm, width, tm), dt))
        else:
            out_specs.append(pl.BlockSpec((tm, width), row))
            out_shape.append(jax.ShapeDtypeStruct((t, width), dt))
    return pl.pallas_call(
        functools.partial(_proj_kernel, npos=npos),
        grid=(t // tm,),
        in_specs=[pl.BlockSpec((tm, d), row), _const_spec((1, d)), _const_spec((d, _PROJ_NW)),
                  _const_spec((_PROJ_TW, d)), tab_n, tab_n, tab_n, tab_t, tab_t],
        out_specs=out_specs,
        out_shape=out_shape,
        compiler_params=_params("parallel"),
        name="proj",
    )(xf, g.reshape(1, d), w_n, w_t, *tables)


def _cmp_kernel(k_ref, v_ref, pk_ref, w1k_ref, b1k_ref, w2k_ref, b2k_ref,
                pv_ref, w1v_ref, b1v_ref, w2v_ref, b2v_ref, kc_ref, vc_ref, *, n_cmp):
    nb = kc_ref.shape[1]

    def hidden(x_ref, pos_ref, w1_ref, b1_ref):
        halves = []
        for half in range(CMP_BLOCK // CMP_STRIDE):
            acc = None
            for l in range(CMP_STRIDE):
                j = half * CMP_STRIDE + l
                x = x_ref[0, pl.ds(l, nb, stride=CMP_STRIDE), :] + pos_ref[j:j + 1, :]
                part = _dot(x.astype(BF16), w1_ref[j])
                acc = part if acc is None else acc + part
            halves.append(acc)
        return jax.nn.gelu(halves[0] + pltpu.roll(halves[1], nb - 1, 0) + b1_ref[...]).astype(BF16)

    kc = _dot(hidden(k_ref, pk_ref, w1k_ref, b1k_ref), w2k_ref[...]) + b2k_ref[...]
    rows = lax.broadcasted_iota(jnp.int32, kc.shape, 0)
    kc_ref[0] = jnp.where(rows < n_cmp, kc, 0.0).astype(kc_ref.dtype)
    vc = _dot_nt(w2v_ref[...], hidden(v_ref, pv_ref, w1v_ref, b1v_ref)) + b2v_ref[...]
    cols = lax.broadcasted_iota(jnp.int32, vc.shape, 1)
    vc_ref[0] = jnp.where(cols < n_cmp, vc, 0.0).astype(vc_ref.dtype)


def _compress(k_cmp, v_cmp, pk, w1k, b1k, w2k, b2k, pv, w1v, b1v, w2v, b2v, bsz, seq_len):
    g, hd = NSA_KV_GROUPS, HEAD_DIM
    assert CMP_BLOCK == 2 * CMP_STRIDE
    nb = seq_len // CMP_STRIDE
    n_cmp = (seq_len - CMP_BLOCK) // CMP_STRIDE + 1
    eye = jnp.eye(g, dtype=F32)
    per_group = lambda m: jnp.kron(eye, m)
    tile_g = lambda a: jnp.concatenate([a] * g, axis=-1)

    def first_layer(pos, w1, b1):
        w1_bd = jax.vmap(per_group)(w1.reshape(CMP_BLOCK, hd, CMP_HIDDEN)).astype(BF16)
        return tile_g(pos), w1_bd, tile_g(b1).reshape(1, -1)

    w2k_pad = jnp.concatenate([w2k, jnp.zeros_like(w2k)], axis=1)
    b2k_pad = jnp.concatenate([b2k, jnp.zeros_like(b2k)])
    k_w = first_layer(pk, w1k, b1k) + (per_group(w2k_pad).astype(BF16), tile_g(b2k_pad).reshape(1, -1))
    v_w = first_layer(pv, w1v, b1v) + (per_group(w2v.T).astype(BF16), tile_g(b2v).reshape(-1, 1))
    seq_spec = pl.BlockSpec((1, seq_len, g * hd), lambda b: (b, 0, 0))
    first = [_const_spec((CMP_BLOCK, g * hd)), _const_spec((CMP_BLOCK, g * hd, g * CMP_HIDDEN)),
             _const_spec((1, g * CMP_HIDDEN))]
    return pl.pallas_call(
        functools.partial(_cmp_kernel, n_cmp=n_cmp),
        grid=(bsz,),
        in_specs=[seq_spec, seq_spec] + first + [_const_spec((g * CMP_HIDDEN, g * LANE)), _const_spec((1, g * LANE))]
        + first + [_const_spec((g * hd, g * CMP_HIDDEN)), _const_spec((g * hd, 1))],
        out_specs=[pl.BlockSpec((1, nb, g * LANE), lambda b: (b, 0, 0)),
                   pl.BlockSpec((1, g * hd, nb), lambda b: (b, 0, 0))],
        out_shape=[jax.ShapeDtypeStruct((bsz, nb, g * LANE), BF16),
                   jax.ShapeDtypeStruct((bsz, g * hd, nb), BF16)],
        compiler_params=_params("parallel"),
        name="compress",
    )(k_cmp.reshape(bsz, seq_len, g * hd), v_cmp.reshape(bsz, seq_len, g * hd), *k_w, *v_w)


def _fold(x, op):
    parts = [x[r * SUBLANE:(r + 1) * SUBLANE] for r in range(x.shape[0] // SUBLANE)]
    while len(parts) > 1:
        pairs = [op(parts[k], parts[k + 1]) for k in range(0, len(parts) - 1, 2)]
        parts = pairs + ([parts[-1]] if len(parts) % 2 else [])
    return parts[0]


def _loop_pairs(n, body, init):
    carry = lax.fori_loop(0, n // 2, lambda j, c: body(2 * j + 1, body(2 * j, c)), init)
    return lax.cond(n % 2 == 1, lambda c: body(n - 1, c), lambda c: c, carry)


def _weights(s, m):
    return jnp.exp2((s - m).astype(BF16))


def _with_ones(v_t):
    return jnp.concatenate([v_t, jnp.ones((ONES_ROWS, v_t.shape[1]), v_t.dtype)], axis=0)


def _nsa_kernel(q_ref, kc_ref, vc_ref, ks_ref, vs_ref, kw_ref, vw_ref, sm_ref, ovl_ref, o_ref,
                qm_ref, ss_ref, sw_ref, as_ref, *, t, seq_len):
    r_heads = NSA_GROUP_SIZE
    gi = pl.program_id(1)
    i = pl.program_id(2)
    t0 = i * t
    rows = r_heads * t
    hd = HEAD_DIM

    for r in range(r_heads):
        qm_ref[0:hd, r * t:(r + 1) * t] = q_ref[0, 0, r * hd:(r + 1) * hd, :]
    qm_ref[hd:, :] = jnp.zeros((LANE - hd, rows), BF16)
    tok = t0 + lax.broadcasted_iota(jnp.int32, (1, t), 1)
    tok_r = _lane_tile(tok, r_heads)

    nb = kc_ref.shape[1]
    s_c = _dot(kc_ref[0], qm_ref[...])
    c_end = lax.broadcasted_iota(jnp.int32, (nb, 1), 0) * CMP_STRIDE + (CMP_BLOCK - 1)
    mask_c = c_end <= tok_r
    s_c = jnp.where(mask_c, s_c, NEG)
    s_c = s_c - jnp.max(s_c, axis=0, keepdims=True)
    p_c = jnp.where(mask_c, jnp.exp2(s_c), 0.0)
    p_c = (p_c / jnp.maximum(jnp.sum(p_c, axis=0, keepdims=True), 1e-30)).astype(BF16)
    o_c = _dot(vc_ref[0], p_c)
    imp4 = _dot(ovl_ref[...], p_c)
    imp = imp4[:, 0:t]
    for r in range(1, r_heads):
        imp = imp + imp4[:, r * t:(r + 1) * t]

    n_slc = seq_len // SLC_BLOCK
    n_sel = min(N_SLC, n_slc)
    blk = lax.broadcasted_iota(jnp.int32, (MAX_SLC_BLOCKS, 1), 0)
    dist = (tok // SLC_BLOCK) - blk
    forced = (blk == 0) | ((dist >= 0) & (dist < N_LOCAL_BLOCKS))
    score = jnp.where(forced, SEL_FORCE_SCORE, imp)
    score = jnp.where(blk * SLC_BLOCK <= tok, score, -jnp.inf)
    groups = [score[v * SUBLANE:(v + 1) * SUBLANE] for v in range(MAX_SLC_BLOCKS // SUBLANE)]
    ranks = [jnp.zeros((SUBLANE, t), F32) for _ in groups]
    sub = lax.broadcasted_iota(jnp.int32, (SUBLANE, 1), 0)
    for j in range(n_slc):
        row = score[j:j + 1]
        for v, sv in enumerate(groups):
            ge = jnp.where(row >= sv, 1.0, 0.0)
            gt = jnp.where(row > sv, 1.0, 0.0)
            if v * SUBLANE > j:
                beats = ge
            elif (v + 1) * SUBLANE <= j + 1:
                beats = gt
            else:
                beats = jnp.where(sub + v * SUBLANE > j, ge, gt)
            ranks[v] = ranks[v] + beats
    selbias = jnp.where(jnp.concatenate(ranks, axis=0) < n_sel, 0.0, NEG).astype(BF16)
    for r in range(r_heads):
        qm_ref[hd:hd + MAX_SLC_BLOCKS, r * t:(r + 1) * t] = selbias
    key_off = lax.broadcasted_iota(jnp.int32, (t, 1), 0)
    q_off = lax.broadcasted_iota(jnp.int32, (1, t), 1)
    causal = _lane_tile(jnp.where(key_off <= q_off, 0.0, NEG), r_heads)
    neg8 = jnp.full((SUBLANE, rows), NEG, F32)

    def key_tile(k_ref_, c):
        return k_ref_[0, pl.ds(pl.multiple_of(c * t, t), t), :]

    def slc_scores(c, mx):
        s = _dot(key_tile(ks_ref, c), qm_ref[...])
        ss_ref[c] = s
        return jnp.maximum(mx, _fold(s, jnp.maximum))

    mx = _loop_pairs(i, slc_scores, neg8)
    s = _dot(key_tile(ks_ref, i), qm_ref[...]) + causal
    ss_ref[i] = s
    m_s = jnp.max(jnp.maximum(mx, _fold(s, jnp.maximum)), axis=0, keepdims=True)

    n_back = -(-WINDOW // t)
    win_tiles = []
    mxw = neg8
    for back in range(n_back, -1, -1):
        if back:
            lo_edge = back * t - WINDOW
            bias = _lane_tile(jnp.where(key_off - q_off > jnp.where(i >= back, lo_edge, t), 0.0, NEG), r_heads)
        else:
            bias = causal
        c = jnp.maximum(i - back, 0)
        s = _dot(key_tile(kw_ref, c), qm_ref[...]) + bias
        sw_ref[n_back - back] = s
        mxw = jnp.maximum(mxw, _fold(s, jnp.maximum))
        win_tiles.append(c)
    m_w = jnp.max(mxw, axis=0, keepdims=True)

    as_ref[...] = jnp.zeros(as_ref.shape, F32)

    def slc_values(c, carry):
        as_ref[...] += _dot(_with_ones(vs_ref[0, c]), _weights(ss_ref[c], m_s))
        return carry

    _loop_pairs(i + 1, slc_values, 0)
    o_s = as_ref[:hd] * (1.0 / as_ref[hd:hd + 1])

    o_w = jnp.zeros(as_ref.shape, F32)
    for n, c in enumerate(win_tiles):
        o_w = o_w + _dot(_with_ones(vw_ref[0, c]), _weights(sw_ref[n], m_w))
    o_w = o_w[:hd] * (1.0 / o_w[hd:hd + 1])

    gates = sm_ref[0, 0]

    def gate(branch, r):
        rows_g = [gates[branch * Q_HEADS + g * r_heads + r:branch * Q_HEADS + g * r_heads + r + 1]
                  for g in range(NSA_KV_GROUPS)]
        out = rows_g[0]
        for g in range(1, NSA_KV_GROUPS):
            out = jnp.where(gi == g, rows_g[g], out)
        return out

    heads = []
    for r in range(r_heads):
        sl = slice(r * t, (r + 1) * t)
        heads.append(gate(0, r) * o_c[:, sl] + gate(1, r) * o_s[:, sl] + gate(2, r) * o_w[:, sl])
    o_ref[0] = jnp.concatenate(heads, axis=0).T.astype(o_ref.dtype)


def _overlap_t(seq_len):
    nb = seq_len // CMP_STRIDE
    n_cmp = (seq_len - CMP_BLOCK) // CMP_STRIDE + 1
    n_slc = seq_len // SLC_BLOCK
    c_start = np.arange(n_cmp)[None, :] * CMP_STRIDE
    s_start = np.arange(n_slc)[:, None] * SLC_BLOCK
    ov = np.minimum(c_start + CMP_BLOCK, s_start + SLC_BLOCK) - np.maximum(c_start, s_start)
    ovl = np.zeros((MAX_SLC_BLOCKS, nb), np.float32)
    ovl[:n_slc, :n_cmp] = np.clip(ov, 0, None) / CMP_BLOCK
    return jnp.asarray(ovl, BF16)


def _nsa(q_t, kc, vc_t, ks, vs_t, kw, vw_t, small_t, bsz, seq_len, t):
    g, r, hd = NSA_KV_GROUPS, NSA_GROUP_SIZE, HEAD_DIM
    assert seq_len // SLC_BLOCK <= MAX_SLC_BLOCKS
    nb = seq_len // CMP_STRIDE
    nc = seq_len // t
    rows = r * t
    ovl = _overlap_t(seq_len)
    k_spec = pl.BlockSpec((1, seq_len, LANE), lambda b, gi, i: (b, 0, gi))
    v_spec = pl.BlockSpec((1, nc, hd, t), lambda b, gi, i: (b, 0, gi, 0))
    return pl.pallas_call(
        functools.partial(_nsa_kernel, t=t, seq_len=seq_len),
        grid=(bsz, g, nc),
        in_specs=[pl.BlockSpec((1, 1, r * hd, t), lambda b, gi, i: (b, i, gi, 0)),
                  pl.BlockSpec((1, nb, LANE), lambda b, gi, i: (b, 0, gi)),
                  pl.BlockSpec((1, hd, nb), lambda b, gi, i: (b, gi, 0)),
                  k_spec, v_spec, k_spec, v_spec,
                  pl.BlockSpec((1, 1, SMALL_W, t), lambda b, gi, i: (b, i, 0, 0)),
                  _const_spec(ovl.shape)],
        out_specs=pl.BlockSpec((1, t, r * hd), lambda b, gi, i: (b, i, gi)),
        out_shape=jax.ShapeDtypeStruct((bsz, seq_len, Q_WIDTH), BF16),
        scratch_shapes=[pltpu.VMEM((LANE, rows), BF16), pltpu.VMEM((nc, t, rows), F32),
                        pltpu.VMEM((-(-WINDOW // t) + 1, t, rows), F32),
                        pltpu.VMEM((hd + ONES_ROWS, rows), F32)],
        compiler_params=_params("parallel", "parallel", "parallel"),
        name="nsa_attn",
    )(q_t.reshape(bsz, nc, Q_WIDTH, t), kc, vc_t, ks.reshape(bsz, seq_len, -1), vs_t.reshape(bsz, nc, g * hd, t),
      kw.reshape(bsz, seq_len, -1), vw_t.reshape(bsz, nc, g * hd, t), small_t.reshape(bsz, nc, SMALL_W, t), ovl)


def _dsa_kernel(q_ref, k_ref, v_ref, iq_ref, ik_ref, sm_ref, o_ref,
                qm_ref, iqm_ref, sc_ref, s_ref, acc_ref, *, t, seq_len, k_top):
    i = pl.program_id(1)
    t0 = i * t
    nck = i + 1
    hd = HEAD_DIM
    tok = t0 + lax.broadcasted_iota(jnp.int32, (1, t), 1)

    zeros = jnp.zeros((hd, t), BF16)
    for h in range(Q_HEADS):
        qh = q_ref[0, 0, h * hd:(h + 1) * hd, :]
        qm_ref[h] = jnp.concatenate([qh, zeros] if h % 2 == 0 else [zeros, qh], axis=0)
        iqm_ref[h] = jnp.concatenate([iq_ref[0, 0, h * hd:(h + 1) * hd, :], zeros], axis=0)
    small = sm_ref[0, 0]

    def key_pos(c):
        return pl.multiple_of(c * t, t) + lax.broadcasted_iota(jnp.int32, (t, 1), 0)

    def score_body(c, carry):
        ik = ik_ref[0, pl.ds(pl.multiple_of(c * t, t), t), :]
        sc = jnp.zeros((t, t), F32)
        for h in range(IDX_HEADS):
            sc = sc + jnp.maximum(_dot(ik, iqm_ref[h]), 0.0) * small[N_GATE + h:N_GATE + h + 1]
        sc_ref[c] = jnp.where(key_pos(c) <= tok, sc, -jnp.inf)
        return carry

    _loop_pairs(nck, score_body, 0)

    def count(indicator):
        def body(c, acc):
            return acc + _fold(indicator(sc_ref[c], key_pos(c)), jnp.add)
        acc = _loop_pairs(nck, body, jnp.zeros((SUBLANE, t), F32))
        return jnp.sum(acc, axis=0, keepdims=True)

    kf = float(k_top)

    def decode(code):
        return lax.bitcast_convert_type(jnp.where(code < 0, code ^ jnp.int32(0x7FFFFFFF), code), F32)

    def reached(code):
        val = decode(code)
        return (count(lambda sc, pos: jnp.where(sc >= val, 1.0, 0.0)) >= kf) | (code <= KEY_NEG_INF)

    zero = jnp.zeros((1, t), jnp.int32)

    def bisect():
        base = jnp.where(reached(zero), zero, jnp.full((1, t), INT_MIN, jnp.int32))

        def bit_body(b, base):
            cand = base + lax.shift_left(jnp.int32(1), 30 - b)
            return jnp.where(reached(cand), cand, base)

        return lax.fori_loop(0, 31, bit_body, base)

    kth_code = lax.cond(nck * t <= k_top, lambda: jnp.full((1, t), KEY_NEG_INF, jnp.int32), bisect)
    kth = decode(jnp.maximum(kth_code, KEY_NEG_INF))

    n_ge = count(lambda sc, pos: jnp.where(sc >= kth, 1.0, 0.0))
    n_gt = count(lambda sc, pos: jnp.where(sc > kth, 1.0, 0.0))
    tie = (n_ge > kf) & (kth > -jnp.inf)

    def tie_cut():
        room = kf - n_gt
        top_bit = seq_len.bit_length() - 1

        def pos_body(b, x):
            cand = x + lax.shift_left(jnp.int32(1), top_bit - b)
            n = count(lambda sc, pos: jnp.where(sc == kth, jnp.where(pos < cand, 1.0, 0.0), 0.0))
            return jnp.where(n <= room, cand, x)

        x = lax.fori_loop(0, top_bit + 1, pos_body, zero)
        return jnp.where(tie, x, seq_len)

    cut = lax.cond(jnp.max(jnp.where(tie, 1.0, 0.0)) > 0.0, tie_cut,
                   lambda: jnp.full((1, t), seq_len, jnp.int32))
    cut = jnp.where(kth > -jnp.inf, cut, 0)

    def att_scores(c, mx):
        k0 = pl.multiple_of(c * t, t)
        sc = sc_ref[c]
        bias = jnp.where(sc > kth, 0.0, jnp.where(sc == kth, jnp.where(key_pos(c) < cut, 0.0, NEG), NEG))
        out = []
        for h in range(Q_HEADS):
            s = _dot(k_ref[0, pl.ds(k0, t), (h // 2) * LANE:(h // 2 + 1) * LANE], qm_ref[h]) + bias
            s_ref[h, c] = s
            out.append(jnp.maximum(mx[h], _fold(s, jnp.maximum)))
        return tuple(out)

    mx = _loop_pairs(nck, att_scores, (jnp.full((SUBLANE, t), NEG, F32),) * Q_HEADS)
    m = [jnp.max(mx[h], axis=0, keepdims=True) for h in range(Q_HEADS)]

    acc_ref[...] = jnp.zeros(acc_ref.shape, F32)

    def att_values(c, carry):
        for h in range(Q_HEADS):
            acc_ref[h] += _dot(_with_ones(v_ref[0, c, h * hd:(h + 1) * hd, :]), _weights(s_ref[h, c], m[h]))
        return carry

    _loop_pairs(nck, att_values, 0)
    out = jnp.concatenate([acc_ref[h, :hd] * (1.0 / acc_ref[h, hd:hd + 1])
                           for h in range(Q_HEADS)], axis=0)
    o_ref[0] = out.T.astype(o_ref.dtype)


def _dsa(q_t, k, v_t, iq_t, ik, small_t, bsz, seq_len, t):
    k_top = min(DSA_TOPK_MAX, seq_len // 4)
    assert t >= k_top
    nc = seq_len // t
    qt_spec = pl.BlockSpec((1, 1, Q_WIDTH, t), lambda b, i: (b, i, 0, 0))
    return pl.pallas_call(
        functools.partial(_dsa_kernel, t=t, seq_len=seq_len, k_top=k_top),
        grid=(bsz, nc),
        in_specs=[qt_spec,
                  pl.BlockSpec((1, seq_len, Q_WIDTH), lambda b, i: (b, 0, 0)),
                  pl.BlockSpec((1, nc, Q_WIDTH, t), lambda b, i: (b, 0, 0, 0)),
                  qt_spec,
                  pl.BlockSpec((1, seq_len, LANE), lambda b, i: (b, 0, 0)),
                  pl.BlockSpec((1, 1, SMALL_W, t), lambda b, i: (b, i, 0, 0))],
        out_specs=pl.BlockSpec((1, t, Q_WIDTH), lambda b, i: (b, i, 0)),
        out_shape=jax.ShapeDtypeStruct((bsz, seq_len, Q_WIDTH), BF16),
        scratch_shapes=[pltpu.VMEM((Q_HEADS, LANE, t), BF16), pltpu.VMEM((IDX_HEADS, LANE, t), BF16),
                        pltpu.VMEM((nc, t, t), F32), pltpu.VMEM((Q_HEADS, nc, t, t), F32),
                        pltpu.VMEM((Q_HEADS, HEAD_DIM + ONES_ROWS, t), F32)],
        compiler_params=_params("parallel", "parallel"),
        name="dsa_attn",
    )(q_t.reshape(bsz, nc, Q_WIDTH, t), k.reshape(bsz, seq_len, -1), v_t.reshape(bsz, nc, Q_WIDTH, t),
      iq_t.reshape(bsz, nc, Q_WIDTH, t), ik.reshape(bsz, seq_len, -1), small_t.reshape(bsz, nc, SMALL_W, t))


def _merge_ffn_kernel(x_ref, yn_ref, yd_ref, gn_ref, gd_ref, wbn_ref, wbd_ref, wo_ref,
                      g_ref, wg_ref, wu_ref, wd_ref, fg_ref, o_ref, *, final):
    merged = gn_ref[...] * _dot(yn_ref[...], wbn_ref[...]) + gd_ref[...] * _dot(yd_ref[...], wbd_ref[...])
    x = x_ref[...] + _dot(merged.astype(BF16), wo_ref[...])
    h = _rmsnorm(x, g_ref[...]).astype(BF16)
    z = (jax.nn.silu(_dot(h, wg_ref[...])) * _dot(h, wu_ref[...])).astype(BF16)
    y = x + _dot(z, wd_ref[...])
    o_ref[...] = _rmsnorm(y, fg_ref[...]) if final else y


def _merge_ffn(xf, y_nsa, y_dsa, gate_nsa, gate_dsa, wbn, wbd, wo, g, wg, wu, wd, final_g, final, tm):
    t, d = xf.shape
    f = wg.shape[1]
    wn = y_nsa.shape[1]
    row = lambda i: (i, 0)
    return pl.pallas_call(
        functools.partial(_merge_ffn_kernel, final=final),
        grid=(t // tm,),
        in_specs=[pl.BlockSpec((tm, d), row), pl.BlockSpec((tm, wn), row), pl.BlockSpec((tm, wn), row),
                  pl.BlockSpec((tm, d), row), pl.BlockSpec((tm, d), row),
                  _const_spec((wn, d)), _const_spec((wn, d)), _const_spec((d, d)),
                  _const_spec((1, d)), _const_spec((d, f)), _const_spec((d, f)), _const_spec((f, d)),
                  _const_spec((1, d))],
        out_specs=pl.BlockSpec((tm, d), row),
        out_shape=jax.ShapeDtypeStruct((t, d), F32),
        compiler_params=_params("parallel"),
        name="merge_ffn",
    )(xf, y_nsa, y_dsa, gate_nsa, gate_dsa, wbn.astype(BF16), wbd.astype(BF16), wo.astype(BF16),
      g.reshape(1, d), wg.astype(BF16), wu.astype(BF16), wd.astype(BF16), final_g.reshape(1, d))


def _rope_tables(seq_len, tm):
    pos = jnp.arange(seq_len, dtype=F32)
    inv_freq = ROPE_THETA ** (-jnp.arange(0, ROPE_DIM, 2, dtype=F32) / ROPE_DIM)
    ang = pos[:, None] * inv_freq[None, :]
    cos, sin = jnp.cos(ang), jnp.sin(ang)
    rest = HEAD_DIM - ROPE_DIM
    one, zero = jnp.ones((seq_len, rest), F32), jnp.zeros((seq_len, rest), F32)
    zh = jnp.zeros((seq_len, ROPE_HALF), F32)
    two = lambda a: jnp.concatenate([a, a], axis=1)
    tile_t = lambda a: a.reshape(seq_len // tm, tm, ROPE_HALF).transpose(0, 2, 1)
    return (two(jnp.concatenate([cos, cos, one], axis=1)),
            two(jnp.concatenate([zh, sin, zero], axis=1)),
            two(jnp.concatenate([-sin, zh, zero], axis=1)),
            tile_t(cos), tile_t(sin))


def _tile(seq_len):
    return min(256, seq_len)


def kernel(x, attn_norm, w_in, cmpk_pos, cmpk_w1, cmpk_b1, cmpk_w2, cmpk_b2, cmpv_pos, cmpv_w1, cmpv_b1,
           cmpv_w2, cmpv_b2, w_branch_nsa, w_branch_dsa, w_out, ffn_norm, w_ffn_gate, w_ffn_up, w_ffn_down,
           final_norm):
    bsz, seq_len, d = x.shape
    depth = w_in.shape[0]
    t = _tile(seq_len)
    tables = _rope_tables(seq_len, t)
    xf = x.reshape(bsz * seq_len, d)
    for l in range(depth):
        (q_nsa, k_cmp, v_cmp, k_slc, v_slc, k_win, v_win, q_dsa, k_dsa, v_dsa, q_idx, k_idx,
         gate_nsa, gate_dsa, small) = _proj(xf, attn_norm[l], *_split_w_in(w_in[l]), tables, seq_len, t)
        kc, vc = _compress(k_cmp, v_cmp, cmpk_pos[l], cmpk_w1[l], cmpk_b1[l], cmpk_w2[l], cmpk_b2[l],
                           cmpv_pos[l], cmpv_w1[l], cmpv_b1[l], cmpv_w2[l], cmpv_b2[l], bsz, seq_len)
        y_nsa = _nsa(q_nsa, kc, vc, k_slc, v_slc, k_win, v_win, small, bsz, seq_len, t)
        y_dsa = _dsa(q_dsa, k_dsa, v_dsa, q_idx, k_idx, small, bsz, seq_len, t)
        xf = _merge_ffn(xf, y_nsa.reshape(bsz * seq_len, -1), y_dsa.reshape(bsz * seq_len, -1),
                        gate_nsa, gate_dsa, w_branch_nsa[l], w_branch_dsa[l], w_out[l],
                        ffn_norm[l], w_ffn_gate[l], w_ffn_up[l], w_ffn_down[l], final_norm,
                        final=(l == depth - 1), tm=t)
    return xf.reshape(bsz, seq_len, d)
```

```python
import functools

import numpy as np
import jax
import jax.numpy as jnp
from jax import lax
from jax.experimental import pallas as pl
from jax.experimental.pallas import tpu as pltpu

HEAD_DIM = 64
ROPE_DIM = HEAD_DIM // 4
ROPE_HALF = ROPE_DIM // 2
ROPE_THETA = 500000.0
NORM_EPS = 1e-6
ATTN_SCALE = HEAD_DIM ** -0.5
Q_SCALE = ATTN_SCALE * float(np.log2(np.e))
Q_HEADS = 8
Q_WIDTH = Q_HEADS * HEAD_DIM
NSA_KV_GROUPS = 2
NSA_GROUP_SIZE = Q_HEADS // NSA_KV_GROUPS
CMP_BLOCK = 32
CMP_STRIDE = 16
CMP_HIDDEN = 2 * HEAD_DIM
SLC_BLOCK = 64
N_SLC = 16
N_LOCAL_BLOCKS = 2
SEL_FORCE_SCORE = 1e4
WINDOW = 512
IDX_HEADS = 8
IDX_DIM = 64
IDX_WEIGHT_SCALE = (IDX_HEADS * IDX_DIM) ** -0.5
DSA_TOPK_MAX = 256
N_GATE = 3 * Q_HEADS
SMALL_W = N_GATE + IDX_HEADS
MAX_SLC_BLOCKS = 32

LANE = 128
SUBLANE = 8
ONES_ROWS = 16
V7X_VMEM_LIMIT = 56 * 1024 * 1024

NEG = -1e30
INT_MIN = -(2 ** 31)
KEY_NEG_INF = -2139095041
BF16 = jnp.bfloat16
F32 = jnp.float32

_NT = (((1,), (1,)), ((), ()))


def _dot(a, b):
    return jnp.dot(a, b, preferred_element_type=F32)


def _dot_nt(a, b):
    return lax.dot_general(a, b, _NT, preferred_element_type=F32)


def _const_spec(shape):
    nd = len(shape)
    return pl.BlockSpec(shape, lambda *_: (0,) * nd, pipeline_mode=pl.Buffered(1))


def _rmsnorm(x, g):
    ms = jnp.mean(x * x, axis=-1, keepdims=True)
    return x * lax.rsqrt(ms + NORM_EPS) * g


def _lane_tile(a, reps):
    return a if reps == 1 else jnp.concatenate([a] * reps, axis=1)


def _params(*sem):
    return pltpu.CompilerParams(dimension_semantics=sem, vmem_limit_bytes=V7X_VMEM_LIMIT)


_PROJ_SEGS = (
    ("t", Q_WIDTH, "ropeq", BF16),
    ("n", 128, "rope", F32),
    ("n", 128, "plain", F32),
    ("n", 256, "rope_onehot", BF16),
    ("t", 128, "plain", BF16),
    ("n", 256, "rope", BF16),
    ("t", 128, "plain", BF16),
    ("t", Q_WIDTH, "ropeq", BF16),
    ("n", Q_WIDTH, "rope", BF16),
    ("t", Q_WIDTH, "plain", BF16),
    ("t", IDX_HEADS * IDX_DIM, "rope", BF16),
    ("n", 128, "rope", BF16),
    ("n", 1024, "sig", F32),
    ("n", 1024, "sig", F32),
    ("t", SMALL_W, "small", F32),
)
_PROJ_NW = sum(s[1] for s in _PROJ_SEGS if s[0] == "n")
_PROJ_TW = sum(s[1] for s in _PROJ_SEGS if s[0] == "t")
_PROJ_CHUNK = 512


def _rope_n(y, c, sa, sb):
    pieces = []
    for j in range(y.shape[1] // LANE):
        yj = y[:, j * LANE:(j + 1) * LANE]
        pieces.append(yj * c + pltpu.roll(yj, ROPE_HALF, 1) * sa + pltpu.roll(yj, LANE - ROPE_HALF, 1) * sb)
    return pieces[0] if len(pieces) == 1 else jnp.concatenate(pieces, axis=1)


def _rope_t(y, cos, sin):
    pieces = []
    for h in range(y.shape[0] // HEAD_DIM):
        r0 = h * HEAD_DIM
        x1, x2 = y[r0:r0 + ROPE_HALF], y[r0 + ROPE_HALF:r0 + ROPE_DIM]
        pieces += [x1 * cos - x2 * sin, x2 * cos + x1 * sin, y[r0 + ROPE_DIM:r0 + HEAD_DIM]]
    return jnp.concatenate(pieces, axis=0)


def _proj_kernel(x_ref, g_ref, wn_ref, wt_ref, c_ref, sa_ref, sb_ref, ct_ref, st_ref, *out_refs, npos):
    tm = x_ref.shape[0]
    h = _rmsnorm(x_ref[...], g_ref[...]).astype(BF16)
    c, sa, sb = c_ref[...], sa_ref[...], sb_ref[...]
    cos_t, sin_t = ct_ref[0], st_ref[0]
    ncol = trow = 0
    for (layout, width, kind, _), o_ref in zip(_PROJ_SEGS, out_refs):
        if layout == "t":
            y = _dot_nt(wt_ref[trow:trow + width, :], h)
            trow += width
            if kind in ("rope", "ropeq"):
                y = _rope_t(y, cos_t, sin_t)
            if kind == "ropeq":
                y = y * Q_SCALE
            if kind == "small":
                row = lax.broadcasted_iota(jnp.int32, y.shape, 0)
                y = jnp.where(row < N_GATE, jax.nn.sigmoid(y), y * IDX_WEIGHT_SCALE)
            o_ref[0] = y.astype(o_ref.dtype)
            continue
        for c0 in range(0, width, _PROJ_CHUNK):
            cw = min(_PROJ_CHUNK, width - c0)
            y = _dot(h, wn_ref[:, ncol + c0:ncol + c0 + cw])
            if kind in ("rope", "rope_onehot"):
                y = _rope_n(y, c, sa, sb)
            if kind == "rope_onehot":
                pos = (pl.program_id(0) % npos) * tm + lax.broadcasted_iota(jnp.int32, y.shape, 0)
                lane = lax.broadcasted_iota(jnp.int32, y.shape, 1) & (LANE - 1)
                y = jnp.where(lane == HEAD_DIM + pos // SLC_BLOCK, 1.0, y)
            if kind == "sig":
                y = jax.nn.sigmoid(y)
            o_ref[:, c0:c0 + cw] = y.astype(o_ref.dtype)
        ncol += width


def _split_w_in(w):
    d = w.shape[0]
    kvw = NSA_KV_GROUPS * HEAD_DIM
    sizes = (Q_WIDTH, kvw, kvw, kvw, kvw, kvw, kvw, N_GATE, Q_WIDTH, Q_WIDTH, Q_WIDTH,
             IDX_HEADS * IDX_DIM, IDX_DIM, IDX_HEADS, d, d)
    assert sum(sizes) == w.shape[1]
    offs = np.concatenate([[0], np.cumsum(sizes)])
    (nsa_q, k_cmp, v_cmp, k_slc, v_slc, k_win, v_win, nsa_gate, dsa_q, dsa_k, dsa_v,
     idx_q, idx_k, idx_w, gate_nsa, gate_dsa) = [w[:, offs[i]:offs[i + 1]] for i in range(16)]

    def pad_groups(a):
        a = a.reshape(d, NSA_KV_GROUPS, HEAD_DIM)
        return jnp.concatenate([a, jnp.zeros_like(a)], axis=2).reshape(d, NSA_KV_GROUPS * LANE)

    idx_k_pad = jnp.concatenate([idx_k, jnp.zeros_like(idx_k)], axis=1)
    w_n = jnp.concatenate([k_cmp, v_cmp, pad_groups(k_slc), pad_groups(k_win), dsa_k, idx_k_pad,
                           gate_nsa, gate_dsa], axis=1)
    w_t = jnp.concatenate([nsa_q, v_slc, v_win, dsa_q, dsa_v, idx_q, nsa_gate, idx_w], axis=1).T
    assert w_n.shape[1] == _PROJ_NW and w_t.shape[0] == _PROJ_TW
    return w_n.astype(BF16), w_t.astype(BF16)


def _proj(xf, g, w_n, w_t, tables, seq_len, tm):
    t, d = xf.shape
    npos = seq_len // tm
    row = lambda i: (i, 0)
    tab_n = pl.BlockSpec((tm, LANE), lambda i: (i % npos, 0))
    tab_t = pl.BlockSpec((1, ROPE_HALF, tm), lambda i: (i % npos, 0, 0))
    out_specs, out_shape = [], []
    for layout, width, _, dt in _PROJ_SEGS:
        if layout == "t":
            out_specs.append(pl.BlockSpec((1, width, tm), lambda i: (i, 0, 0)))
            out_shape.append(jax.ShapeDtypeStruct((t // tm, width, tm), dt))
        else:
            out_specs.append(pl.BlockSpec((tm, width), row))
            out_shape.append(jax.ShapeDtypeStruct((t, width), dt))
    return pl.pallas_call(
        functools.partial(_proj_kernel, npos=npos),
        grid=(t // tm,),
        in_specs=[pl.BlockSpec((tm, d), row), _const_spec((1, d)), _const_spec((d, _PROJ_NW)),
                  _const_spec((_PROJ_TW, d)), tab_n, tab_n, tab_n, tab_t, tab_t],
        out_specs=out_specs,
        out_shape=out_shape,
        compiler_params=_params("parallel"),
        name="proj",
    )(xf, g.reshape(1, d), w_n, w_t, *tables)


def _cmp_kernel(k_ref, v_ref, pk_ref, w1k_ref, b1k_ref, w2k_ref, b2k_ref,
                pv_ref, w1v_ref, b1v_ref, w2v_ref, b2v_ref, kc_ref, vc_ref, *, n_cmp):
    nb = kc_ref.shape[1]

    def hidden(x_ref, pos_ref, w1_ref, b1_ref):
        halves = []
        for half in range(CMP_BLOCK // CMP_STRIDE):
            acc = None
            for l in range(CMP_STRIDE):
                j = half * CMP_STRIDE + l
                x = x_ref[0, pl.ds(l, nb, stride=CMP_STRIDE), :] + pos_ref[j:j + 1, :]
                part = _dot(x.astype(BF16), w1_ref[j])
                acc = part if acc is None else acc + part
            halves.append(acc)
        return jax.nn.gelu(halves[0] + pltpu.roll(halves[1], nb - 1, 0) + b1_ref[...]).astype(BF16)

    kc = _dot(hidden(k_ref, pk_ref, w1k_ref, b1k_ref), w2k_ref[...]) + b2k_ref[...]
    rows = lax.broadcasted_iota(jnp.int32, kc.shape, 0)
    kc_ref[0] = jnp.where(rows < n_cmp, kc, 0.0).astype(kc_ref.dtype)
    vc = _dot_nt(w2v_ref[...], hidden(v_ref, pv_ref, w1v_ref, b1v_ref)) + b2v_ref[...]
    cols = lax.broadcasted_iota(jnp.int32, vc.shape, 1)
    vc_ref[0] = jnp.where(cols < n_cmp, vc, 0.0).astype(vc_ref.dtype)


def _compress(k_cmp, v_cmp, pk, w1k, b1k, w2k, b2k, pv, w1v, b1v, w2v, b2v, bsz, seq_len):
    g, hd = NSA_KV_GROUPS, HEAD_DIM
    assert CMP_BLOCK == 2 * CMP_STRIDE
    nb = seq_len // CMP_STRIDE
    n_cmp = (seq_len - CMP_BLOCK) // CMP_STRIDE + 1
    eye = jnp.eye(g, dtype=F32)
    per_group = lambda m: jnp.kron(eye, m)
    tile_g = lambda a: jnp.concatenate([a] * g, axis=-1)

    def first_layer(pos, w1, b1):
        w1_bd = jax.vmap(per_group)(w1.reshape(CMP_BLOCK, hd, CMP_HIDDEN)).astype(BF16)
        return tile_g(pos), w1_bd, tile_g(b1).reshape(1, -1)

    w2k_pad = jnp.concatenate([w2k, jnp.zeros_like(w2k)], axis=1)
    b2k_pad = jnp.concatenate([b2k, jnp.zeros_like(b2k)])
    k_w = first_layer(pk, w1k, b1k) + (per_group(w2k_pad).astype(BF16), tile_g(b2k_pad).reshape(1, -1))
    v_w = first_layer(pv, w1v, b1v) + (per_group(w2v.T).astype(BF16), tile_g(b2v).reshape(-1, 1))
    seq_spec = pl.BlockSpec((1, seq_len, g * hd), lambda b: (b, 0, 0))
    first = [_const_spec((CMP_BLOCK, g * hd)), _const_spec((CMP_BLOCK, g * hd, g * CMP_HIDDEN)),
             _const_spec((1, g * CMP_HIDDEN))]
    return pl.pallas_call(
        functools.partial(_cmp_kernel, n_cmp=n_cmp),
        grid=(bsz,),
        in_specs=[seq_spec, seq_spec] + first + [_const_spec((g * CMP_HIDDEN, g * LANE)), _const_spec((1, g * LANE))]
        + first + [_const_spec((g * hd, g * CMP_HIDDEN)), _const_spec((g * hd, 1))],
        out_specs=[pl.BlockSpec((1, nb, g * LANE), lambda b: (b, 0, 0)),
                   pl.BlockSpec((1, g * hd, nb), lambda b: (b, 0, 0))],
        out_shape=[jax.ShapeDtypeStruct((bsz, nb, g * LANE), BF16),
                   jax.ShapeDtypeStruct((bsz, g * hd, nb), BF16)],
        compiler_params=_params("parallel"),
        name="compress",
    )(k_cmp.reshape(bsz, seq_len, g * hd), v_cmp.reshape(bsz, seq_len, g * hd), *k_w, *v_w)


def _fold(x, op):
    parts = [x[r * SUBLANE:(r + 1) * SUBLANE] for r in range(x.shape[0] // SUBLANE)]
    while len(parts) > 1:
        pairs = [op(parts[k], parts[k + 1]) for k in range(0, len(parts) - 1, 2)]
        parts = pairs + ([parts[-1]] if len(parts) % 2 else [])
    return parts[0]


def _loop_pairs(n, body, init):
    carry = lax.fori_loop(0, n // 2, lambda j, c: body(2 * j + 1, body(2 * j, c)), init)
    return lax.cond(n % 2 == 1, lambda c: body(n - 1, c), lambda c: c, carry)


def _weights(s, m):
    return jnp.exp2((s - m).astype(BF16))


def _with_ones(v_t):
    return jnp.concatenate([v_t, jnp.ones((ONES_ROWS, v_t.shape[1]), v_t.dtype)], axis=0)


def _nsa_kernel(q_ref, kc_ref, vc_ref, ks_ref, vs_ref, kw_ref, vw_ref, sm_ref, ovl_ref, o_ref,
                qm_ref, ss_ref, sw_ref, as_ref, *, t, seq_len):
    r_heads = NSA_GROUP_SIZE
    gi = pl.program_id(1)
    i = pl.program_id(2)
    t0 = i * t
    rows = r_heads * t
    hd = HEAD_DIM

    for r in range(r_heads):
        qm_ref[0:hd, r * t:(r + 1) * t] = q_ref[0, 0, r * hd:(r + 1) * hd, :]
    qm_ref[hd:, :] = jnp.zeros((LANE - hd, rows), BF16)
    tok = t0 + lax.broadcasted_iota(jnp.int32, (1, t), 1)
    tok_r = _lane_tile(tok, r_heads)

    nb = kc_ref.shape[1]
    s_c = _dot(kc_ref[0], qm_ref[...])
    c_end = lax.broadcasted_iota(jnp.int32, (nb, 1), 0) * CMP_STRIDE + (CMP_BLOCK - 1)
    mask_c = c_end <= tok_r
    s_c = jnp.where(mask_c, s_c, NEG)
    s_c = s_c - jnp.max(s_c, axis=0, keepdims=True)
    p_c = jnp.where(mask_c, jnp.exp2(s_c), 0.0)
    p_c = (p_c / jnp.maximum(jnp.sum(p_c, axis=0, keepdims=True), 1e-30)).astype(BF16)
    o_c = _dot(vc_ref[0], p_c)
    imp4 = _dot(ovl_ref[...], p_c)
    imp = imp4[:, 0:t]
    for r in range(1, r_heads):
        imp = imp + imp4[:, r * t:(r + 1) * t]

    n_slc = seq_len // SLC_BLOCK
    n_sel = min(N_SLC, n_slc)
    blk = lax.broadcasted_iota(jnp.int32, (MAX_SLC_BLOCKS, 1), 0)
    dist = (tok // SLC_BLOCK) - blk
    forced = (blk == 0) | ((dist >= 0) & (dist < N_LOCAL_BLOCKS))
    score = jnp.where(forced, SEL_FORCE_SCORE, imp)
    score = jnp.where(blk * SLC_BLOCK <= tok, score, -jnp.inf)
    groups = [score[v * SUBLANE:(v + 1) * SUBLANE] for v in range(MAX_SLC_BLOCKS // SUBLANE)]
    ranks = [jnp.zeros((SUBLANE, t), F32) for _ in groups]
    sub = lax.broadcasted_iota(jnp.int32, (SUBLANE, 1), 0)
    for j in range(n_slc):
        row = score[j:j + 1]
        for v, sv in enumerate(groups):
            ge = jnp.where(row >= sv, 1.0, 0.0)
            gt = jnp.where(row > sv, 1.0, 0.0)
            if v * SUBLANE > j:
                beats = ge
            elif (v + 1) * SUBLANE <= j + 1:
                beats = gt
            else:
                beats = jnp.where(sub + v * SUBLANE > j, ge, gt)
            ranks[v] = ranks[v] + beats
    selbias = jnp.where(jnp.concatenate(ranks, axis=0) < n_sel, 0.0, NEG).astype(BF16)
    for r in range(r_heads):
        qm_ref[hd:hd + MAX_SLC_BLOCKS, r * t:(r + 1) * t] = selbias
    key_off = lax.broadcasted_iota(jnp.int32, (t, 1), 0)
    q_off = lax.broadcasted_iota(jnp.int32, (1, t), 1)
    causal = _lane_tile(jnp.where(key_off <= q_off, 0.0, NEG), r_heads)
    neg8 = jnp.full((SUBLANE, rows), NEG, F32)

    def key_tile(k_ref_, c):
        return k_ref_[0, pl.ds(pl.multiple_of(c * t, t), t), :]

    def slc_scores(c, mx):
        s = _dot(key_tile(ks_ref, c), qm_ref[...])
        ss_ref[c] = s
        return jnp.maximum(mx, _fold(s, jnp.maximum))

    mx = _loop_pairs(i, slc_scores, neg8)
    s = _dot(key_tile(ks_ref, i), qm_ref[...]) + causal
    ss_ref[i] = s
    m_s = jnp.max(jnp.maximum(mx, _fold(s, jnp.maximum)), axis=0, keepdims=True)

    n_back = -(-WINDOW // t)
    win_tiles = []
    mxw = neg8
    for back in range(n_back, -1, -1):
        if back:
            lo_edge = back * t - WINDOW
            bias = _lane_tile(jnp.where(key_off - q_off > jnp.where(i >= back, lo_edge, t), 0.0, NEG), r_heads)
        else:
            bias = causal
        c = jnp.maximum(i - back, 0)
        s = _dot(key_tile(kw_ref, c), qm_ref[...]) + bias
        sw_ref[n_back - back] = s
        mxw = jnp.maximum(mxw, _fold(s, jnp.maximum))
        win_tiles.append(c)
    m_w = jnp.max(mxw, axis=0, keepdims=True)

    as_ref[...] = jnp.zeros(as_ref.shape, F32)

    def slc_values(c, carry):
        as_ref[...] += _dot(_with_ones(vs_ref[0, c]), _weights(ss_ref[c], m_s))
        return carry

    _loop_pairs(i + 1, slc_values, 0)
    o_s = as_ref[:hd] * (1.0 / as_ref[hd:hd + 1])

    o_w = jnp.zeros(as_ref.shape, F32)
    for n, c in enumerate(win_tiles):
        o_w = o_w + _dot(_with_ones(vw_ref[0, c]), _weights(sw_ref[n], m_w))
    o_w = o_w[:hd] * (1.0 / o_w[hd:hd + 1])

    gates = sm_ref[0, 0]

    def gate(branch, r):
        rows_g = [gates[branch * Q_HEADS + g * r_heads + r:branch * Q_HEADS + g * r_heads + r + 1]
                  for g in range(NSA_KV_GROUPS)]
        out = rows_g[0]
        for g in range(1, NSA_KV_GROUPS):
            out = jnp.where(gi == g, rows_g[g], out)
        return out

    heads = []
    for r in range(r_heads):
        sl = slice(r * t, (r + 1) * t)
        heads.append(gate(0, r) * o_c[:, sl] + gate(1, r) * o_s[:, sl] + gate(2, r) * o_w[:, sl])
    o_ref[0] = jnp.concatenate(heads, axis=0).T.astype(o_ref.dtype)


def _overlap_t(seq_len):
    nb = seq_len // CMP_STRIDE
    n_cmp = (seq_len - CMP_BLOCK) // CMP_STRIDE + 1
    n_slc = seq_len // SLC_BLOCK
    c_start = np.arange(n_cmp)[None, :] * CMP_STRIDE
    s_start = np.arange(n_slc)[:, None] * SLC_BLOCK
    ov = np.minimum(c_start + CMP_BLOCK, s_start + SLC_BLOCK) - np.maximum(c_start, s_start)
    ovl = np.zeros((MAX_SLC_BLOCKS, nb), np.float32)
    ovl[:n_slc, :n_cmp] = np.clip(ov, 0, None) / CMP_BLOCK
    return jnp.asarray(ovl, BF16)


def _nsa(q_t, kc, vc_t, ks, vs_t, kw, vw_t, small_t, bsz, seq_len, t):
    g, r, hd = NSA_KV_GROUPS, NSA_GROUP_SIZE, HEAD_DIM
    assert seq_len // SLC_BLOCK <= MAX_SLC_BLOCKS
    nb = seq_len // CMP_STRIDE
    nc = seq_len // t
    rows = r * t
    ovl = _overlap_t(seq_len)
    k_spec = pl.BlockSpec((1, seq_len, LANE), lambda b, gi, i: (b, 0, gi))
    v_spec = pl.BlockSpec((1, nc, hd, t), lambda b, gi, i: (b, 0, gi, 0))
    return pl.pallas_call(
        functools.partial(_nsa_kernel, t=t, seq_len=seq_len),
        grid=(bsz, g, nc),
        in_specs=[pl.BlockSpec((1, 1, r * hd, t), lambda b, gi, i: (b, i, gi, 0)),
                  pl.BlockSpec((1, nb, LANE), lambda b, gi, i: (b, 0, gi)),
                  pl.BlockSpec((1, hd, nb), lambda b, gi, i: (b, gi, 0)),
                  k_spec, v_spec, k_spec, v_spec,
                  pl.BlockSpec((1, 1, SMALL_W, t), lambda b, gi, i: (b, i, 0, 0)),
                  _const_spec(ovl.shape)],
        out_specs=pl.BlockSpec((1, t, r * hd), lambda b, gi, i: (b, i, gi)),
        out_shape=jax.ShapeDtypeStruct((bsz, seq_len, Q_WIDTH), BF16),
        scratch_shapes=[pltpu.VMEM((LANE, rows), BF16), pltpu.VMEM((nc, t, rows), F32),
                        pltpu.VMEM((-(-WINDOW // t) + 1, t, rows), F32),
                        pltpu.VMEM((hd + ONES_ROWS, rows), F32)],
        compiler_params=_params("parallel", "parallel", "parallel"),
        name="nsa_attn",
    )(q_t.reshape(bsz, nc, Q_WIDTH, t), kc, vc_t, ks.reshape(bsz, seq_len, -1), vs_t.reshape(bsz, nc, g * hd, t),
      kw.reshape(bsz, seq_len, -1), vw_t.reshape(bsz, nc, g * hd, t), small_t.reshape(bsz, nc, SMALL_W, t), ovl)


def _dsa_kernel(q_ref, k_ref, v_ref, iq_ref, ik_ref, sm_ref, o_ref,
                qm_ref, iqm_ref, sc_ref, s_ref, acc_ref, kth_ref, *, t, seq_len, k_top):
    i = pl.program_id(1)
    t0 = i * t
    nck = i + 1
    hd = HEAD_DIM
    tok = t0 + lax.broadcasted_iota(jnp.int32, (1, t), 1)

    zeros = jnp.zeros((hd, t), BF16)
    for h in range(Q_HEADS):
        qh = q_ref[0, 0, h * hd:(h + 1) * hd, :]
        qm_ref[h] = jnp.concatenate([qh, zeros] if h % 2 == 0 else [zeros, qh], axis=0)
        iqm_ref[h] = jnp.concatenate([iq_ref[0, 0, h * hd:(h + 1) * hd, :], zeros], axis=0)
    small = sm_ref[0, 0]

    def key_pos(c):
        start = c * t if isinstance(c, int) else pl.multiple_of(c * t, t)
        return start + lax.broadcasted_iota(jnp.int32, (t, 1), 0)

    def score_body(c, carry):
        ik = ik_ref[0, pl.ds(pl.multiple_of(c * t, t), t), :]
        sc = jnp.zeros((t, t), F32)
        for h in range(IDX_HEADS):
            sc = sc + jnp.maximum(_dot(ik, iqm_ref[h]), 0.0) * small[N_GATE + h:N_GATE + h + 1]
        sc_ref[c] = jnp.where(key_pos(c) <= tok, sc, -jnp.inf)
        return carry

    _loop_pairs(nck, score_body, 0)

    def count(indicator, n_tiles=None):
        def body(c, acc):
            return acc + _fold(indicator(sc_ref[c], key_pos(c)), jnp.add)
        acc = jnp.zeros((SUBLANE, t), F32)
        if n_tiles is None:
            acc = _loop_pairs(nck, body, acc)
        else:
            for c in range(n_tiles):
                acc = body(c, acc)
        return jnp.sum(acc, axis=0, keepdims=True)

    kf = float(k_top)
    zero = jnp.zeros((1, t), jnp.int32)

    def decode(code):
        return lax.bitcast_convert_type(jnp.where(code < 0, code ^ jnp.int32(0x7FFFFFFF), code), F32)

    def search(n_tiles):
        def reached(code):
            val = decode(code)
            return (count(lambda sc, pos: jnp.where(sc >= val, 1.0, 0.0), n_tiles) >= kf) | (code <= KEY_NEG_INF)

        def bit_body(b, base):
            cand = base + lax.shift_left(jnp.int32(1), 30 - b)
            return jnp.where(reached(cand), cand, base)

        base = jnp.where(reached(zero), zero, jnp.full((1, t), INT_MIN, jnp.int32))
        kth = decode(jnp.maximum(lax.fori_loop(0, 31, bit_body, base), KEY_NEG_INF))
        return (kth, count(lambda sc, pos: jnp.where(sc >= kth, 1.0, 0.0), n_tiles),
                count(lambda sc, pos: jnp.where(sc > kth, 1.0, 0.0), n_tiles))

    for n_tiles in range(1, seq_len // t + 1):
        @pl.when(nck == n_tiles)
        def _(n_tiles=n_tiles):
            if n_tiles * t <= k_top:
                found = (jnp.full((1, t), -jnp.inf, F32), jnp.full((1, t), kf, F32), jnp.zeros((1, t), F32))
            else:
                found = search(n_tiles)
            for r, v in enumerate(found):
                kth_ref[r:r + 1, :] = v

    kth = kth_ref[0:1, :]
    n_ge, n_gt = kth_ref[1:2, :], kth_ref[2:3, :]

    tie = (n_ge > kf) & (kth > -jnp.inf)

    def tie_cut():
        room = kf - n_gt
        top_bit = seq_len.bit_length() - 1

        def pos_body(b, x):
            cand = x + lax.shift_left(jnp.int32(1), top_bit - b)
            n = count(lambda sc, pos: jnp.where(sc == kth, jnp.where(pos < cand, 1.0, 0.0), 0.0))
            return jnp.where(n <= room, cand, x)

        x = lax.fori_loop(0, top_bit + 1, pos_body, zero)
        return jnp.where(tie, x, seq_len)

    cut = lax.cond(jnp.max(jnp.where(tie, 1.0, 0.0)) > 0.0, tie_cut,
                   lambda: jnp.full((1, t), seq_len, jnp.int32))
    cut = jnp.where(kth > -jnp.inf, cut, 0)

    def att_scores(c, mx):
        k0 = pl.multiple_of(c * t, t)
        sc = sc_ref[c]
        bias = jnp.where(sc > kth, 0.0, jnp.where(sc == kth, jnp.where(key_pos(c) < cut, 0.0, NEG), NEG))
        out = []
        for h in range(Q_HEADS):
            s = _dot(k_ref[0, pl.ds(k0, t), (h // 2) * LANE:(h // 2 + 1) * LANE], qm_ref[h]) + bias
            s_ref[h, c] = s
            out.append(jnp.maximum(mx[h], _fold(s, jnp.maximum)))
        return tuple(out)

    mx = _loop_pairs(nck, att_scores, (jnp.full((SUBLANE, t), NEG, F32),) * Q_HEADS)
    m = [jnp.max(mx[h], axis=0, keepdims=True) for h in range(Q_HEADS)]

    acc_ref[...] = jnp.zeros(acc_ref.shape, F32)

    def att_values(c, carry):
        for h in range(Q_HEADS):
            acc_ref[h] += _dot(_with_ones(v_ref[0, c, h * hd:(h + 1) * hd, :]), _weights(s_ref[h, c], m[h]))
        return carry

    _loop_pairs(nck, att_values, 0)
    out = jnp.concatenate([acc_ref[h, :hd] * (1.0 / acc_ref[h, hd:hd + 1])
                           for h in range(Q_HEADS)], axis=0)
    o_ref[0] = out.T.astype(o_ref.dtype)


def _dsa(q_t, k, v_t, iq_t, ik, small_t, bsz, seq_len, t):
    k_top = min(DSA_TOPK_MAX, seq_len // 4)
    assert t >= k_top
    nc = seq_len // t
    qt_spec = pl.BlockSpec((1, 1, Q_WIDTH, t), lambda b, i: (b, i, 0, 0))
    return pl.pallas_call(
        functools.partial(_dsa_kernel, t=t, seq_len=seq_len, k_top=k_top),
        grid=(bsz, nc),
        in_specs=[qt_spec,
                  pl.BlockSpec((1, seq_len, Q_WIDTH), lambda b, i: (b, 0, 0)),
                  pl.BlockSpec((1, nc, Q_WIDTH, t), lambda b, i: (b, 0, 0, 0)),
                  qt_spec,
                  pl.BlockSpec((1, seq_len, LANE), lambda b, i: (b, 0, 0)),
                  pl.BlockSpec((1, 1, SMALL_W, t), lambda b, i: (b, i, 0, 0))],
        out_specs=pl.BlockSpec((1, t, Q_WIDTH), lambda b, i: (b, i, 0)),
        out_shape=jax.ShapeDtypeStruct((bsz, seq_len, Q_WIDTH), BF16),
        scratch_shapes=[pltpu.VMEM((Q_HEADS, LANE, t), BF16), pltpu.VMEM((IDX_HEADS, LANE, t), BF16),
                        pltpu.VMEM((nc, t, t), F32), pltpu.VMEM((Q_HEADS, nc, t, t), F32),
                        pltpu.VMEM((Q_HEADS, HEAD_DIM + ONES_ROWS, t), F32), pltpu.VMEM((SUBLANE, t), F32)],
        compiler_params=_params("parallel", "parallel"),
        name="dsa_attn",
    )(q_t.reshape(bsz, nc, Q_WIDTH, t), k.reshape(bsz, seq_len, -1), v_t.reshape(bsz, nc, Q_WIDTH, t),
      iq_t.reshape(bsz, nc, Q_WIDTH, t), ik.reshape(bsz, seq_len, -1), small_t.reshape(bsz, nc, SMALL_W, t))


def _merge_ffn_kernel(x_ref, yn_ref, yd_ref, gn_ref, gd_ref, wbn_ref, wbd_ref, wo_ref,
                      g_ref, wg_ref, wu_ref, wd_ref, fg_ref, o_ref, *, final):
    merged = gn_ref[...] * _dot(yn_ref[...], wbn_ref[...]) + gd_ref[...] * _dot(yd_ref[...], wbd_ref[...])
    x = x_ref[...] + _dot(merged.astype(BF16), wo_ref[...])
    h = _rmsnorm(x, g_ref[...]).astype(BF16)
    z = (jax.nn.silu(_dot(h, wg_ref[...])) * _dot(h, wu_ref[...])).astype(BF16)
    y = x + _dot(z, wd_ref[...])
    o_ref[...] = _rmsnorm(y, fg_ref[...]) if final else y


def _merge_ffn(xf, y_nsa, y_dsa, gate_nsa, gate_dsa, wbn, wbd, wo, g, wg, wu, wd, final_g, final, tm):
    t, d = xf.shape
    f = wg.shape[1]
    wn = y_nsa.shape[1]
    row = lambda i: (i, 0)
    return pl.pallas_call(
        functools.partial(_merge_ffn_kernel, final=final),
        grid=(t // tm,),
        in_specs=[pl.BlockSpec((tm, d), row), pl.BlockSpec((tm, wn), row), pl.BlockSpec((tm, wn), row),
                  pl.BlockSpec((tm, d), row), pl.BlockSpec((tm, d), row),
                  _const_spec((wn, d)), _const_spec((wn, d)), _const_spec((d, d)),
                  _const_spec((1, d)), _const_spec((d, f)), _const_spec((d, f)), _const_spec((f, d)),
                  _const_spec((1, d))],
        out_specs=pl.BlockSpec((tm, d), row),
        out_shape=jax.ShapeDtypeStruct((t, d), F32),
        compiler_params=_params("parallel"),
        name="merge_ffn",
    )(xf, y_nsa, y_dsa, gate_nsa, gate_dsa, wbn.astype(BF16), wbd.astype(BF16), wo.astype(BF16),
      g.reshape(1, d), wg.astype(BF16), wu.astype(BF16), wd.astype(BF16), final_g.reshape(1, d))


def _rope_tables(seq_len, tm):
    pos = jnp.arange(seq_len, dtype=F32)
    inv_freq = ROPE_THETA ** (-jnp.arange(0, ROPE_DIM, 2, dtype=F32) / ROPE_DIM)
    ang = pos[:, None] * inv_freq[None, :]
    cos, sin = jnp.cos(ang), jnp.sin(ang)
    rest = HEAD_DIM - ROPE_DIM
    one, zero = jnp.ones((seq_len, rest), F32), jnp.zeros((seq_len, rest), F32)
    zh = jnp.zeros((seq_len, ROPE_HALF), F32)
    two = lambda a: jnp.concatenate([a, a], axis=1)
    tile_t = lambda a: a.reshape(seq_len // tm, tm, ROPE_HALF).transpose(0, 2, 1)
    return (two(jnp.concatenate([cos, cos, one], axis=1)),
            two(jnp.concatenate([zh, sin, zero], axis=1)),
            two(jnp.concatenate([-sin, zh, zero], axis=1)),
            tile_t(cos), tile_t(sin))


def _tile(seq_len):
    return min(256, seq_len)


def kernel(x, attn_norm, w_in, cmpk_pos, cmpk_w1, cmpk_b1, cmpk_w2, cmpk_b2, cmpv_pos, cmpv_w1, cmpv_b1,
           cmpv_w2, cmpv_b2, w_branch_nsa, w_branch_dsa, w_out, ffn_norm, w_ffn_gate, w_ffn_up, w_ffn_down,
           final_norm):
    bsz, seq_len, d = x.shape
    depth = w_in.shape[0]
    t = _tile(seq_len)
    tables = _rope_tables(seq_len, t)
    xf = x.reshape(bsz * seq_len, d)
    for l in range(depth):
        (q_nsa, k_cmp, v_cmp, k_slc, v_slc, k_win, v_win, q_dsa, k_dsa, v_dsa, q_idx, k_idx,
         gate_nsa, gate_dsa, small) = _proj(xf, attn_norm[l], *_split_w_in(w_in[l]), tables, seq_len, t)
        kc, vc = _compress(k_cmp, v_cmp, cmpk_pos[l], cmpk_w1[l], cmpk_b1[l], cmpk_w2[l], cmpk_b2[l],
                           cmpv_pos[l], cmpv_w1[l], cmpv_b1[l], cmpv_w2[l], cmpv_b2[l], bsz, seq_len)
        y_nsa = _nsa(q_nsa, kc, vc, k_slc, v_slc, k_win, v_win, small, bsz, seq_len, t)
        y_dsa = _dsa(q_dsa, k_dsa, v_dsa, q_idx, k_idx, small, bsz, seq_len, t)
        xf = _merge_ffn(xf, y_nsa.reshape(bsz * seq_len, -1), y_dsa.reshape(bsz * seq_len, -1),
                        gate_nsa, gate_dsa, w_branch_nsa[l], w_branch_dsa[l], w_out[l],
                        ffn_norm[l], w_ffn_gate[l], w_ffn_up[l], w_ffn_down[l], final_norm,
                        final=(l == depth - 1), tm=t)
    return xf.reshape(bsz, seq_len, d)
```

```python
import functools

import numpy as np
import jax
import jax.numpy as jnp
from jax import lax
from jax.experimental import pallas as pl
from jax.experimental.pallas import tpu as pltpu

HEAD_DIM = 64
ROPE_DIM = HEAD_DIM // 4
ROPE_HALF = ROPE_DIM // 2
ROPE_THETA = 500000.0
NORM_EPS = 1e-6
ATTN_SCALE = HEAD_DIM ** -0.5
Q_SCALE = ATTN_SCALE * float(np.log2(np.e))
Q_HEADS = 8
Q_WIDTH = Q_HEADS * HEAD_DIM
NSA_KV_GROUPS = 2
NSA_GROUP_SIZE = Q_HEADS // NSA_KV_GROUPS
CMP_BLOCK = 32
CMP_STRIDE = 16
CMP_HIDDEN = 2 * HEAD_DIM
SLC_BLOCK = 64
N_SLC = 16
N_LOCAL_BLOCKS = 2
SEL_FORCE_SCORE = 1e4
WINDOW = 512
IDX_HEADS = 8
IDX_DIM = 64
IDX_WEIGHT_SCALE = (IDX_HEADS * IDX_DIM) ** -0.5
DSA_TOPK_MAX = 256
N_GATE = 3 * Q_HEADS
SMALL_W = N_GATE + IDX_HEADS
MAX_SLC_BLOCKS = 32

LANE = 128
SUBLANE = 8
ONES_ROWS = 16
V7X_VMEM_LIMIT = 56 * 1024 * 1024

NEG = -1e30
INT_MIN = -(2 ** 31)
KEY_NEG_INF = -2139095041
KEY16_NEG_INF = -32641
BF16_ROWS = 16
BF16 = jnp.bfloat16
F32 = jnp.float32

_NT = (((1,), (1,)), ((), ()))


def _dot(a, b):
    return jnp.dot(a, b, preferred_element_type=F32)


def _dot_nt(a, b):
    return lax.dot_general(a, b, _NT, preferred_element_type=F32)


def _const_spec(shape):
    nd = len(shape)
    return pl.BlockSpec(shape, lambda *_: (0,) * nd, pipeline_mode=pl.Buffered(1))


def _rmsnorm(x, g):
    ms = jnp.mean(x * x, axis=-1, keepdims=True)
    return x * lax.rsqrt(ms + NORM_EPS) * g


def _lane_tile(a, reps):
    return a if reps == 1 else jnp.concatenate([a] * reps, axis=1)


def _params(*sem):
    return pltpu.CompilerParams(dimension_semantics=sem, vmem_limit_bytes=V7X_VMEM_LIMIT)


_PROJ_SEGS = (
    ("t", Q_WIDTH, "ropeq", BF16),
    ("n", 128, "rope", F32),
    ("n", 128, "plain", F32),
    ("n", 256, "rope_onehot", BF16),
    ("t", 128, "plain", BF16),
    ("n", 256, "rope", BF16),
    ("t", 128, "plain", BF16),
    ("t", Q_WIDTH, "ropeq", BF16),
    ("n", Q_WIDTH, "rope", BF16),
    ("t", Q_WIDTH, "plain", BF16),
    ("t", IDX_HEADS * IDX_DIM, "rope", BF16),
    ("n", 128, "rope", BF16),
    ("n", 1024, "sig", F32),
    ("n", 1024, "sig", F32),
    ("t", SMALL_W, "small", F32),
)
_PROJ_NW = sum(s[1] for s in _PROJ_SEGS if s[0] == "n")
_PROJ_TW = sum(s[1] for s in _PROJ_SEGS if s[0] == "t")
_PROJ_CHUNK = 512


def _rope_n(y, c, sa, sb):
    pieces = []
    for j in range(y.shape[1] // LANE):
        yj = y[:, j * LANE:(j + 1) * LANE]
        pieces.append(yj * c + pltpu.roll(yj, ROPE_HALF, 1) * sa + pltpu.roll(yj, LANE - ROPE_HALF, 1) * sb)
    return pieces[0] if len(pieces) == 1 else jnp.concatenate(pieces, axis=1)


def _rope_t(y, cos, sin):
    pieces = []
    for h in range(y.shape[0] // HEAD_DIM):
        r0 = h * HEAD_DIM
        x1, x2 = y[r0:r0 + ROPE_HALF], y[r0 + ROPE_HALF:r0 + ROPE_DIM]
        pieces += [x1 * cos - x2 * sin, x2 * cos + x1 * sin, y[r0 + ROPE_DIM:r0 + HEAD_DIM]]
    return jnp.concatenate(pieces, axis=0)


def _proj_kernel(x_ref, g_ref, wn_ref, wt_ref, c_ref, sa_ref, sb_ref, ct_ref, st_ref, *out_refs, npos):
    tm = x_ref.shape[0]
    h = _rmsnorm(x_ref[...], g_ref[...]).astype(BF16)
    c, sa, sb = c_ref[...], sa_ref[...], sb_ref[...]
    cos_t, sin_t = ct_ref[0], st_ref[0]
    ncol = trow = 0
    for (layout, width, kind, _), o_ref in zip(_PROJ_SEGS, out_refs):
        if layout == "t":
            y = _dot_nt(wt_ref[trow:trow + width, :], h)
            trow += width
            if kind in ("rope", "ropeq"):
                y = _rope_t(y, cos_t, sin_t)
            if kind == "ropeq":
                y = y * Q_SCALE
            if kind == "small":
                row = lax.broadcasted_iota(jnp.int32, y.shape, 0)
                y = jnp.where(row < N_GATE, jax.nn.sigmoid(y), y * IDX_WEIGHT_SCALE)
            o_ref[0] = y.astype(o_ref.dtype)
            continue
        for c0 in range(0, width, _PROJ_CHUNK):
            cw = min(_PROJ_CHUNK, width - c0)
            y = _dot(h, wn_ref[:, ncol + c0:ncol + c0 + cw])
            if kind in ("rope", "rope_onehot"):
                y = _rope_n(y, c, sa, sb)
            if kind == "rope_onehot":
                pos = (pl.program_id(0) % npos) * tm + lax.broadcasted_iota(jnp.int32, y.shape, 0)
                lane = lax.broadcasted_iota(jnp.int32, y.shape, 1) & (LANE - 1)
                y = jnp.where(lane == HEAD_DIM + pos // SLC_BLOCK, 1.0, y)
            if kind == "sig":
                y = jax.nn.sigmoid(y)
            o_ref[:, c0:c0 + cw] = y.astype(o_ref.dtype)
        ncol += width


def _split_w_in(w):
    d = w.shape[0]
    kvw = NSA_KV_GROUPS * HEAD_DIM
    sizes = (Q_WIDTH, kvw, kvw, kvw, kvw, kvw, kvw, N_GATE, Q_WIDTH, Q_WIDTH, Q_WIDTH,
             IDX_HEADS * IDX_DIM, IDX_DIM, IDX_HEADS, d, d)
    assert sum(sizes) == w.shape[1]
    offs = np.concatenate([[0], np.cumsum(sizes)])
    (nsa_q, k_cmp, v_cmp, k_slc, v_slc, k_win, v_win, nsa_gate, dsa_q, dsa_k, dsa_v,
     idx_q, idx_k, idx_w, gate_nsa, gate_dsa) = [w[:, offs[i]:offs[i + 1]] for i in range(16)]

    def pad_groups(a):
        a = a.reshape(d, NSA_KV_GROUPS, HEAD_DIM)
        return jnp.concatenate([a, jnp.zeros_like(a)], axis=2).reshape(d, NSA_KV_GROUPS * LANE)

    idx_k_pad = jnp.concatenate([idx_k, jnp.zeros_like(idx_k)], axis=1)
    w_n = jnp.concatenate([k_cmp, v_cmp, pad_groups(k_slc), pad_groups(k_win), dsa_k, idx_k_pad,
                           gate_nsa, gate_dsa], axis=1)
    w_t = jnp.concatenate([nsa_q, v_slc, v_win, dsa_q, dsa_v, idx_q, nsa_gate, idx_w], axis=1).T
    assert w_n.shape[1] == _PROJ_NW and w_t.shape[0] == _PROJ_TW
    return w_n.astype(BF16), w_t.astype(BF16)


def _proj(xf, g, w_n, w_t, tables, seq_len, tm):
    t, d = xf.shape
    npos = seq_len // tm
    row = lambda i: (i, 0)
    tab_n = pl.BlockSpec((tm, LANE), lambda i: (i % npos, 0))
    tab_t = pl.BlockSpec((1, ROPE_HALF, tm), lambda i: (i % npos, 0, 0))
    out_specs, out_shape = [], []
    for layout, width, _, dt in _PROJ_SEGS:
        if layout == "t":
            out_specs.append(pl.BlockSpec((1, width, tm), lambda i: (i, 0, 0)))
            out_shape.append(jax.ShapeDtypeStruct((t // tm, width, tm), dt))
        else:
            out_specs.append(pl.BlockSpec((tm, width), row))
            out_shape.append(jax.ShapeDtypeStruct((t, width), dt))
    return pl.pallas_call(
        functools.partial(_proj_kernel, npos=npos),
        grid=(t // tm,),
        in_specs=[pl.BlockSpec((tm, d), row), _const_spec((1, d)), _const_spec((d, _PROJ_NW)),
                  _const_spec((_PROJ_TW, d)), tab_n, tab_n, tab_n, tab_t, tab_t],
        out_specs=out_specs,
        out_shape=out_shape,
        compiler_params=_params("parallel"),
        name="proj",
    )(xf, g.reshape(1, d), w_n, w_t, *tables)


def _cmp_kernel(k_ref, v_ref, pk_ref, w1k_ref, b1k_ref, w2k_ref, b2k_ref,
                pv_ref, w1v_ref, b1v_ref, w2v_ref, b2v_ref, kc_ref, vc_ref, *, n_cmp):
    nb = kc_ref.shape[1]

    def hidden(x_ref, pos_ref, w1_ref, b1_ref):
        halves = []
        for half in range(CMP_BLOCK // CMP_STRIDE):
            acc = None
            for l in range(CMP_STRIDE):
                j = half * CMP_STRIDE + l
                x = x_ref[0, pl.ds(l, nb, stride=CMP_STRIDE), :] + pos_ref[j:j + 1, :]
                part = _dot(x.astype(BF16), w1_ref[j])
                acc = part if acc is None else acc + part
            halves.append(acc)
        return jax.nn.gelu(halves[0] + pltpu.roll(halves[1], nb - 1, 0) + b1_ref[...]).astype(BF16)

    kc = _dot(hidden(k_ref, pk_ref, w1k_ref, b1k_ref), w2k_ref[...]) + b2k_ref[...]
    rows = lax.broadcasted_iota(jnp.int32, kc.shape, 0)
    kc_ref[0] = jnp.where(rows < n_cmp, kc, 0.0).astype(kc_ref.dtype)
    vc = _dot_nt(w2v_ref[...], hidden(v_ref, pv_ref, w1v_ref, b1v_ref)) + b2v_ref[...]
    cols = lax.broadcasted_iota(jnp.int32, vc.shape, 1)
    vc_ref[0] = jnp.where(cols < n_cmp, vc, 0.0).astype(vc_ref.dtype)


def _compress(k_cmp, v_cmp, pk, w1k, b1k, w2k, b2k, pv, w1v, b1v, w2v, b2v, bsz, seq_len):
    g, hd = NSA_KV_GROUPS, HEAD_DIM
    assert CMP_BLOCK == 2 * CMP_STRIDE
    nb = seq_len // CMP_STRIDE
    n_cmp = (seq_len - CMP_BLOCK) // CMP_STRIDE + 1
    eye = jnp.eye(g, dtype=F32)
    per_group = lambda m: jnp.kron(eye, m)
    tile_g = lambda a: jnp.concatenate([a] * g, axis=-1)

    def first_layer(pos, w1, b1):
        w1_bd = jax.vmap(per_group)(w1.reshape(CMP_BLOCK, hd, CMP_HIDDEN)).astype(BF16)
        return tile_g(pos), w1_bd, tile_g(b1).reshape(1, -1)

    w2k_pad = jnp.concatenate([w2k, jnp.zeros_like(w2k)], axis=1)
    b2k_pad = jnp.concatenate([b2k, jnp.zeros_like(b2k)])
    k_w = first_layer(pk, w1k, b1k) + (per_group(w2k_pad).astype(BF16), tile_g(b2k_pad).reshape(1, -1))
    v_w = first_layer(pv, w1v, b1v) + (per_group(w2v.T).astype(BF16), tile_g(b2v).reshape(-1, 1))
    seq_spec = pl.BlockSpec((1, seq_len, g * hd), lambda b: (b, 0, 0))
    first = [_const_spec((CMP_BLOCK, g * hd)), _const_spec((CMP_BLOCK, g * hd, g * CMP_HIDDEN)),
             _const_spec((1, g * CMP_HIDDEN))]
    return pl.pallas_call(
        functools.partial(_cmp_kernel, n_cmp=n_cmp),
        grid=(bsz,),
        in_specs=[seq_spec, seq_spec] + first + [_const_spec((g * CMP_HIDDEN, g * LANE)), _const_spec((1, g * LANE))]
        + first + [_const_spec((g * hd, g * CMP_HIDDEN)), _const_spec((g * hd, 1))],
        out_specs=[pl.BlockSpec((1, nb, g * LANE), lambda b: (b, 0, 0)),
                   pl.BlockSpec((1, g * hd, nb), lambda b: (b, 0, 0))],
        out_shape=[jax.ShapeDtypeStruct((bsz, nb, g * LANE), BF16),
                   jax.ShapeDtypeStruct((bsz, g * hd, nb), BF16)],
        compiler_params=_params("parallel"),
        name="compress",
    )(k_cmp.reshape(bsz, seq_len, g * hd), v_cmp.reshape(bsz, seq_len, g * hd), *k_w, *v_w)


def _fold(x, op, group=SUBLANE):
    parts = [x[r * group:(r + 1) * group] for r in range(x.shape[0] // group)]
    while len(parts) > 1:
        pairs = [op(parts[k], parts[k + 1]) for k in range(0, len(parts) - 1, 2)]
        parts = pairs + ([parts[-1]] if len(parts) % 2 else [])
    return parts[0]


def _loop_pairs(n, body, init):
    carry = lax.fori_loop(0, n // 2, lambda j, c: body(2 * j + 1, body(2 * j, c)), init)
    return lax.cond(n % 2 == 1, lambda c: body(n - 1, c), lambda c: c, carry)


def _weights(s, m):
    return jnp.exp2((s - m).astype(BF16))


def _with_ones(v_t):
    return jnp.concatenate([v_t, jnp.ones((ONES_ROWS, v_t.shape[1]), v_t.dtype)], axis=0)


def _nsa_kernel(q_ref, kc_ref, vc_ref, ks_ref, vs_ref, kw_ref, vw_ref, sm_ref, ovl_ref, o_ref,
                qm_ref, ss_ref, sw_ref, as_ref, *, t, seq_len):
    r_heads = NSA_GROUP_SIZE
    gi = pl.program_id(1)
    i = pl.program_id(2)
    t0 = i * t
    rows = r_heads * t
    hd = HEAD_DIM

    for r in range(r_heads):
        qm_ref[0:hd, r * t:(r + 1) * t] = q_ref[0, 0, r * hd:(r + 1) * hd, :]
    qm_ref[hd:, :] = jnp.zeros((LANE - hd, rows), BF16)
    tok = t0 + lax.broadcasted_iota(jnp.int32, (1, t), 1)
    tok_r = _lane_tile(tok, r_heads)

    nb = kc_ref.shape[1]
    s_c = _dot(kc_ref[0], qm_ref[...])
    c_end = lax.broadcasted_iota(jnp.int32, (nb, 1), 0) * CMP_STRIDE + (CMP_BLOCK - 1)
    mask_c = c_end <= tok_r
    s_c = jnp.where(mask_c, s_c, NEG)
    s_c = s_c - jnp.max(s_c, axis=0, keepdims=True)
    p_c = jnp.where(mask_c, jnp.exp2(s_c), 0.0)
    p_c = (p_c / jnp.maximum(jnp.sum(p_c, axis=0, keepdims=True), 1e-30)).astype(BF16)
    o_c = _dot(vc_ref[0], p_c)
    imp4 = _dot(ovl_ref[...], p_c)
    imp = imp4[:, 0:t]
    for r in range(1, r_heads):
        imp = imp + imp4[:, r * t:(r + 1) * t]

    n_slc = seq_len // SLC_BLOCK
    n_sel = min(N_SLC, n_slc)
    blk = lax.broadcasted_iota(jnp.int32, (MAX_SLC_BLOCKS, 1), 0)
    dist = (tok // SLC_BLOCK) - blk
    forced = (blk == 0) | ((dist >= 0) & (dist < N_LOCAL_BLOCKS))
    score = jnp.where(forced, SEL_FORCE_SCORE, imp)
    score = jnp.where(blk * SLC_BLOCK <= tok, score, -jnp.inf)
    groups = [score[v * SUBLANE:(v + 1) * SUBLANE] for v in range(MAX_SLC_BLOCKS // SUBLANE)]
    ranks = [jnp.zeros((SUBLANE, t), F32) for _ in groups]
    sub = lax.broadcasted_iota(jnp.int32, (SUBLANE, 1), 0)
    for j in range(n_slc):
        row = score[j:j + 1]
        for v, sv in enumerate(groups):
            ge = jnp.where(row >= sv, 1.0, 0.0)
            gt = jnp.where(row > sv, 1.0, 0.0)
            if v * SUBLANE > j:
                beats = ge
            elif (v + 1) * SUBLANE <= j + 1:
                beats = gt
            else:
                beats = jnp.where(sub + v * SUBLANE > j, ge, gt)
            ranks[v] = ranks[v] + beats
    selbias = jnp.where(jnp.concatenate(ranks, axis=0) < n_sel, 0.0, NEG).astype(BF16)
    for r in range(r_heads):
        qm_ref[hd:hd + MAX_SLC_BLOCKS, r * t:(r + 1) * t] = selbias
    key_off = lax.broadcasted_iota(jnp.int32, (t, 1), 0)
    q_off = lax.broadcasted_iota(jnp.int32, (1, t), 1)
    causal = _lane_tile(jnp.where(key_off <= q_off, 0.0, NEG), r_heads)
    neg8 = jnp.full((SUBLANE, rows), NEG, F32)

    def key_tile(k_ref_, c):
        return k_ref_[0, pl.ds(pl.multiple_of(c * t, t), t), :]

    def slc_scores(c, mx):
        s = _dot(key_tile(ks_ref, c), qm_ref[...])
        ss_ref[c] = s
        return jnp.maximum(mx, _fold(s, jnp.maximum))

    mx = _loop_pairs(i, slc_scores, neg8)
    s = _dot(key_tile(ks_ref, i), qm_ref[...]) + causal
    ss_ref[i] = s
    m_s = jnp.max(jnp.maximum(mx, _fold(s, jnp.maximum)), axis=0, keepdims=True)

    n_back = -(-WINDOW // t)
    win_tiles = []
    mxw = neg8
    for back in range(n_back, -1, -1):
        if back:
            lo_edge = back * t - WINDOW
            bias = _lane_tile(jnp.where(key_off - q_off > jnp.where(i >= back, lo_edge, t), 0.0, NEG), r_heads)
        else:
            bias = causal
        c = jnp.maximum(i - back, 0)
        s = _dot(key_tile(kw_ref, c), qm_ref[...]) + bias
        sw_ref[n_back - back] = s
        mxw = jnp.maximum(mxw, _fold(s, jnp.maximum))
        win_tiles.append(c)
    m_w = jnp.max(mxw, axis=0, keepdims=True)

    as_ref[...] = jnp.zeros(as_ref.shape, F32)

    def slc_values(c, carry):
        as_ref[...] += _dot(_with_ones(vs_ref[0, c]), _weights(ss_ref[c], m_s))
        return carry

    _loop_pairs(i + 1, slc_values, 0)
    o_s = as_ref[:hd] * (1.0 / as_ref[hd:hd + 1])

    o_w = jnp.zeros(as_ref.shape, F32)
    for n, c in enumerate(win_tiles):
        o_w = o_w + _dot(_with_ones(vw_ref[0, c]), _weights(sw_ref[n], m_w))
    o_w = o_w[:hd] * (1.0 / o_w[hd:hd + 1])

    gates = sm_ref[0, 0]

    def gate(branch, r):
        rows_g = [gates[branch * Q_HEADS + g * r_heads + r:branch * Q_HEADS + g * r_heads + r + 1]
                  for g in range(NSA_KV_GROUPS)]
        out = rows_g[0]
        for g in range(1, NSA_KV_GROUPS):
            out = jnp.where(gi == g, rows_g[g], out)
        return out

    heads = []
    for r in range(r_heads):
        sl = slice(r * t, (r + 1) * t)
        heads.append(gate(0, r) * o_c[:, sl] + gate(1, r) * o_s[:, sl] + gate(2, r) * o_w[:, sl])
    o_ref[0] = jnp.concatenate(heads, axis=0).T.astype(o_ref.dtype)


def _overlap_t(seq_len):
    nb = seq_len // CMP_STRIDE
    n_cmp = (seq_len - CMP_BLOCK) // CMP_STRIDE + 1
    n_slc = seq_len // SLC_BLOCK
    c_start = np.arange(n_cmp)[None, :] * CMP_STRIDE
    s_start = np.arange(n_slc)[:, None] * SLC_BLOCK
    ov = np.minimum(c_start + CMP_BLOCK, s_start + SLC_BLOCK) - np.maximum(c_start, s_start)
    ovl = np.zeros((MAX_SLC_BLOCKS, nb), np.float32)
    ovl[:n_slc, :n_cmp] = np.clip(ov, 0, None) / CMP_BLOCK
    return jnp.asarray(ovl, BF16)


def _nsa(q_t, kc, vc_t, ks, vs_t, kw, vw_t, small_t, bsz, seq_len, t):
    g, r, hd = NSA_KV_GROUPS, NSA_GROUP_SIZE, HEAD_DIM
    assert seq_len // SLC_BLOCK <= MAX_SLC_BLOCKS
    nb = seq_len // CMP_STRIDE
    nc = seq_len // t
    rows = r * t
    ovl = _overlap_t(seq_len)
    k_spec = pl.BlockSpec((1, seq_len, LANE), lambda b, gi, i: (b, 0, gi))
    v_spec = pl.BlockSpec((1, nc, hd, t), lambda b, gi, i: (b, 0, gi, 0))
    return pl.pallas_call(
        functools.partial(_nsa_kernel, t=t, seq_len=seq_len),
        grid=(bsz, g, nc),
        in_specs=[pl.BlockSpec((1, 1, r * hd, t), lambda b, gi, i: (b, i, gi, 0)),
                  pl.BlockSpec((1, nb, LANE), lambda b, gi, i: (b, 0, gi)),
                  pl.BlockSpec((1, hd, nb), lambda b, gi, i: (b, gi, 0)),
                  k_spec, v_spec, k_spec, v_spec,
                  pl.BlockSpec((1, 1, SMALL_W, t), lambda b, gi, i: (b, i, 0, 0)),
                  _const_spec(ovl.shape)],
        out_specs=pl.BlockSpec((1, t, r * hd), lambda b, gi, i: (b, i, gi)),
        out_shape=jax.ShapeDtypeStruct((bsz, seq_len, Q_WIDTH), BF16),
        scratch_shapes=[pltpu.VMEM((LANE, rows), BF16), pltpu.VMEM((nc, t, rows), F32),
                        pltpu.VMEM((-(-WINDOW // t) + 1, t, rows), F32),
                        pltpu.VMEM((hd + ONES_ROWS, rows), F32)],
        compiler_params=_params("parallel", "parallel", "parallel"),
        name="nsa_attn",
    )(q_t.reshape(bsz, nc, Q_WIDTH, t), kc, vc_t, ks.reshape(bsz, seq_len, -1), vs_t.reshape(bsz, nc, g * hd, t),
      kw.reshape(bsz, seq_len, -1), vw_t.reshape(bsz, nc, g * hd, t), small_t.reshape(bsz, nc, SMALL_W, t), ovl)


def _dsa_kernel(q_ref, k_ref, v_ref, iq_ref, ik_ref, sm_ref, o_ref,
                qm_ref, iqm_ref, sc_ref, sc16_ref, s_ref, acc_ref, kth_ref, *, t, seq_len, k_top):
    i = pl.program_id(1)
    t0 = i * t
    nck = i + 1
    hd = HEAD_DIM
    tok = t0 + lax.broadcasted_iota(jnp.int32, (1, t), 1)

    zeros = jnp.zeros((hd, t), BF16)
    for h in range(Q_HEADS):
        qh = q_ref[0, 0, h * hd:(h + 1) * hd, :]
        qm_ref[h] = jnp.concatenate([qh, zeros] if h % 2 == 0 else [zeros, qh], axis=0)
        iqm_ref[h] = jnp.concatenate([iq_ref[0, 0, h * hd:(h + 1) * hd, :], zeros], axis=0)
    small = sm_ref[0, 0]

    def key_pos(c):
        start = c * t if isinstance(c, int) else pl.multiple_of(c * t, t)
        return start + lax.broadcasted_iota(jnp.int32, (t, 1), 0)

    def score_body(c, carry):
        ik = ik_ref[0, pl.ds(pl.multiple_of(c * t, t), t), :]
        sc = jnp.zeros((t, t), F32)
        for h in range(IDX_HEADS):
            sc = sc + jnp.maximum(_dot(ik, iqm_ref[h]), 0.0) * small[N_GATE + h:N_GATE + h + 1]
        sc = jnp.where(key_pos(c) <= tok, sc, -jnp.inf)
        sc_ref[c] = sc
        sc16_ref[c] = sc.astype(BF16)
        return carry

    _loop_pairs(nck, score_body, 0)

    def count(indicator, n_tiles=None):
        def body(c, acc):
            return acc + _fold(indicator(sc_ref[c], key_pos(c)), jnp.add)
        acc = jnp.zeros((SUBLANE, t), F32)
        if n_tiles is None:
            acc = _loop_pairs(nck, body, acc)
        else:
            for c in range(n_tiles):
                acc = body(c, acc)
        return jnp.sum(acc, axis=0, keepdims=True)

    kf = float(k_top)
    zero = jnp.zeros((1, t), jnp.int32)

    def decode(code):
        return lax.bitcast_convert_type(jnp.where(code < 0, code ^ jnp.int32(0x7FFFFFFF), code), F32)

    def search(n_tiles):
        one16, zero16 = jnp.ones((), BF16), jnp.zeros((), BF16)

        def decode16(code16):
            pattern = jnp.where(code16 < 0, code16 ^ jnp.int32(0x7FFF), code16)
            return lax.bitcast_convert_type(lax.shift_left(pattern, 16), F32)

        def reached16(code16):
            val = decode16(code16).astype(BF16)
            acc = jnp.zeros((BF16_ROWS, t), BF16)
            for c in range(n_tiles):
                acc = acc + _fold(jnp.where(sc16_ref[c] >= val, one16, zero16), jnp.add, BF16_ROWS)
            return (jnp.sum(acc.astype(F32), axis=0, keepdims=True) >= kf) | (code16 <= KEY16_NEG_INF)

        def bit16_body(b, base16):
            cand = base16 + lax.shift_left(jnp.int32(1), 14 - b)
            return jnp.where(reached16(cand), cand, base16)

        base16 = jnp.where(reached16(zero), zero, jnp.full((1, t), -(2 ** 15), jnp.int32))
        below = lax.fori_loop(0, 15, bit16_body, base16) - 1
        lo = lax.shift_left(below, 16) + jnp.where(below < 0, 0xFFFF, 0)

        def reached(code):
            val = decode(code)
            return (count(lambda sc, pos: jnp.where(sc >= val, 1.0, 0.0), n_tiles) >= kf) | (code <= KEY_NEG_INF)

        def bit_body(b, base):
            cand = base + lax.shift_left(jnp.int32(1), 16 - b)
            return jnp.where(reached(cand), cand, base)

        kth = decode(jnp.maximum(lax.fori_loop(0, 17, bit_body, lo), KEY_NEG_INF))
        return (kth, count(lambda sc, pos: jnp.where(sc >= kth, 1.0, 0.0), n_tiles),
                count(lambda sc, pos: jnp.where(sc > kth, 1.0, 0.0), n_tiles))

    for n_tiles in range(1, seq_len // t + 1):
        @pl.when(nck == n_tiles)
        def _(n_tiles=n_tiles):
            if n_tiles * t <= k_top:
                found = (jnp.full((1, t), -jnp.inf, F32), jnp.full((1, t), kf, F32), jnp.zeros((1, t), F32))
            else:
                found = search(n_tiles)
            for r, v in enumerate(found):
                kth_ref[r:r + 1, :] = v

    kth = kth_ref[0:1, :]
    n_ge, n_gt = kth_ref[1:2, :], kth_ref[2:3, :]

    tie = (n_ge > kf) & (kth > -jnp.inf)

    def tie_cut():
        room = kf - n_gt
        top_bit = seq_len.bit_length() - 1

        def pos_body(b, x):
            cand = x + lax.shift_left(jnp.int32(1), top_bit - b)
            n = count(lambda sc, pos: jnp.where(sc == kth, jnp.where(pos < cand, 1.0, 0.0), 0.0))
            return jnp.where(n <= room, cand, x)

        x = lax.fori_loop(0, top_bit + 1, pos_body, zero)
        return jnp.where(tie, x, seq_len)

    cut = lax.cond(jnp.max(jnp.where(tie, 1.0, 0.0)) > 0.0, tie_cut,
                   lambda: jnp.full((1, t), seq_len, jnp.int32))
    cut = jnp.where(kth > -jnp.inf, cut, 0)

    def att_scores(c, mx):
        k0 = pl.multiple_of(c * t, t)
        sc = sc_ref[c]
        bias = jnp.where(sc > kth, 0.0, jnp.where(sc == kth, jnp.where(key_pos(c) < cut, 0.0, NEG), NEG))
        out = []
        for h in range(Q_HEADS):
            s = _dot(k_ref[0, pl.ds(k0, t), (h // 2) * LANE:(h // 2 + 1) * LANE], qm_ref[h]) + bias
            s_ref[h, c] = s
            out.append(jnp.maximum(mx[h], _fold(s, jnp.maximum)))
        return tuple(out)

    mx = _loop_pairs(nck, att_scores, (jnp.full((SUBLANE, t), NEG, F32),) * Q_HEADS)
    m = [jnp.max(mx[h], axis=0, keepdims=True) for h in range(Q_HEADS)]

    acc_ref[...] = jnp.zeros(acc_ref.shape, F32)

    def att_values(c, carry):
        for h in range(Q_HEADS):
            acc_ref[h] += _dot(_with_ones(v_ref[0, c, h * hd:(h + 1) * hd, :]), _weights(s_ref[h, c], m[h]))
        return carry

    _loop_pairs(nck, att_values, 0)
    out = jnp.concatenate([acc_ref[h, :hd] * (1.0 / acc_ref[h, hd:hd + 1])
                           for h in range(Q_HEADS)], axis=0)
    o_ref[0] = out.T.astype(o_ref.dtype)


def _dsa(q_t, k, v_t, iq_t, ik, small_t, bsz, seq_len, t):
    k_top = min(DSA_TOPK_MAX, seq_len // 4)
    assert t >= k_top
    nc = seq_len // t
    qt_spec = pl.BlockSpec((1, 1, Q_WIDTH, t), lambda b, i: (b, i, 0, 0))
    return pl.pallas_call(
        functools.partial(_dsa_kernel, t=t, seq_len=seq_len, k_top=k_top),
        grid=(bsz, nc),
        in_specs=[qt_spec,
                  pl.BlockSpec((1, seq_len, Q_WIDTH), lambda b, i: (b, 0, 0)),
                  pl.BlockSpec((1, nc, Q_WIDTH, t), lambda b, i: (b, 0, 0, 0)),
                  qt_spec,
                  pl.BlockSpec((1, seq_len, LANE), lambda b, i: (b, 0, 0)),
                  pl.BlockSpec((1, 1, SMALL_W, t), lambda b, i: (b, i, 0, 0))],
        out_specs=pl.BlockSpec((1, t, Q_WIDTH), lambda b, i: (b, i, 0)),
        out_shape=jax.ShapeDtypeStruct((bsz, seq_len, Q_WIDTH), BF16),
        scratch_shapes=[pltpu.VMEM((Q_HEADS, LANE, t), BF16), pltpu.VMEM((IDX_HEADS, LANE, t), BF16),
                        pltpu.VMEM((nc, t, t), F32), pltpu.VMEM((nc, t, t), BF16),
                        pltpu.VMEM((Q_HEADS, nc, t, t), F32),
                        pltpu.VMEM((Q_HEADS, HEAD_DIM + ONES_ROWS, t), F32), pltpu.VMEM((SUBLANE, t), F32)],
        compiler_params=_params("parallel", "parallel"),
        name="dsa_attn",
    )(q_t.reshape(bsz, nc, Q_WIDTH, t), k.reshape(bsz, seq_len, -1), v_t.reshape(bsz, nc, Q_WIDTH, t),
      iq_t.reshape(bsz, nc, Q_WIDTH, t), ik.reshape(bsz, seq_len, -1), small_t.reshape(bsz, nc, SMALL_W, t))


def _merge_ffn_kernel(x_ref, yn_ref, yd_ref, gn_ref, gd_ref, wbn_ref, wbd_ref, wo_ref,
                      g_ref, wg_ref, wu_ref, wd_ref, fg_ref, o_ref, *, final):
    merged = gn_ref[...] * _dot(yn_ref[...], wbn_ref[...]) + gd_ref[...] * _dot(yd_ref[...], wbd_ref[...])
    x = x_ref[...] + _dot(merged.astype(BF16), wo_ref[...])
    h = _rmsnorm(x, g_ref[...]).astype(BF16)
    z = (jax.nn.silu(_dot(h, wg_ref[...])) * _dot(h, wu_ref[...])).astype(BF16)
    y = x + _dot(z, wd_ref[...])
    o_ref[...] = _rmsnorm(y, fg_ref[...]) if final else y


def _merge_ffn(xf, y_nsa, y_dsa, gate_nsa, gate_dsa, wbn, wbd, wo, g, wg, wu, wd, final_g, final, tm):
    t, d = xf.shape
    f = wg.shape[1]
    wn = y_nsa.shape[1]
    row = lambda i: (i, 0)
    return pl.pallas_call(
        functools.partial(_merge_ffn_kernel, final=final),
        grid=(t // tm,),
        in_specs=[pl.BlockSpec((tm, d), row), pl.BlockSpec((tm, wn), row), pl.BlockSpec((tm, wn), row),
                  pl.BlockSpec((tm, d), row), pl.BlockSpec((tm, d), row),
                  _const_spec((wn, d)), _const_spec((wn, d)), _const_spec((d, d)),
                  _const_spec((1, d)), _const_spec((d, f)), _const_spec((d, f)), _const_spec((f, d)),
                  _const_spec((1, d))],
        out_specs=pl.BlockSpec((tm, d), row),
        out_shape=jax.ShapeDtypeStruct((t, d), F32),
        compiler_params=_params("parallel"),
        name="merge_ffn",
    )(xf, y_nsa, y_dsa, gate_nsa, gate_dsa, wbn.astype(BF16), wbd.astype(BF16), wo.astype(BF16),
      g.reshape(1, d), wg.astype(BF16), wu.astype(BF16), wd.astype(BF16), final_g.reshape(1, d))


def _rope_tables(seq_len, tm):
    pos = jnp.arange(seq_len, dtype=F32)
    inv_freq = ROPE_THETA ** (-jnp.arange(0, ROPE_DIM, 2, dtype=F32) / ROPE_DIM)
    ang = pos[:, None] * inv_freq[None, :]
    cos, sin = jnp.cos(ang), jnp.sin(ang)
    rest = HEAD_DIM - ROPE_DIM
    one, zero = jnp.ones((seq_len, rest), F32), jnp.zeros((seq_len, rest), F32)
    zh = jnp.zeros((seq_len, ROPE_HALF), F32)
    two = lambda a: jnp.concatenate([a, a], axis=1)
    tile_t = lambda a: a.reshape(seq_len // tm, tm, ROPE_HALF).transpose(0, 2, 1)
    return (two(jnp.concatenate([cos, cos, one], axis=1)),
            two(jnp.concatenate([zh, sin, zero], axis=1)),
            two(jnp.concatenate([-sin, zh, zero], axis=1)),
            tile_t(cos), tile_t(sin))


def _tile(seq_len):
    return min(256, seq_len)


def kernel(x, attn_norm, w_in, cmpk_pos, cmpk_w1, cmpk_b1, cmpk_w2, cmpk_b2, cmpv_pos, cmpv_w1, cmpv_b1,
           cmpv_w2, cmpv_b2, w_branch_nsa, w_branch_dsa, w_out, ffn_norm, w_ffn_gate, w_ffn_up, w_ffn_down,
           final_norm):
    bsz, seq_len, d = x.shape
    depth = w_in.shape[0]
    t = _tile(seq_len)
    tables = _rope_tables(seq_len, t)
    xf = x.reshape(bsz * seq_len, d)
    for l in range(depth):
        (q_nsa, k_cmp, v_cmp, k_slc, v_slc, k_win, v_win, q_dsa, k_dsa, v_dsa, q_idx, k_idx,
         gate_nsa, gate_dsa, small) = _proj(xf, attn_norm[l], *_split_w_in(w_in[l]), tables, seq_len, t)
        kc, vc = _compress(k_cmp, v_cmp, cmpk_pos[l], cmpk_w1[l], cmpk_b1[l], cmpk_w2[l], cmpk_b2[l],
                           cmpv_pos[l], cmpv_w1[l], cmpv_b1[l], cmpv_w2[l], cmpv_b2[l], bsz, seq_len)
        y_nsa = _nsa(q_nsa, kc, vc, k_slc, v_slc, k_win, v_win, small, bsz, seq_len, t)
        y_dsa = _dsa(q_dsa, k_dsa, v_dsa, q_idx, k_idx, small, bsz, seq_len, t)
        xf = _merge_ffn(xf, y_nsa.reshape(bsz * seq_len, -1), y_dsa.reshape(bsz * seq_len, -1),
                        gate_nsa, gate_dsa, w_branch_nsa[l], w_branch_dsa[l], w_out[l],
                        ffn_norm[l], w_ffn_gate[l], w_ffn_up[l], w_ffn_down[l], final_norm,
                        final=(l == depth - 1), tm=t)
    return xf.reshape(bsz, seq_len, d)
```

```python
import functools

import numpy as np
import jax
import jax.numpy as jnp
from jax import lax
from jax.experimental import pallas as pl
from jax.experimental.pallas import tpu as pltpu

HEAD_DIM = 64
ROPE_DIM = HEAD_DIM // 4
ROPE_HALF = ROPE_DIM // 2
ROPE_THETA = 500000.0
NORM_EPS = 1e-6
ATTN_SCALE = HEAD_DIM ** -0.5
Q_SCALE = ATTN_SCALE * float(np.log2(np.e))
Q_HEADS = 8
Q_WIDTH = Q_HEADS * HEAD_DIM
NSA_KV_GROUPS = 2
NSA_GROUP_SIZE = Q_HEADS // NSA_KV_GROUPS
CMP_BLOCK = 32
CMP_STRIDE = 16
CMP_HIDDEN = 2 * HEAD_DIM
SLC_BLOCK = 64
N_SLC = 16
N_LOCAL_BLOCKS = 2
SEL_FORCE_SCORE = 1e4
WINDOW = 512
IDX_HEADS = 8
IDX_DIM = 64
IDX_WEIGHT_SCALE = (IDX_HEADS * IDX_DIM) ** -0.5
DSA_TOPK_MAX = 256
N_GATE = 3 * Q_HEADS
SMALL_W = N_GATE + IDX_HEADS
MAX_SLC_BLOCKS = 32

LANE = 128
SUBLANE = 8
ONES_ROWS = 16
V7X_VMEM_LIMIT = 56 * 1024 * 1024

NEG = -1e30
INT_MIN = -(2 ** 31)
KEY_NEG_INF = -2139095041
KEY16_NEG_INF = -32641
BF16_ROWS = 16
DSA_TILES_PER_TRIP = 4
BF16 = jnp.bfloat16
F32 = jnp.float32

_NT = (((1,), (1,)), ((), ()))


def _dot(a, b):
    return jnp.dot(a, b, preferred_element_type=F32)


def _dot_nt(a, b):
    return lax.dot_general(a, b, _NT, preferred_element_type=F32)


def _const_spec(shape):
    nd = len(shape)
    return pl.BlockSpec(shape, lambda *_: (0,) * nd, pipeline_mode=pl.Buffered(1))


def _rmsnorm(x, g):
    ms = jnp.mean(x * x, axis=-1, keepdims=True)
    return x * lax.rsqrt(ms + NORM_EPS) * g


def _lane_tile(a, reps):
    return a if reps == 1 else jnp.concatenate([a] * reps, axis=1)


def _params(*sem):
    return pltpu.CompilerParams(dimension_semantics=sem, vmem_limit_bytes=V7X_VMEM_LIMIT)


_PROJ_SEGS = (
    ("t", Q_WIDTH, "ropeq", BF16),
    ("n", 128, "rope", F32),
    ("n", 128, "plain", F32),
    ("n", 256, "rope_onehot", BF16),
    ("t", 128, "plain", BF16),
    ("n", 256, "rope", BF16),
    ("t", 128, "plain", BF16),
    ("t", Q_WIDTH, "ropeq", BF16),
    ("n", Q_WIDTH, "rope", BF16),
    ("t", Q_WIDTH, "plain", BF16),
    ("t", IDX_HEADS * IDX_DIM, "rope", BF16),
    ("n", 128, "rope", BF16),
    ("n", 1024, "sig", F32),
    ("n", 1024, "sig", F32),
    ("t", SMALL_W, "small", F32),
)
_PROJ_NW = sum(s[1] for s in _PROJ_SEGS if s[0] == "n")
_PROJ_TW = sum(s[1] for s in _PROJ_SEGS if s[0] == "t")
_PROJ_CHUNK = 512


def _rope_n(y, c, sa, sb):
    pieces = []
    for j in range(y.shape[1] // LANE):
        yj = y[:, j * LANE:(j + 1) * LANE]
        pieces.append(yj * c + pltpu.roll(yj, ROPE_HALF, 1) * sa + pltpu.roll(yj, LANE - ROPE_HALF, 1) * sb)
    return pieces[0] if len(pieces) == 1 else jnp.concatenate(pieces, axis=1)


def _rope_t(y, cos, sin):
    pieces = []
    for h in range(y.shape[0] // HEAD_DIM):
        r0 = h * HEAD_DIM
        x1, x2 = y[r0:r0 + ROPE_HALF], y[r0 + ROPE_HALF:r0 + ROPE_DIM]
        pieces += [x1 * cos - x2 * sin, x2 * cos + x1 * sin, y[r0 + ROPE_DIM:r0 + HEAD_DIM]]
    return jnp.concatenate(pieces, axis=0)


def _proj_kernel(x_ref, g_ref, wn_ref, wt_ref, c_ref, sa_ref, sb_ref, ct_ref, st_ref, *out_refs, npos):
    tm = x_ref.shape[0]
    h = _rmsnorm(x_ref[...], g_ref[...]).astype(BF16)
    c, sa, sb = c_ref[...], sa_ref[...], sb_ref[...]
    cos_t, sin_t = ct_ref[0], st_ref[0]
    ncol = trow = 0
    for (layout, width, kind, _), o_ref in zip(_PROJ_SEGS, out_refs):
        if layout == "t":
            y = _dot_nt(wt_ref[trow:trow + width, :], h)
            trow += width
            if kind in ("rope", "ropeq"):
                y = _rope_t(y, cos_t, sin_t)
            if kind == "ropeq":
                y = y * Q_SCALE
            if kind == "small":
                row = lax.broadcasted_iota(jnp.int32, y.shape, 0)
                y = jnp.where(row < N_GATE, jax.nn.sigmoid(y), y * IDX_WEIGHT_SCALE)
            o_ref[0] = y.astype(o_ref.dtype)
            continue
        for c0 in range(0, width, _PROJ_CHUNK):
            cw = min(_PROJ_CHUNK, width - c0)
            y = _dot(h, wn_ref[:, ncol + c0:ncol + c0 + cw])
            if kind in ("rope", "rope_onehot"):
                y = _rope_n(y, c, sa, sb)
            if kind == "rope_onehot":
                pos = (pl.program_id(0) % npos) * tm + lax.broadcasted_iota(jnp.int32, y.shape, 0)
                lane = lax.broadcasted_iota(jnp.int32, y.shape, 1) & (LANE - 1)
                y = jnp.where(lane == HEAD_DIM + pos // SLC_BLOCK, 1.0, y)
            if kind == "sig":
                y = jax.nn.sigmoid(y)
            o_ref[:, c0:c0 + cw] = y.astype(o_ref.dtype)
        ncol += width


def _split_w_in(w):
    d = w.shape[0]
    kvw = NSA_KV_GROUPS * HEAD_DIM
    sizes = (Q_WIDTH, kvw, kvw, kvw, kvw, kvw, kvw, N_GATE, Q_WIDTH, Q_WIDTH, Q_WIDTH,
             IDX_HEADS * IDX_DIM, IDX_DIM, IDX_HEADS, d, d)
    assert sum(sizes) == w.shape[1]
    offs = np.concatenate([[0], np.cumsum(sizes)])
    (nsa_q, k_cmp, v_cmp, k_slc, v_slc, k_win, v_win, nsa_gate, dsa_q, dsa_k, dsa_v,
     idx_q, idx_k, idx_w, gate_nsa, gate_dsa) = [w[:, offs[i]:offs[i + 1]] for i in range(16)]

    def pad_groups(a):
        a = a.reshape(d, NSA_KV_GROUPS, HEAD_DIM)
        return jnp.concatenate([a, jnp.zeros_like(a)], axis=2).reshape(d, NSA_KV_GROUPS * LANE)

    idx_k_pad = jnp.concatenate([idx_k, jnp.zeros_like(idx_k)], axis=1)
    w_n = jnp.concatenate([k_cmp, v_cmp, pad_groups(k_slc), pad_groups(k_win), dsa_k, idx_k_pad,
                           gate_nsa, gate_dsa], axis=1)
    w_t = jnp.concatenate([nsa_q, v_slc, v_win, dsa_q, dsa_v, idx_q, nsa_gate, idx_w], axis=1).T
    assert w_n.shape[1] == _PROJ_NW and w_t.shape[0] == _PROJ_TW
    return w_n.astype(BF16), w_t.astype(BF16)


def _proj(xf, g, w_n, w_t, tables, seq_len, tm):
    t, d = xf.shape
    npos = seq_len // tm
    row = lambda i: (i, 0)
    tab_n = pl.BlockSpec((tm, LANE), lambda i: (i % npos, 0))
    tab_t = pl.BlockSpec((1, ROPE_HALF, tm), lambda i: (i % npos, 0, 0))
    out_specs, out_shape = [], []
    for layout, width, _, dt in _PROJ_SEGS:
        if layout == "t":
            out_specs.append(pl.BlockSpec((1, width, tm), lambda i: (i, 0, 0)))
            out_shape.append(jax.ShapeDtypeStruct((t // tm, width, tm), dt))
        else:
            out_specs.append(pl.BlockSpec((tm, width), row))
            out_shape.append(jax.ShapeDtypeStruct((t, width), dt))
    return pl.pallas_call(
        functools.partial(_proj_kernel, npos=npos),
        grid=(t // tm,),
        in_specs=[pl.BlockSpec((tm, d), row), _const_spec((1, d)), _const_spec((d, _PROJ_NW)),
                  _const_spec((_PROJ_TW, d)), tab_n, tab_n, tab_n, tab_t, tab_t],
        out_specs=out_specs,
        out_shape=out_shape,
        compiler_params=_params("parallel"),
        name="proj",
    )(xf, g.reshape(1, d), w_n, w_t, *tables)


def _cmp_kernel(k_ref, v_ref, pk_ref, w1k_ref, b1k_ref, w2k_ref, b2k_ref,
                pv_ref, w1v_ref, b1v_ref, w2v_ref, b2v_ref, kc_ref, vc_ref, *, n_cmp):
    nb = kc_ref.shape[1]

    def hidden(x_ref, pos_ref, w1_ref, b1_ref):
        halves = []
        for half in range(CMP_BLOCK // CMP_STRIDE):
            acc = None
            for l in range(CMP_STRIDE):
                j = half * CMP_STRIDE + l
                x = x_ref[0, pl.ds(l, nb, stride=CMP_STRIDE), :] + pos_ref[j:j + 1, :]
                part = _dot(x.astype(BF16), w1_ref[j])
                acc = part if acc is None else acc + part
            halves.append(acc)
        return jax.nn.gelu(halves[0] + pltpu.roll(halves[1], nb - 1, 0) + b1_ref[...]).astype(BF16)

    kc = _dot(hidden(k_ref, pk_ref, w1k_ref, b1k_ref), w2k_ref[...]) + b2k_ref[...]
    rows = lax.broadcasted_iota(jnp.int32, kc.shape, 0)
    kc_ref[0] = jnp.where(rows < n_cmp, kc, 0.0).astype(kc_ref.dtype)
    vc = _dot_nt(w2v_ref[...], hidden(v_ref, pv_ref, w1v_ref, b1v_ref)) + b2v_ref[...]
    cols = lax.broadcasted_iota(jnp.int32, vc.shape, 1)
    vc_ref[0] = jnp.where(cols < n_cmp, vc, 0.0).astype(vc_ref.dtype)


def _compress(k_cmp, v_cmp, pk, w1k, b1k, w2k, b2k, pv, w1v, b1v, w2v, b2v, bsz, seq_len):
    g, hd = NSA_KV_GROUPS, HEAD_DIM
    assert CMP_BLOCK == 2 * CMP_STRIDE
    nb = seq_len // CMP_STRIDE
    n_cmp = (seq_len - CMP_BLOCK) // CMP_STRIDE + 1
    eye = jnp.eye(g, dtype=F32)
    per_group = lambda m: jnp.kron(eye, m)
    tile_g = lambda a: jnp.concatenate([a] * g, axis=-1)

    def first_layer(pos, w1, b1):
        w1_bd = jax.vmap(per_group)(w1.reshape(CMP_BLOCK, hd, CMP_HIDDEN)).astype(BF16)
        return tile_g(pos), w1_bd, tile_g(b1).reshape(1, -1)

    w2k_pad = jnp.concatenate([w2k, jnp.zeros_like(w2k)], axis=1)
    b2k_pad = jnp.concatenate([b2k, jnp.zeros_like(b2k)])
    k_w = first_layer(pk, w1k, b1k) + (per_group(w2k_pad).astype(BF16), tile_g(b2k_pad).reshape(1, -1))
    v_w = first_layer(pv, w1v, b1v) + (per_group(w2v.T).astype(BF16), tile_g(b2v).reshape(-1, 1))
    seq_spec = pl.BlockSpec((1, seq_len, g * hd), lambda b: (b, 0, 0))
    first = [_const_spec((CMP_BLOCK, g * hd)), _const_spec((CMP_BLOCK, g * hd, g * CMP_HIDDEN)),
             _const_spec((1, g * CMP_HIDDEN))]
    return pl.pallas_call(
        functools.partial(_cmp_kernel, n_cmp=n_cmp),
        grid=(bsz,),
        in_specs=[seq_spec, seq_spec] + first + [_const_spec((g * CMP_HIDDEN, g * LANE)), _const_spec((1, g * LANE))]
        + first + [_const_spec((g * hd, g * CMP_HIDDEN)), _const_spec((g * hd, 1))],
        out_specs=[pl.BlockSpec((1, nb, g * LANE), lambda b: (b, 0, 0)),
                   pl.BlockSpec((1, g * hd, nb), lambda b: (b, 0, 0))],
        out_shape=[jax.ShapeDtypeStruct((bsz, nb, g * LANE), BF16),
                   jax.ShapeDtypeStruct((bsz, g * hd, nb), BF16)],
        compiler_params=_params("parallel"),
        name="compress",
    )(k_cmp.reshape(bsz, seq_len, g * hd), v_cmp.reshape(bsz, seq_len, g * hd), *k_w, *v_w)


def _fold(x, op, group=SUBLANE):
    parts = [x[r * group:(r + 1) * group] for r in range(x.shape[0] // group)]
    while len(parts) > 1:
        pairs = [op(parts[k], parts[k + 1]) for k in range(0, len(parts) - 1, 2)]
        parts = pairs + ([parts[-1]] if len(parts) % 2 else [])
    return parts[0]


def _loop_pairs(n, body, init, per_trip=2):
    def steps(start, count, carry):
        for s in range(count):
            carry = body(start + s, carry)
        return carry

    carry = lax.fori_loop(0, n // per_trip, lambda j, c: steps(j * per_trip, per_trip, c), init)
    group = per_trip // 2
    while group:
        carry = lax.cond(n & group != 0, functools.partial(steps, n - n % (2 * group), group), lambda c: c, carry)
        group //= 2
    return carry


def _weights(s, m):
    return jnp.exp2((s - m).astype(BF16))


def _with_ones(v_t):
    return jnp.concatenate([v_t, jnp.ones((ONES_ROWS, v_t.shape[1]), v_t.dtype)], axis=0)


def _nsa_kernel(q_ref, kc_ref, vc_ref, ks_ref, vs_ref, kw_ref, vw_ref, sm_ref, ovl_ref, o_ref,
                qm_ref, ss_ref, sw_ref, as_ref, *, t, seq_len):
    r_heads = NSA_GROUP_SIZE
    gi = pl.program_id(1)
    i = pl.program_id(2)
    t0 = i * t
    rows = r_heads * t
    hd = HEAD_DIM

    for r in range(r_heads):
        qm_ref[0:hd, r * t:(r + 1) * t] = q_ref[0, 0, r * hd:(r + 1) * hd, :]
    qm_ref[hd:, :] = jnp.zeros((LANE - hd, rows), BF16)
    tok = t0 + lax.broadcasted_iota(jnp.int32, (1, t), 1)
    tok_r = _lane_tile(tok, r_heads)

    nb = kc_ref.shape[1]
    s_c = _dot(kc_ref[0], qm_ref[...])
    c_end = lax.broadcasted_iota(jnp.int32, (nb, 1), 0) * CMP_STRIDE + (CMP_BLOCK - 1)
    mask_c = c_end <= tok_r
    s_c = jnp.where(mask_c, s_c, NEG)
    s_c = s_c - jnp.max(s_c, axis=0, keepdims=True)
    p_c = jnp.where(mask_c, jnp.exp2(s_c), 0.0)
    p_c = (p_c / jnp.maximum(jnp.sum(p_c, axis=0, keepdims=True), 1e-30)).astype(BF16)
    o_c = _dot(vc_ref[0], p_c)
    imp4 = _dot(ovl_ref[...], p_c)
    imp = imp4[:, 0:t]
    for r in range(1, r_heads):
        imp = imp + imp4[:, r * t:(r + 1) * t]

    n_slc = seq_len // SLC_BLOCK
    n_sel = min(N_SLC, n_slc)
    blk = lax.broadcasted_iota(jnp.int32, (MAX_SLC_BLOCKS, 1), 0)
    dist = (tok // SLC_BLOCK) - blk
    forced = (blk == 0) | ((dist >= 0) & (dist < N_LOCAL_BLOCKS))
    score = jnp.where(forced, SEL_FORCE_SCORE, imp)
    score = jnp.where(blk * SLC_BLOCK <= tok, score, -jnp.inf)
    groups = [score[v * SUBLANE:(v + 1) * SUBLANE] for v in range(MAX_SLC_BLOCKS // SUBLANE)]
    ranks = [jnp.zeros((SUBLANE, t), F32) for _ in groups]
    sub = lax.broadcasted_iota(jnp.int32, (SUBLANE, 1), 0)
    for j in range(n_slc):
        row = score[j:j + 1]
        for v, sv in enumerate(groups):
            ge = jnp.where(row >= sv, 1.0, 0.0)
            gt = jnp.where(row > sv, 1.0, 0.0)
            if v * SUBLANE > j:
                beats = ge
            elif (v + 1) * SUBLANE <= j + 1:
                beats = gt
            else:
                beats = jnp.where(sub + v * SUBLANE > j, ge, gt)
            ranks[v] = ranks[v] + beats
    selbias = jnp.where(jnp.concatenate(ranks, axis=0) < n_sel, 0.0, NEG).astype(BF16)
    for r in range(r_heads):
        qm_ref[hd:hd + MAX_SLC_BLOCKS, r * t:(r + 1) * t] = selbias
    key_off = lax.broadcasted_iota(jnp.int32, (t, 1), 0)
    q_off = lax.broadcasted_iota(jnp.int32, (1, t), 1)
    causal = _lane_tile(jnp.where(key_off <= q_off, 0.0, NEG), r_heads)
    neg8 = jnp.full((SUBLANE, rows), NEG, F32)

    def key_tile(k_ref_, c):
        return k_ref_[0, pl.ds(pl.multiple_of(c * t, t), t), :]

    def slc_scores(c, mx):
        s = _dot(key_tile(ks_ref, c), qm_ref[...])
        ss_ref[c] = s
        return jnp.maximum(mx, _fold(s, jnp.maximum))

    mx = _loop_pairs(i, slc_scores, neg8)
    s = _dot(key_tile(ks_ref, i), qm_ref[...]) + causal
    ss_ref[i] = s
    m_s = jnp.max(jnp.maximum(mx, _fold(s, jnp.maximum)), axis=0, keepdims=True)

    n_back = -(-WINDOW // t)
    win_tiles = []
    mxw = neg8
    for back in range(n_back, -1, -1):
        if back:
            lo_edge = back * t - WINDOW
            bias = _lane_tile(jnp.where(key_off - q_off > jnp.where(i >= back, lo_edge, t), 0.0, NEG), r_heads)
        else:
            bias = causal
        c = jnp.maximum(i - back, 0)
        s = _dot(key_tile(kw_ref, c), qm_ref[...]) + bias
        sw_ref[n_back - back] = s
        mxw = jnp.maximum(mxw, _fold(s, jnp.maximum))
        win_tiles.append(c)
    m_w = jnp.max(mxw, axis=0, keepdims=True)

    as_ref[...] = jnp.zeros(as_ref.shape, F32)

    def slc_values(c, carry):
        as_ref[...] += _dot(_with_ones(vs_ref[0, c]), _weights(ss_ref[c], m_s))
        return carry

    _loop_pairs(i + 1, slc_values, 0)
    o_s = as_ref[:hd] * (1.0 / as_ref[hd:hd + 1])

    o_w = jnp.zeros(as_ref.shape, F32)
    for n, c in enumerate(win_tiles):
        o_w = o_w + _dot(_with_ones(vw_ref[0, c]), _weights(sw_ref[n], m_w))
    o_w = o_w[:hd] * (1.0 / o_w[hd:hd + 1])

    gates = sm_ref[0, 0]

    def gate(branch, r):
        rows_g = [gates[branch * Q_HEADS + g * r_heads + r:branch * Q_HEADS + g * r_heads + r + 1]
                  for g in range(NSA_KV_GROUPS)]
        out = rows_g[0]
        for g in range(1, NSA_KV_GROUPS):
            out = jnp.where(gi == g, rows_g[g], out)
        return out

    heads = []
    for r in range(r_heads):
        sl = slice(r * t, (r + 1) * t)
        heads.append(gate(0, r) * o_c[:, sl] + gate(1, r) * o_s[:, sl] + gate(2, r) * o_w[:, sl])
    o_ref[0] = jnp.concatenate(heads, axis=0).T.astype(o_ref.dtype)


def _overlap_t(seq_len):
    nb = seq_len // CMP_STRIDE
    n_cmp = (seq_len - CMP_BLOCK) // CMP_STRIDE + 1
    n_slc = seq_len // SLC_BLOCK
    c_start = np.arange(n_cmp)[None, :] * CMP_STRIDE
    s_start = np.arange(n_slc)[:, None] * SLC_BLOCK
    ov = np.minimum(c_start + CMP_BLOCK, s_start + SLC_BLOCK) - np.maximum(c_start, s_start)
    ovl = np.zeros((MAX_SLC_BLOCKS, nb), np.float32)
    ovl[:n_slc, :n_cmp] = np.clip(ov, 0, None) / CMP_BLOCK
    return jnp.asarray(ovl, BF16)


def _nsa(q_t, kc, vc_t, ks, vs_t, kw, vw_t, small_t, bsz, seq_len, t):
    g, r, hd = NSA_KV_GROUPS, NSA_GROUP_SIZE, HEAD_DIM
    assert seq_len // SLC_BLOCK <= MAX_SLC_BLOCKS
    nb = seq_len // CMP_STRIDE
    nc = seq_len // t
    rows = r * t
    ovl = _overlap_t(seq_len)
    k_spec = pl.BlockSpec((1, seq_len, LANE), lambda b, gi, i: (b, 0, gi))
    v_spec = pl.BlockSpec((1, nc, hd, t), lambda b, gi, i: (b, 0, gi, 0))
    return pl.pallas_call(
        functools.partial(_nsa_kernel, t=t, seq_len=seq_len),
        grid=(bsz, g, nc),
        in_specs=[pl.BlockSpec((1, 1, r * hd, t), lambda b, gi, i: (b, i, gi, 0)),
                  pl.BlockSpec((1, nb, LANE), lambda b, gi, i: (b, 0, gi)),
                  pl.BlockSpec((1, hd, nb), lambda b, gi, i: (b, gi, 0)),
                  k_spec, v_spec, k_spec, v_spec,
                  pl.BlockSpec((1, 1, SMALL_W, t), lambda b, gi, i: (b, i, 0, 0)),
                  _const_spec(ovl.shape)],
        out_specs=pl.BlockSpec((1, t, r * hd), lambda b, gi, i: (b, i, gi)),
        out_shape=jax.ShapeDtypeStruct((bsz, seq_len, Q_WIDTH), BF16),
        scratch_shapes=[pltpu.VMEM((LANE, rows), BF16), pltpu.VMEM((nc, t, rows), F32),
                        pltpu.VMEM((-(-WINDOW // t) + 1, t, rows), F32),
                        pltpu.VMEM((hd + ONES_ROWS, rows), F32)],
        compiler_params=_params("parallel", "parallel", "parallel"),
        name="nsa_attn",
    )(q_t.reshape(bsz, nc, Q_WIDTH, t), kc, vc_t, ks.reshape(bsz, seq_len, -1), vs_t.reshape(bsz, nc, g * hd, t),
      kw.reshape(bsz, seq_len, -1), vw_t.reshape(bsz, nc, g * hd, t), small_t.reshape(bsz, nc, SMALL_W, t), ovl)


def _dsa_kernel(q_ref, k_ref, v_ref, iq_ref, ik_ref, sm_ref, o_ref,
                qm_ref, iqm_ref, sc_ref, sc16_ref, s_ref, acc_ref, kth_ref, *, t, seq_len, k_top):
    i = pl.program_id(1)
    t0 = i * t
    nck = i + 1
    hd = HEAD_DIM
    tok = t0 + lax.broadcasted_iota(jnp.int32, (1, t), 1)

    zeros = jnp.zeros((hd, t), BF16)
    for h in range(Q_HEADS):
        qh = q_ref[0, 0, h * hd:(h + 1) * hd, :]
        qm_ref[h] = jnp.concatenate([qh, zeros] if h % 2 == 0 else [zeros, qh], axis=0)
        iqm_ref[h] = jnp.concatenate([iq_ref[0, 0, h * hd:(h + 1) * hd, :], zeros], axis=0)
    small = sm_ref[0, 0]

    def key_pos(c):
        start = c * t if isinstance(c, int) else pl.multiple_of(c * t, t)
        return start + lax.broadcasted_iota(jnp.int32, (t, 1), 0)

    def score_body(c, carry):
        ik = ik_ref[0, pl.ds(pl.multiple_of(c * t, t), t), :]
        sc = jnp.zeros((t, t), F32)
        for h in range(IDX_HEADS):
            sc = sc + jnp.maximum(_dot(ik, iqm_ref[h]), 0.0) * small[N_GATE + h:N_GATE + h + 1]
        sc = jnp.where(key_pos(c) <= tok, sc, -jnp.inf)
        sc_ref[c] = sc
        sc16_ref[c] = sc.astype(BF16)
        return carry

    _loop_pairs(nck, score_body, 0, DSA_TILES_PER_TRIP)

    def count(indicator):
        def body(c, acc):
            return acc + _fold(indicator(sc_ref[c], key_pos(c)), jnp.add)
        acc = _loop_pairs(nck, body, jnp.zeros((SUBLANE, t), F32))
        return jnp.sum(acc, axis=0, keepdims=True)

    def count_tiles(tiles_ref, n_tiles, indicator, group):
        acc = None
        for c in range(n_tiles):
            part = _fold(indicator(tiles_ref[c]), jnp.add, group)
            acc = part if acc is None else acc + part
        return acc

    kf = float(k_top)
    zero = jnp.zeros((1, t), jnp.int32)

    def decode(code):
        return lax.bitcast_convert_type(jnp.where(code < 0, code ^ jnp.int32(0x7FFFFFFF), code), F32)

    def search(n_tiles):
        one16, zero16 = jnp.ones((), BF16), jnp.zeros((), BF16)

        def decode16(code16):
            pattern = jnp.where(code16 < 0, code16 ^ jnp.int32(0x7FFF), code16)
            return lax.bitcast_convert_type(lax.shift_left(pattern, 16), F32)

        def reached16(code16):
            val = decode16(code16).astype(BF16)
            acc = count_tiles(sc16_ref, n_tiles, lambda sc: jnp.where(sc >= val, one16, zero16), BF16_ROWS)
            return (jnp.sum(acc.astype(F32), axis=0, keepdims=True) >= kf) | (code16 <= KEY16_NEG_INF)

        def bit16_body(b, base16):
            cand = base16 + lax.shift_left(jnp.int32(1), 14 - b)
            return jnp.where(reached16(cand), cand, base16)

        base16 = jnp.where(reached16(zero), zero, jnp.full((1, t), -(2 ** 15), jnp.int32))
        below = lax.fori_loop(0, 15, bit16_body, base16) - 1
        lo = lax.shift_left(below, 16) + jnp.where(below < 0, 0xFFFF, 0)

        def count32(indicator):
            return jnp.sum(count_tiles(sc_ref, n_tiles, indicator, SUBLANE), axis=0, keepdims=True)

        def reached(code):
            val = decode(code)
            return (count32(lambda sc: jnp.where(sc >= val, 1.0, 0.0)) >= kf) | (code <= KEY_NEG_INF)

        def bit_body(b, base):
            cand = base + lax.shift_left(jnp.int32(1), 16 - b)
            return jnp.where(reached(cand), cand, base)

        kth = decode(jnp.maximum(lax.fori_loop(0, 17, bit_body, lo), KEY_NEG_INF))
        return (kth, count32(lambda sc: jnp.where(sc >= kth, 1.0, 0.0)),
                count32(lambda sc: jnp.where(sc > kth, 1.0, 0.0)))

    for n_tiles in range(1, seq_len // t + 1):
        @pl.when(nck == n_tiles)
        def _(n_tiles=n_tiles):
            if n_tiles * t <= k_top:
                found = (jnp.full((1, t), -jnp.inf, F32), jnp.full((1, t), kf, F32), jnp.zeros((1, t), F32))
            else:
                found = search(n_tiles)
            for r, v in enumerate(found):
                kth_ref[r:r + 1, :] = v

    kth = kth_ref[0:1, :]
    n_ge, n_gt = kth_ref[1:2, :], kth_ref[2:3, :]

    tie = (n_ge > kf) & (kth > -jnp.inf)

    def tie_cut():
        room = kf - n_gt
        top_bit = seq_len.bit_length() - 1

        def pos_body(b, x):
            cand = x + lax.shift_left(jnp.int32(1), top_bit - b)
            n = count(lambda sc, pos: jnp.where(sc == kth, jnp.where(pos < cand, 1.0, 0.0), 0.0))
            return jnp.where(n <= room, cand, x)

        x = lax.fori_loop(0, top_bit + 1, pos_body, zero)
        return jnp.where(tie, x, seq_len)

    cut = lax.cond(jnp.max(jnp.where(tie, 1.0, 0.0)) > 0.0, tie_cut,
                   lambda: jnp.full((1, t), seq_len, jnp.int32))
    cut = jnp.where(kth > -jnp.inf, cut, 0)

    def att_scores(c, mx):
        k0 = pl.multiple_of(c * t, t)
        sc = sc_ref[c]
        bias = jnp.where(sc > kth, 0.0, jnp.where(sc == kth, jnp.where(key_pos(c) < cut, 0.0, NEG), NEG))
        out = []
        for h in range(Q_HEADS):
            s = _dot(k_ref[0, pl.ds(k0, t), (h // 2) * LANE:(h // 2 + 1) * LANE], qm_ref[h]) + bias
            s_ref[h, c] = s
            out.append(jnp.maximum(mx[h], _fold(s, jnp.maximum)))
        return tuple(out)

    mx = _loop_pairs(nck, att_scores, (jnp.full((SUBLANE, t), NEG, F32),) * Q_HEADS, DSA_TILES_PER_TRIP)
    m = [jnp.max(mx[h], axis=0, keepdims=True) for h in range(Q_HEADS)]

    acc_ref[...] = jnp.zeros(acc_ref.shape, F32)

    def att_values(c, carry):
        for h in range(Q_HEADS):
            acc_ref[h] += _dot(_with_ones(v_ref[0, c, h * hd:(h + 1) * hd, :]), _weights(s_ref[h, c], m[h]))
        return carry

    _loop_pairs(nck, att_values, 0, DSA_TILES_PER_TRIP)
    out = jnp.concatenate([acc_ref[h, :hd] * (1.0 / acc_ref[h, hd:hd + 1])
                           for h in range(Q_HEADS)], axis=0)
    o_ref[0] = out.T.astype(o_ref.dtype)


def _dsa(q_t, k, v_t, iq_t, ik, small_t, bsz, seq_len, t):
    k_top = min(DSA_TOPK_MAX, seq_len // 4)
    assert t >= k_top
    nc = seq_len // t
    qt_spec = pl.BlockSpec((1, 1, Q_WIDTH, t), lambda b, i: (b, i, 0, 0))
    return pl.pallas_call(
        functools.partial(_dsa_kernel, t=t, seq_len=seq_len, k_top=k_top),
        grid=(bsz, nc),
        in_specs=[qt_spec,
                  pl.BlockSpec((1, seq_len, Q_WIDTH), lambda b, i: (b, 0, 0)),
                  pl.BlockSpec((1, nc, Q_WIDTH, t), lambda b, i: (b, 0, 0, 0)),
                  qt_spec,
                  pl.BlockSpec((1, seq_len, LANE), lambda b, i: (b, 0, 0)),
                  pl.BlockSpec((1, 1, SMALL_W, t), lambda b, i: (b, i, 0, 0))],
        out_specs=pl.BlockSpec((1, t, Q_WIDTH), lambda b, i: (b, i, 0)),
        out_shape=jax.ShapeDtypeStruct((bsz, seq_len, Q_WIDTH), BF16),
        scratch_shapes=[pltpu.VMEM((Q_HEADS, LANE, t), BF16), pltpu.VMEM((IDX_HEADS, LANE, t), BF16),
                        pltpu.VMEM((nc, t, t), F32), pltpu.VMEM((nc, t, t), BF16),
                        pltpu.VMEM((Q_HEADS, nc, t, t), F32),
                        pltpu.VMEM((Q_HEADS, HEAD_DIM + ONES_ROWS, t), F32), pltpu.VMEM((SUBLANE, t), F32)],
        compiler_params=_params("parallel", "parallel"),
        name="dsa_attn",
    )(q_t.reshape(bsz, nc, Q_WIDTH, t), k.reshape(bsz, seq_len, -1), v_t.reshape(bsz, nc, Q_WIDTH, t),
      iq_t.reshape(bsz, nc, Q_WIDTH, t), ik.reshape(bsz, seq_len, -1), small_t.reshape(bsz, nc, SMALL_W, t))


def _merge_ffn_kernel(x_ref, yn_ref, yd_ref, gn_ref, gd_ref, wbn_ref, wbd_ref, wo_ref,
                      g_ref, wg_ref, wu_ref, wd_ref, fg_ref, o_ref, *, final):
    merged = gn_ref[...] * _dot(yn_ref[...], wbn_ref[...]) + gd_ref[...] * _dot(yd_ref[...], wbd_ref[...])
    x = x_ref[...] + _dot(merged.astype(BF16), wo_ref[...])
    h = _rmsnorm(x, g_ref[...]).astype(BF16)
    z = (jax.nn.silu(_dot(h, wg_ref[...])) * _dot(h, wu_ref[...])).astype(BF16)
    y = x + _dot(z, wd_ref[...])
    o_ref[...] = _rmsnorm(y, fg_ref[...]) if final else y


def _merge_ffn(xf, y_nsa, y_dsa, gate_nsa, gate_dsa, wbn, wbd, wo, g, wg, wu, wd, final_g, final, tm):
    t, d = xf.shape
    f = wg.shape[1]
    wn = y_nsa.shape[1]
    row = lambda i: (i, 0)
    return pl.pallas_call(
        functools.partial(_merge_ffn_kernel, final=final),
        grid=(t // tm,),
        in_specs=[pl.BlockSpec((tm, d), row), pl.BlockSpec((tm, wn), row), pl.BlockSpec((tm, wn), row),
                  pl.BlockSpec((tm, d), row), pl.BlockSpec((tm, d), row),
                  _const_spec((wn, d)), _const_spec((wn, d)), _const_spec((d, d)),
                  _const_spec((1, d)), _const_spec((d, f)), _const_spec((d, f)), _const_spec((f, d)),
                  _const_spec((1, d))],
        out_specs=pl.BlockSpec((tm, d), row),
        out_shape=jax.ShapeDtypeStruct((t, d), F32),
        compiler_params=_params("parallel"),
        name="merge_ffn",
    )(xf, y_nsa, y_dsa, gate_nsa, gate_dsa, wbn.astype(BF16), wbd.astype(BF16), wo.astype(BF16),
      g.reshape(1, d), wg.astype(BF16), wu.astype(BF16), wd.astype(BF16), final_g.reshape(1, d))


def _rope_tables(seq_len, tm):
    pos = jnp.arange(seq_len, dtype=F32)
    inv_freq = ROPE_THETA ** (-jnp.arange(0, ROPE_DIM, 2, dtype=F32) / ROPE_DIM)
    ang = pos[:, None] * inv_freq[None, :]
    cos, sin = jnp.cos(ang), jnp.sin(ang)
    rest = HEAD_DIM - ROPE_DIM
    one, zero = jnp.ones((seq_len, rest), F32), jnp.zeros((seq_len, rest), F32)
    zh = jnp.zeros((seq_len, ROPE_HALF), F32)
    two = lambda a: jnp.concatenate([a, a], axis=1)
    tile_t = lambda a: a.reshape(seq_len // tm, tm, ROPE_HALF).transpose(0, 2, 1)
    return (two(jnp.concatenate([cos, cos, one], axis=1)),
            two(jnp.concatenate([zh, sin, zero], axis=1)),
            two(jnp.concatenate([-sin, zh, zero], axis=1)),
            tile_t(cos), tile_t(sin))


def _tile(seq_len):
    return min(256, seq_len)


def kernel(x, attn_norm, w_in, cmpk_pos, cmpk_w1, cmpk_b1, cmpk_w2, cmpk_b2, cmpv_pos, cmpv_w1, cmpv_b1,
           cmpv_w2, cmpv_b2, w_branch_nsa, w_branch_dsa, w_out, ffn_norm, w_ffn_gate, w_ffn_up, w_ffn_down,
           final_norm):
    bsz, seq_len, d = x.shape
    depth = w_in.shape[0]
    t = _tile(seq_len)
    tables = _rope_tables(seq_len, t)
    xf = x.reshape(bsz * seq_len, d)
    for l in range(depth):
        (q_nsa, k_cmp, v_cmp, k_slc, v_slc, k_win, v_win, q_dsa, k_dsa, v_dsa, q_idx, k_idx,
         gate_nsa, gate_dsa, small) = _proj(xf, attn_norm[l], *_split_w_in(w_in[l]), tables, seq_len, t)
        kc, vc = _compress(k_cmp, v_cmp, cmpk_pos[l], cmpk_w1[l], cmpk_b1[l], cmpk_w2[l], cmpk_b2[l],
                           cmpv_pos[l], cmpv_w1[l], cmpv_b1[l], cmpv_w2[l], cmpv_b2[l], bsz, seq_len)
        y_nsa = _nsa(q_nsa, kc, vc, k_slc, v_slc, k_win, v_win, small, bsz, seq_len, t)
        y_dsa = _dsa(q_dsa, k_dsa, v_dsa, q_idx, k_idx, small, bsz, seq_len, t)
        xf = _merge_ffn(xf, y_nsa.reshape(bsz * seq_len, -1), y_dsa.reshape(bsz * seq_len, -1),
                        gate_nsa, gate_dsa, w_branch_nsa[l], w_branch_dsa[l], w_out[l],
                        ffn_norm[l], w_ffn_gate[l], w_ffn_up[l], w_ffn_down[l], final_norm,
                        final=(l == depth - 1), tm=t)
    return xf.reshape(bsz, seq_len, d)
```

```python
import functools

import numpy as np
import jax
import jax.numpy as jnp
from jax import lax
from jax.experimental import pallas as pl
from jax.experimental.pallas import tpu as pltpu

HEAD_DIM = 64
ROPE_DIM = HEAD_DIM // 4
ROPE_HALF = ROPE_DIM // 2
ROPE_THETA = 500000.0
NORM_EPS = 1e-6
ATTN_SCALE = HEAD_DIM ** -0.5
Q_SCALE = ATTN_SCALE * float(np.log2(np.e))
Q_HEADS = 8
Q_WIDTH = Q_HEADS * HEAD_DIM
NSA_KV_GROUPS = 2
NSA_GROUP_SIZE = Q_HEADS // NSA_KV_GROUPS
CMP_BLOCK = 32
CMP_STRIDE = 16
CMP_HIDDEN = 2 * HEAD_DIM
SLC_BLOCK = 64
N_SLC = 16
N_LOCAL_BLOCKS = 2
SEL_FORCE_SCORE = 1e4
WINDOW = 512
IDX_HEADS = 8
IDX_DIM = 64
IDX_WEIGHT_SCALE = (IDX_HEADS * IDX_DIM) ** -0.5
DSA_TOPK_MAX = 256
N_GATE = 3 * Q_HEADS
SMALL_W = N_GATE + IDX_HEADS
MAX_SLC_BLOCKS = 32

LANE = 128
SUBLANE = 8
ONES_ROWS = 16
V7X_VMEM_LIMIT = 56 * 1024 * 1024

NEG = -1e30
INT_MIN = -(2 ** 31)
KEY_NEG_INF = -2139095041
KEY16_NEG_INF = -32641
BF16_ROWS = 16
DSA_TILES_PER_TRIP = 4
NSA_TILES_PER_TRIP = 4
BF16 = jnp.bfloat16
F32 = jnp.float32

_NT = (((1,), (1,)), ((), ()))


def _dot(a, b):
    return jnp.dot(a, b, preferred_element_type=F32)


def _dot_nt(a, b):
    return lax.dot_general(a, b, _NT, preferred_element_type=F32)


def _const_spec(shape):
    nd = len(shape)
    return pl.BlockSpec(shape, lambda *_: (0,) * nd, pipeline_mode=pl.Buffered(1))


def _rmsnorm(x, g):
    ms = jnp.mean(x * x, axis=-1, keepdims=True)
    return x * lax.rsqrt(ms + NORM_EPS) * g


def _lane_tile(a, reps):
    return a if reps == 1 else jnp.concatenate([a] * reps, axis=1)


def _params(*sem):
    return pltpu.CompilerParams(dimension_semantics=sem, vmem_limit_bytes=V7X_VMEM_LIMIT)


_PROJ_SEGS = (
    ("t", Q_WIDTH, "ropeq", BF16),
    ("n", 128, "rope", F32),
    ("n", 128, "plain", F32),
    ("n", 256, "rope_onehot", BF16),
    ("t", 128, "plain", BF16),
    ("n", 256, "rope", BF16),
    ("t", 128, "plain", BF16),
    ("t", Q_WIDTH, "ropeq", BF16),
    ("n", Q_WIDTH, "rope", BF16),
    ("t", Q_WIDTH, "plain", BF16),
    ("t", IDX_HEADS * IDX_DIM, "rope", BF16),
    ("n", 128, "rope", BF16),
    ("n", 1024, "sig", F32),
    ("n", 1024, "sig", F32),
    ("t", SMALL_W, "small", F32),
)
_PROJ_NW = sum(s[1] for s in _PROJ_SEGS if s[0] == "n")
_PROJ_TW = sum(s[1] for s in _PROJ_SEGS if s[0] == "t")
_PROJ_CHUNK = 512


def _rope_n(y, c, sa, sb):
    pieces = []
    for j in range(y.shape[1] // LANE):
        yj = y[:, j * LANE:(j + 1) * LANE]
        pieces.append(yj * c + pltpu.roll(yj, ROPE_HALF, 1) * sa + pltpu.roll(yj, LANE - ROPE_HALF, 1) * sb)
    return pieces[0] if len(pieces) == 1 else jnp.concatenate(pieces, axis=1)


def _rope_t(y, cos, sin):
    pieces = []
    for h in range(y.shape[0] // HEAD_DIM):
        r0 = h * HEAD_DIM
        x1, x2 = y[r0:r0 + ROPE_HALF], y[r0 + ROPE_HALF:r0 + ROPE_DIM]
        pieces += [x1 * cos - x2 * sin, x2 * cos + x1 * sin, y[r0 + ROPE_DIM:r0 + HEAD_DIM]]
    return jnp.concatenate(pieces, axis=0)


def _proj_kernel(x_ref, g_ref, wn_ref, wt_ref, c_ref, sa_ref, sb_ref, ct_ref, st_ref, *out_refs, npos):
    tm = x_ref.shape[0]
    h = _rmsnorm(x_ref[...], g_ref[...]).astype(BF16)
    c, sa, sb = c_ref[...], sa_ref[...], sb_ref[...]
    cos_t, sin_t = ct_ref[0], st_ref[0]
    ncol = trow = 0
    for (layout, width, kind, _), o_ref in zip(_PROJ_SEGS, out_refs):
        if layout == "t":
            y = _dot_nt(wt_ref[trow:trow + width, :], h)
            trow += width
            if kind in ("rope", "ropeq"):
                y = _rope_t(y, cos_t, sin_t)
            if kind == "ropeq":
                y = y * Q_SCALE
            if kind == "small":
                row = lax.broadcasted_iota(jnp.int32, y.shape, 0)
                y = jnp.where(row < N_GATE, jax.nn.sigmoid(y), y * IDX_WEIGHT_SCALE)
            o_ref[0] = y.astype(o_ref.dtype)
            continue
        for c0 in range(0, width, _PROJ_CHUNK):
            cw = min(_PROJ_CHUNK, width - c0)
            y = _dot(h, wn_ref[:, ncol + c0:ncol + c0 + cw])
            if kind in ("rope", "rope_onehot"):
                y = _rope_n(y, c, sa, sb)
            if kind == "rope_onehot":
                pos = (pl.program_id(0) % npos) * tm + lax.broadcasted_iota(jnp.int32, y.shape, 0)
                lane = lax.broadcasted_iota(jnp.int32, y.shape, 1) & (LANE - 1)
                y = jnp.where(lane == HEAD_DIM + pos // SLC_BLOCK, 1.0, y)
            if kind == "sig":
                y = jax.nn.sigmoid(y)
            o_ref[:, c0:c0 + cw] = y.astype(o_ref.dtype)
        ncol += width


def _split_w_in(w):
    d = w.shape[0]
    kvw = NSA_KV_GROUPS * HEAD_DIM
    sizes = (Q_WIDTH, kvw, kvw, kvw, kvw, kvw, kvw, N_GATE, Q_WIDTH, Q_WIDTH, Q_WIDTH,
             IDX_HEADS * IDX_DIM, IDX_DIM, IDX_HEADS, d, d)
    assert sum(sizes) == w.shape[1]
    offs = np.concatenate([[0], np.cumsum(sizes)])
    (nsa_q, k_cmp, v_cmp, k_slc, v_slc, k_win, v_win, nsa_gate, dsa_q, dsa_k, dsa_v,
     idx_q, idx_k, idx_w, gate_nsa, gate_dsa) = [w[:, offs[i]:offs[i + 1]] for i in range(16)]

    def pad_groups(a):
        a = a.reshape(d, NSA_KV_GROUPS, HEAD_DIM)
        return jnp.concatenate([a, jnp.zeros_like(a)], axis=2).reshape(d, NSA_KV_GROUPS * LANE)

    idx_k_pad = jnp.concatenate([idx_k, jnp.zeros_like(idx_k)], axis=1)
    w_n = jnp.concatenate([k_cmp, v_cmp, pad_groups(k_slc), pad_groups(k_win), dsa_k, idx_k_pad,
                           gate_nsa, gate_dsa], axis=1)
    w_t = jnp.concatenate([nsa_q, v_slc, v_win, dsa_q, dsa_v, idx_q, nsa_gate, idx_w], axis=1).T
    assert w_n.shape[1] == _PROJ_NW and w_t.shape[0] == _PROJ_TW
    return w_n.astype(BF16), w_t.astype(BF16)


def _proj(xf, g, w_n, w_t, tables, seq_len, tm):
    t, d = xf.shape
    npos = seq_len // tm
    row = lambda i: (i, 0)
    tab_n = pl.BlockSpec((tm, LANE), lambda i: (i % npos, 0))
    tab_t = pl.BlockSpec((1, ROPE_HALF, tm), lambda i: (i % npos, 0, 0))
    out_specs, out_shape = [], []
    for layout, width, _, dt in _PROJ_SEGS:
        if layout == "t":
            out_specs.append(pl.BlockSpec((1, width, tm), lambda i: (i, 0, 0)))
            out_shape.append(jax.ShapeDtypeStruct((t // tm, width, tm), dt))
        else:
            out_specs.append(pl.BlockSpec((tm, width), row))
            out_shape.append(jax.ShapeDtypeStruct((t, width), dt))
    return pl.pallas_call(
        functools.partial(_proj_kernel, npos=npos),
        grid=(t // tm,),
        in_specs=[pl.BlockSpec((tm, d), row), _const_spec((1, d)), _const_spec((d, _PROJ_NW)),
                  _const_spec((_PROJ_TW, d)), tab_n, tab_n, tab_n, tab_t, tab_t],
        out_specs=out_specs,
        out_shape=out_shape,
        compiler_params=_params("parallel"),
        name="proj",
    )(xf, g.reshape(1, d), w_n, w_t, *tables)


def _cmp_kernel(k_ref, v_ref, pk_ref, w1k_ref, b1k_ref, w2k_ref, b2k_ref,
                pv_ref, w1v_ref, b1v_ref, w2v_ref, b2v_ref, kc_ref, vc_ref, *, n_cmp):
    nb = kc_ref.shape[1]

    def hidden(x_ref, pos_ref, w1_ref, b1_ref):
        halves = []
        for half in range(CMP_BLOCK // CMP_STRIDE):
            acc = None
            for l in range(CMP_STRIDE):
                j = half * CMP_STRIDE + l
                x = x_ref[0, pl.ds(l, nb, stride=CMP_STRIDE), :] + pos_ref[j:j + 1, :]
                part = _dot(x.astype(BF16), w1_ref[j])
                acc = part if acc is None else acc + part
            halves.append(acc)
        return jax.nn.gelu(halves[0] + pltpu.roll(halves[1], nb - 1, 0) + b1_ref[...]).astype(BF16)

    kc = _dot(hidden(k_ref, pk_ref, w1k_ref, b1k_ref), w2k_ref[...]) + b2k_ref[...]
    rows = lax.broadcasted_iota(jnp.int32, kc.shape, 0)
    kc_ref[0] = jnp.where(rows < n_cmp, kc, 0.0).astype(kc_ref.dtype)
    vc = _dot_nt(w2v_ref[...], hidden(v_ref, pv_ref, w1v_ref, b1v_ref)) + b2v_ref[...]
    cols = lax.broadcasted_iota(jnp.int32, vc.shape, 1)
    vc_ref[0] = jnp.where(cols < n_cmp, vc, 0.0).astype(vc_ref.dtype)


def _compress(k_cmp, v_cmp, pk, w1k, b1k, w2k, b2k, pv, w1v, b1v, w2v, b2v, bsz, seq_len):
    g, hd = NSA_KV_GROUPS, HEAD_DIM
    assert CMP_BLOCK == 2 * CMP_STRIDE
    nb = seq_len // CMP_STRIDE
    n_cmp = (seq_len - CMP_BLOCK) // CMP_STRIDE + 1
    eye = jnp.eye(g, dtype=F32)
    per_group = lambda m: jnp.kron(eye, m)
    tile_g = lambda a: jnp.concatenate([a] * g, axis=-1)

    def first_layer(pos, w1, b1):
        w1_bd = jax.vmap(per_group)(w1.reshape(CMP_BLOCK, hd, CMP_HIDDEN)).astype(BF16)
        return tile_g(pos), w1_bd, tile_g(b1).reshape(1, -1)

    w2k_pad = jnp.concatenate([w2k, jnp.zeros_like(w2k)], axis=1)
    b2k_pad = jnp.concatenate([b2k, jnp.zeros_like(b2k)])
    k_w = first_layer(pk, w1k, b1k) + (per_group(w2k_pad).astype(BF16), tile_g(b2k_pad).reshape(1, -1))
    v_w = first_layer(pv, w1v, b1v) + (per_group(w2v.T).astype(BF16), tile_g(b2v).reshape(-1, 1))
    seq_spec = pl.BlockSpec((1, seq_len, g * hd), lambda b: (b, 0, 0))
    first = [_const_spec((CMP_BLOCK, g * hd)), _const_spec((CMP_BLOCK, g * hd, g * CMP_HIDDEN)),
             _const_spec((1, g * CMP_HIDDEN))]
    return pl.pallas_call(
        functools.partial(_cmp_kernel, n_cmp=n_cmp),
        grid=(bsz,),
        in_specs=[seq_spec, seq_spec] + first + [_const_spec((g * CMP_HIDDEN, g * LANE)), _const_spec((1, g * LANE))]
        + first + [_const_spec((g * hd, g * CMP_HIDDEN)), _const_spec((g * hd, 1))],
        out_specs=[pl.BlockSpec((1, nb, g * LANE), lambda b: (b, 0, 0)),
                   pl.BlockSpec((1, g * hd, nb), lambda b: (b, 0, 0))],
        out_shape=[jax.ShapeDtypeStruct((bsz, nb, g * LANE), BF16),
                   jax.ShapeDtypeStruct((bsz, g * hd, nb), BF16)],
        compiler_params=_params("parallel"),
        name="compress",
    )(k_cmp.reshape(bsz, seq_len, g * hd), v_cmp.reshape(bsz, seq_len, g * hd), *k_w, *v_w)


def _fold(x, op, group=SUBLANE):
    parts = [x[r * group:(r + 1) * group] for r in range(x.shape[0] // group)]
    while len(parts) > 1:
        pairs = [op(parts[k], parts[k + 1]) for k in range(0, len(parts) - 1, 2)]
        parts = pairs + ([parts[-1]] if len(parts) % 2 else [])
    return parts[0]


def _loop_pairs(n, body, init, per_trip=2):
    def steps(start, count, carry):
        for s in range(count):
            carry = body(start + s, carry)
        return carry

    carry = lax.fori_loop(0, n // per_trip, lambda j, c: steps(j * per_trip, per_trip, c), init)
    group = per_trip // 2
    while group:
        carry = lax.cond(n & group != 0, functools.partial(steps, n - n % (2 * group), group), lambda c: c, carry)
        group //= 2
    return carry


def _weights(s, m):
    return jnp.exp2((s - m).astype(BF16))


def _with_ones(v_t):
    return jnp.concatenate([v_t, jnp.ones((ONES_ROWS, v_t.shape[1]), v_t.dtype)], axis=0)


def _nsa_kernel(q_ref, kc_ref, vc_ref, ks_ref, vs_ref, kw_ref, vw_ref, sm_ref, ovl_ref, o_ref,
                qm_ref, ss_ref, sw_ref, as_ref, *, t, seq_len):
    r_heads = NSA_GROUP_SIZE
    gi = pl.program_id(1)
    i = pl.program_id(2)
    t0 = i * t
    rows = r_heads * t
    hd = HEAD_DIM

    for r in range(r_heads):
        qm_ref[0:hd, r * t:(r + 1) * t] = q_ref[0, 0, r * hd:(r + 1) * hd, :]
    qm_ref[hd:, :] = jnp.zeros((LANE - hd, rows), BF16)
    tok = t0 + lax.broadcasted_iota(jnp.int32, (1, t), 1)
    tok_r = _lane_tile(tok, r_heads)

    nb = kc_ref.shape[1]
    s_c = _dot(kc_ref[0], qm_ref[...])
    c_end = lax.broadcasted_iota(jnp.int32, (nb, 1), 0) * CMP_STRIDE + (CMP_BLOCK - 1)
    mask_c = c_end <= tok_r
    s_c = jnp.where(mask_c, s_c, NEG)
    s_c = s_c - jnp.max(s_c, axis=0, keepdims=True)
    p_c = jnp.where(mask_c, jnp.exp2(s_c), 0.0)
    p_c = (p_c / jnp.maximum(jnp.sum(p_c, axis=0, keepdims=True), 1e-30)).astype(BF16)
    o_c = _dot(vc_ref[0], p_c)
    imp4 = _dot(ovl_ref[...], p_c)
    imp = imp4[:, 0:t]
    for r in range(1, r_heads):
        imp = imp + imp4[:, r * t:(r + 1) * t]

    n_slc = seq_len // SLC_BLOCK
    n_sel = min(N_SLC, n_slc)
    blk = lax.broadcasted_iota(jnp.int32, (MAX_SLC_BLOCKS, 1), 0)
    dist = (tok // SLC_BLOCK) - blk
    forced = (blk == 0) | ((dist >= 0) & (dist < N_LOCAL_BLOCKS))
    score = jnp.where(forced, SEL_FORCE_SCORE, imp)
    score = jnp.where(blk * SLC_BLOCK <= tok, score, -jnp.inf)
    groups = [score[v * SUBLANE:(v + 1) * SUBLANE] for v in range(MAX_SLC_BLOCKS // SUBLANE)]
    ranks = [jnp.zeros((SUBLANE, t), F32) for _ in groups]
    sub = lax.broadcasted_iota(jnp.int32, (SUBLANE, 1), 0)
    for j in range(n_slc):
        row = score[j:j + 1]
        for v, sv in enumerate(groups):
            ge = jnp.where(row >= sv, 1.0, 0.0)
            gt = jnp.where(row > sv, 1.0, 0.0)
            if v * SUBLANE > j:
                beats = ge
            elif (v + 1) * SUBLANE <= j + 1:
                beats = gt
            else:
                beats = jnp.where(sub + v * SUBLANE > j, ge, gt)
            ranks[v] = ranks[v] + beats
    selbias = jnp.where(jnp.concatenate(ranks, axis=0) < n_sel, 0.0, NEG).astype(BF16)
    for r in range(r_heads):
        qm_ref[hd:hd + MAX_SLC_BLOCKS, r * t:(r + 1) * t] = selbias
    key_off = lax.broadcasted_iota(jnp.int32, (t, 1), 0)
    q_off = lax.broadcasted_iota(jnp.int32, (1, t), 1)
    causal = _lane_tile(jnp.where(key_off <= q_off, 0.0, NEG), r_heads)
    neg8 = jnp.full((SUBLANE, rows), NEG, F32)

    def key_tile(k_ref_, c):
        return k_ref_[0, pl.ds(pl.multiple_of(c * t, t), t), :]

    def slc_scores(c, mx):
        s = _dot(key_tile(ks_ref, c), qm_ref[...])
        ss_ref[c] = s
        return jnp.maximum(mx, _fold(s, jnp.maximum))

    mx = _loop_pairs(i, slc_scores, neg8, NSA_TILES_PER_TRIP)
    s = _dot(key_tile(ks_ref, i), qm_ref[...]) + causal
    ss_ref[i] = s
    m_s = jnp.max(jnp.maximum(mx, _fold(s, jnp.maximum)), axis=0, keepdims=True)

    n_back = -(-WINDOW // t)
    win_tiles = []
    mxw = neg8
    for back in range(n_back, -1, -1):
        if back:
            lo_edge = back * t - WINDOW
            bias = _lane_tile(jnp.where(key_off - q_off > jnp.where(i >= back, lo_edge, t), 0.0, NEG), r_heads)
        else:
            bias = causal
        c = jnp.maximum(i - back, 0)
        s = _dot(key_tile(kw_ref, c), qm_ref[...]) + bias
        sw_ref[n_back - back] = s
        mxw = jnp.maximum(mxw, _fold(s, jnp.maximum))
        win_tiles.append(c)
    m_w = jnp.max(mxw, axis=0, keepdims=True)

    as_ref[...] = jnp.zeros(as_ref.shape, F32)

    def slc_values(c, carry):
        as_ref[...] += _dot(_with_ones(vs_ref[0, c]), _weights(ss_ref[c], m_s))
        return carry

    _loop_pairs(i + 1, slc_values, 0, NSA_TILES_PER_TRIP)
    o_s = as_ref[:hd] * (1.0 / as_ref[hd:hd + 1])

    o_w = jnp.zeros(as_ref.shape, F32)
    for n, c in enumerate(win_tiles):
        o_w = o_w + _dot(_with_ones(vw_ref[0, c]), _weights(sw_ref[n], m_w))
    o_w = o_w[:hd] * (1.0 / o_w[hd:hd + 1])

    gates = sm_ref[0, 0]

    def gate(branch, r):
        rows_g = [gates[branch * Q_HEADS + g * r_heads + r:branch * Q_HEADS + g * r_heads + r + 1]
                  for g in range(NSA_KV_GROUPS)]
        out = rows_g[0]
        for g in range(1, NSA_KV_GROUPS):
            out = jnp.where(gi == g, rows_g[g], out)
        return out

    heads = []
    for r in range(r_heads):
        sl = slice(r * t, (r + 1) * t)
        heads.append(gate(0, r) * o_c[:, sl] + gate(1, r) * o_s[:, sl] + gate(2, r) * o_w[:, sl])
    o_ref[0] = jnp.concatenate(heads, axis=0).T.astype(o_ref.dtype)


def _overlap_t(seq_len):
    nb = seq_len // CMP_STRIDE
    n_cmp = (seq_len - CMP_BLOCK) // CMP_STRIDE + 1
    n_slc = seq_len // SLC_BLOCK
    c_start = np.arange(n_cmp)[None, :] * CMP_STRIDE
    s_start = np.arange(n_slc)[:, None] * SLC_BLOCK
    ov = np.minimum(c_start + CMP_BLOCK, s_start + SLC_BLOCK) - np.maximum(c_start, s_start)
    ovl = np.zeros((MAX_SLC_BLOCKS, nb), np.float32)
    ovl[:n_slc, :n_cmp] = np.clip(ov, 0, None) / CMP_BLOCK
    return jnp.asarray(ovl, BF16)


def _nsa(q_t, kc, vc_t, ks, vs_t, kw, vw_t, small_t, bsz, seq_len, t):
    g, r, hd = NSA_KV_GROUPS, NSA_GROUP_SIZE, HEAD_DIM
    assert seq_len // SLC_BLOCK <= MAX_SLC_BLOCKS
    nb = seq_len // CMP_STRIDE
    nc = seq_len // t
    rows = r * t
    ovl = _overlap_t(seq_len)
    k_spec = pl.BlockSpec((1, seq_len, LANE), lambda b, gi, i: (b, 0, gi))
    v_spec = pl.BlockSpec((1, nc, hd, t), lambda b, gi, i: (b, 0, gi, 0))
    return pl.pallas_call(
        functools.partial(_nsa_kernel, t=t, seq_len=seq_len),
        grid=(bsz, g, nc),
        in_specs=[pl.BlockSpec((1, 1, r * hd, t), lambda b, gi, i: (b, i, gi, 0)),
                  pl.BlockSpec((1, nb, LANE), lambda b, gi, i: (b, 0, gi)),
                  pl.BlockSpec((1, hd, nb), lambda b, gi, i: (b, gi, 0)),
                  k_spec, v_spec, k_spec, v_spec,
                  pl.BlockSpec((1, 1, SMALL_W, t), lambda b, gi, i: (b, i, 0, 0)),
                  _const_spec(ovl.shape)],
        out_specs=pl.BlockSpec((1, t, r * hd), lambda b, gi, i: (b, i, gi)),
        out_shape=jax.ShapeDtypeStruct((bsz, seq_len, Q_WIDTH), BF16),
        scratch_shapes=[pltpu.VMEM((LANE, rows), BF16), pltpu.VMEM((nc, t, rows), F32),
                        pltpu.VMEM((-(-WINDOW // t) + 1, t, rows), F32),
                        pltpu.VMEM((hd + ONES_ROWS, rows), F32)],
        compiler_params=_params("parallel", "parallel", "parallel"),
        name="nsa_attn",
    )(q_t.reshape(bsz, nc, Q_WIDTH, t), kc, vc_t, ks.reshape(bsz, seq_len, -1), vs_t.reshape(bsz, nc, g * hd, t),
      kw.reshape(bsz, seq_len, -1), vw_t.reshape(bsz, nc, g * hd, t), small_t.reshape(bsz, nc, SMALL_W, t), ovl)


def _dsa_kernel(q_ref, k_ref, v_ref, iq_ref, ik_ref, sm_ref, o_ref,
                qm_ref, iqm_ref, sc_ref, sc16_ref, s_ref, acc_ref, kth_ref, *, t, seq_len, k_top):
    i = pl.program_id(1)
    t0 = i * t
    nck = i + 1
    hd = HEAD_DIM
    tok = t0 + lax.broadcasted_iota(jnp.int32, (1, t), 1)

    zeros = jnp.zeros((hd, t), BF16)
    for h in range(Q_HEADS):
        qh = q_ref[0, 0, h * hd:(h + 1) * hd, :]
        qm_ref[h] = jnp.concatenate([qh, zeros] if h % 2 == 0 else [zeros, qh], axis=0)
        iqm_ref[h] = jnp.concatenate([iq_ref[0, 0, h * hd:(h + 1) * hd, :], zeros], axis=0)
    small = sm_ref[0, 0]

    def key_pos(c):
        start = c * t if isinstance(c, int) else pl.multiple_of(c * t, t)
        return start + lax.broadcasted_iota(jnp.int32, (t, 1), 0)

    def score_body(c, carry):
        ik = ik_ref[0, pl.ds(pl.multiple_of(c * t, t), t), :]
        sc = jnp.zeros((t, t), F32)
        for h in range(IDX_HEADS):
            sc = sc + jnp.maximum(_dot(ik, iqm_ref[h]), 0.0) * small[N_GATE + h:N_GATE + h + 1]
        sc = jnp.where(key_pos(c) <= tok, sc, -jnp.inf)
        sc_ref[c] = sc
        sc16_ref[c] = sc.astype(BF16)
        return carry

    _loop_pairs(nck, score_body, 0, DSA_TILES_PER_TRIP)

    def count(indicator):
        def body(c, acc):
            return acc + _fold(indicator(sc_ref[c], key_pos(c)), jnp.add)
        acc = _loop_pairs(nck, body, jnp.zeros((SUBLANE, t), F32))
        return jnp.sum(acc, axis=0, keepdims=True)

    def count_tiles(tiles_ref, n_tiles, indicator, group):
        acc = None
        for c in range(n_tiles):
            part = _fold(indicator(tiles_ref[c]), jnp.add, group)
            acc = part if acc is None else acc + part
        return acc

    kf = float(k_top)
    zero = jnp.zeros((1, t), jnp.int32)

    def decode(code):
        return lax.bitcast_convert_type(jnp.where(code < 0, code ^ jnp.int32(0x7FFFFFFF), code), F32)

    def search(n_tiles):
        one16, zero16 = jnp.ones((), BF16), jnp.zeros((), BF16)

        def decode16(code16):
            pattern = jnp.where(code16 < 0, code16 ^ jnp.int32(0x7FFF), code16)
            return lax.bitcast_convert_type(lax.shift_left(pattern, 16), F32)

        def reached16(code16):
            val = decode16(code16).astype(BF16)
            acc = count_tiles(sc16_ref, n_tiles, lambda sc: jnp.where(sc >= val, one16, zero16), BF16_ROWS)
            return (jnp.sum(acc.astype(F32), axis=0, keepdims=True) >= kf) | (code16 <= KEY16_NEG_INF)

        def bit16_body(b, base16):
            cand = base16 + lax.shift_left(jnp.int32(1), 14 - b)
            return jnp.where(reached16(cand), cand, base16)

        base16 = jnp.where(reached16(zero), zero, jnp.full((1, t), -(2 ** 15), jnp.int32))
        below = lax.fori_loop(0, 15, bit16_body, base16) - 1
        lo = lax.shift_left(below, 16) + jnp.where(below < 0, 0xFFFF, 0)

        def count32(indicator):
            return jnp.sum(count_tiles(sc_ref, n_tiles, indicator, SUBLANE), axis=0, keepdims=True)

        def reached(code):
            val = decode(code)
            return (count32(lambda sc: jnp.where(sc >= val, 1.0, 0.0)) >= kf) | (code <= KEY_NEG_INF)

        def bit_body(b, base):
            cand = base + lax.shift_left(jnp.int32(1), 16 - b)
            return jnp.where(reached(cand), cand, base)

        kth = decode(jnp.maximum(lax.fori_loop(0, 17, bit_body, lo), KEY_NEG_INF))
        return (kth, count32(lambda sc: jnp.where(sc >= kth, 1.0, 0.0)),
                count32(lambda sc: jnp.where(sc > kth, 1.0, 0.0)))

    for n_tiles in range(1, seq_len // t + 1):
        @pl.when(nck == n_tiles)
        def _(n_tiles=n_tiles):
            if n_tiles * t <= k_top:
                found = (jnp.full((1, t), -jnp.inf, F32), jnp.full((1, t), kf, F32), jnp.zeros((1, t), F32))
            else:
                found = search(n_tiles)
            for r, v in enumerate(found):
                kth_ref[r:r + 1, :] = v

    kth = kth_ref[0:1, :]
    n_ge, n_gt = kth_ref[1:2, :], kth_ref[2:3, :]

    tie = (n_ge > kf) & (kth > -jnp.inf)

    def tie_cut():
        room = kf - n_gt
        top_bit = seq_len.bit_length() - 1

        def pos_body(b, x):
            cand = x + lax.shift_left(jnp.int32(1), top_bit - b)
            n = count(lambda sc, pos: jnp.where(sc == kth, jnp.where(pos < cand, 1.0, 0.0), 0.0))
            return jnp.where(n <= room, cand, x)

        x = lax.fori_loop(0, top_bit + 1, pos_body, zero)
        return jnp.where(tie, x, seq_len)

    cut = lax.cond(jnp.max(jnp.where(tie, 1.0, 0.0)) > 0.0, tie_cut,
                   lambda: jnp.full((1, t), seq_len, jnp.int32))
    cut = jnp.where(kth > -jnp.inf, cut, 0)

    def att_scores(c, mx):
        k0 = pl.multiple_of(c * t, t)
        sc = sc_ref[c]
        bias = jnp.where(sc > kth, 0.0, jnp.where(sc == kth, jnp.where(key_pos(c) < cut, 0.0, NEG), NEG))
        out = []
        for h in range(Q_HEADS):
            s = _dot(k_ref[0, pl.ds(k0, t), (h // 2) * LANE:(h // 2 + 1) * LANE], qm_ref[h]) + bias
            s_ref[h, c] = s
            out.append(jnp.maximum(mx[h], _fold(s, jnp.maximum)))
        return tuple(out)

    mx = _loop_pairs(nck, att_scores, (jnp.full((SUBLANE, t), NEG, F32),) * Q_HEADS, DSA_TILES_PER_TRIP)
    m = [jnp.max(mx[h], axis=0, keepdims=True) for h in range(Q_HEADS)]

    acc_ref[...] = jnp.zeros(acc_ref.shape, F32)

    def att_values(c, carry):
        for h in range(Q_HEADS):
            acc_ref[h] += _dot(_with_ones(v_ref[0, c, h * hd:(h + 1) * hd, :]), _weights(s_ref[h, c], m[h]))
        return carry

    _loop_pairs(nck, att_values, 0, DSA_TILES_PER_TRIP)
    out = jnp.concatenate([acc_ref[h, :hd] * (1.0 / acc_ref[h, hd:hd + 1])
                           for h in range(Q_HEADS)], axis=0)
    o_ref[0] = out.T.astype(o_ref.dtype)


def _dsa(q_t, k, v_t, iq_t, ik, small_t, bsz, seq_len, t):
    k_top = min(DSA_TOPK_MAX, seq_len // 4)
    assert t >= k_top
    nc = seq_len // t
    qt_spec = pl.BlockSpec((1, 1, Q_WIDTH, t), lambda b, i: (b, i, 0, 0))
    return pl.pallas_call(
        functools.partial(_dsa_kernel, t=t, seq_len=seq_len, k_top=k_top),
        grid=(bsz, nc),
        in_specs=[qt_spec,
                  pl.BlockSpec((1, seq_len, Q_WIDTH), lambda b, i: (b, 0, 0)),
                  pl.BlockSpec((1, nc, Q_WIDTH, t), lambda b, i: (b, 0, 0, 0)),
                  qt_spec,
                  pl.BlockSpec((1, seq_len, LANE), lambda b, i: (b, 0, 0)),
                  pl.BlockSpec((1, 1, SMALL_W, t), lambda b, i: (b, i, 0, 0))],
        out_specs=pl.BlockSpec((1, t, Q_WIDTH), lambda b, i: (b, i, 0)),
        out_shape=jax.ShapeDtypeStruct((bsz, seq_len, Q_WIDTH), BF16),
        scratch_shapes=[pltpu.VMEM((Q_HEADS, LANE, t), BF16), pltpu.VMEM((IDX_HEADS, LANE, t), BF16),
                        pltpu.VMEM((nc, t, t), F32), pltpu.VMEM((nc, t, t), BF16),
                        pltpu.VMEM((Q_HEADS, nc, t, t), F32),
                        pltpu.VMEM((Q_HEADS, HEAD_DIM + ONES_ROWS, t), F32), pltpu.VMEM((SUBLANE, t), F32)],
        compiler_params=_params("parallel", "parallel"),
        name="dsa_attn",
    )(q_t.reshape(bsz, nc, Q_WIDTH, t), k.reshape(bsz, seq_len, -1), v_t.reshape(bsz, nc, Q_WIDTH, t),
      iq_t.reshape(bsz, nc, Q_WIDTH, t), ik.reshape(bsz, seq_len, -1), small_t.reshape(bsz, nc, SMALL_W, t))


def _merge_ffn_kernel(x_ref, yn_ref, yd_ref, gn_ref, gd_ref, wbn_ref, wbd_ref, wo_ref,
                      g_ref, wg_ref, wu_ref, wd_ref, fg_ref, o_ref, *, final):
    merged = gn_ref[...] * _dot(yn_ref[...], wbn_ref[...]) + gd_ref[...] * _dot(yd_ref[...], wbd_ref[...])
    x = x_ref[...] + _dot(merged.astype(BF16), wo_ref[...])
    h = _rmsnorm(x, g_ref[...]).astype(BF16)
    z = (jax.nn.silu(_dot(h, wg_ref[...])) * _dot(h, wu_ref[...])).astype(BF16)
    y = x + _dot(z, wd_ref[...])
    o_ref[...] = _rmsnorm(y, fg_ref[...]) if final else y


def _merge_ffn(xf, y_nsa, y_dsa, gate_nsa, gate_dsa, wbn, wbd, wo, g, wg, wu, wd, final_g, final, tm):
    t, d = xf.shape
    f = wg.shape[1]
    wn = y_nsa.shape[1]
    row = lambda i: (i, 0)
    return pl.pallas_call(
        functools.partial(_merge_ffn_kernel, final=final),
        grid=(t // tm,),
        in_specs=[pl.BlockSpec((tm, d), row), pl.BlockSpec((tm, wn), row), pl.BlockSpec((tm, wn), row),
                  pl.BlockSpec((tm, d), row), pl.BlockSpec((tm, d), row),
                  _const_spec((wn, d)), _const_spec((wn, d)), _const_spec((d, d)),
                  _const_spec((1, d)), _const_spec((d, f)), _const_spec((d, f)), _const_spec((f, d)),
                  _const_spec((1, d))],
        out_specs=pl.BlockSpec((tm, d), row),
        out_shape=jax.ShapeDtypeStruct((t, d), F32),
        compiler_params=_params("parallel"),
        name="merge_ffn",
    )(xf, y_nsa, y_dsa, gate_nsa, gate_dsa, wbn.astype(BF16), wbd.astype(BF16), wo.astype(BF16),
      g.reshape(1, d), wg.astype(BF16), wu.astype(BF16), wd.astype(BF16), final_g.reshape(1, d))


def _rope_tables(seq_len, tm):
    pos = jnp.arange(seq_len, dtype=F32)
    inv_freq = ROPE_THETA ** (-jnp.arange(0, ROPE_DIM, 2, dtype=F32) / ROPE_DIM)
    ang = pos[:, None] * inv_freq[None, :]
    cos, sin = jnp.cos(ang), jnp.sin(ang)
    rest = HEAD_DIM - ROPE_DIM
    one, zero = jnp.ones((seq_len, rest), F32), jnp.zeros((seq_len, rest), F32)
    zh = jnp.zeros((seq_len, ROPE_HALF), F32)
    two = lambda a: jnp.concatenate([a, a], axis=1)
    tile_t = lambda a: a.reshape(seq_len // tm, tm, ROPE_HALF).transpose(0, 2, 1)
    return (two(jnp.concatenate([cos, cos, one], axis=1)),
            two(jnp.concatenate([zh, sin, zero], axis=1)),
            two(jnp.concatenate([-sin, zh, zero], axis=1)),
            tile_t(cos), tile_t(sin))


def _tile(seq_len):
    return min(256, seq_len)


def kernel(x, attn_norm, w_in, cmpk_pos, cmpk_w1, cmpk_b1, cmpk_w2, cmpk_b2, cmpv_pos, cmpv_w1, cmpv_b1,
           cmpv_w2, cmpv_b2, w_branch_nsa, w_branch_dsa, w_out, ffn_norm, w_ffn_gate, w_ffn_up, w_ffn_down,
           final_norm):
    bsz, seq_len, d = x.shape
    depth = w_in.shape[0]
    t = _tile(seq_len)
    tables = _rope_tables(seq_len, t)
    xf = x.reshape(bsz * seq_len, d)
    for l in range(depth):
        (q_nsa, k_cmp, v_cmp, k_slc, v_slc, k_win, v_win, q_dsa, k_dsa, v_dsa, q_idx, k_idx,
         gate_nsa, gate_dsa, small) = _proj(xf, attn_norm[l], *_split_w_in(w_in[l]), tables, seq_len, t)
        kc, vc = _compress(k_cmp, v_cmp, cmpk_pos[l], cmpk_w1[l], cmpk_b1[l], cmpk_w2[l], cmpk_b2[l],
                           cmpv_pos[l], cmpv_w1[l], cmpv_b1[l], cmpv_w2[l], cmpv_b2[l], bsz, seq_len)
        y_nsa = _nsa(q_nsa, kc, vc, k_slc, v_slc, k_win, v_win, small, bsz, seq_len, t)
        y_dsa = _dsa(q_dsa, k_dsa, v_dsa, q_idx, k_idx, small, bsz, seq_len, t)
        xf = _merge_ffn(xf, y_nsa.reshape(bsz * seq_len, -1), y_dsa.reshape(bsz * seq_len, -1),
                        gate_nsa, gate_dsa, w_branch_nsa[l], w_branch_dsa[l], w_out[l],
                        ffn_norm[l], w_ffn_gate[l], w_ffn_up[l], w_ffn_down[l], final_norm,
                        final=(l == depth - 1), tm=t)
    return xf.reshape(bsz, seq_len, d)
```

```python
import functools

import numpy as np
import jax
import jax.numpy as jnp
from jax import lax
from jax.experimental import pallas as pl
from jax.experimental.pallas import tpu as pltpu

HEAD_DIM = 64
ROPE_DIM = HEAD_DIM // 4
ROPE_HALF = ROPE_DIM // 2
ROPE_THETA = 500000.0
NORM_EPS = 1e-6
ATTN_SCALE = HEAD_DIM ** -0.5
Q_SCALE = ATTN_SCALE * float(np.log2(np.e))
Q_HEADS = 8
Q_WIDTH = Q_HEADS * HEAD_DIM
NSA_KV_GROUPS = 2
NSA_GROUP_SIZE = Q_HEADS // NSA_KV_GROUPS
CMP_BLOCK = 32
CMP_STRIDE = 16
CMP_HIDDEN = 2 * HEAD_DIM
SLC_BLOCK = 64
N_SLC = 16
N_LOCAL_BLOCKS = 2
SEL_FORCE_SCORE = 1e4
WINDOW = 512
IDX_HEADS = 8
IDX_DIM = 64
IDX_WEIGHT_SCALE = (IDX_HEADS * IDX_DIM) ** -0.5
DSA_TOPK_MAX = 256
N_GATE = 3 * Q_HEADS
SMALL_W = N_GATE + IDX_HEADS
MAX_SLC_BLOCKS = 32

LANE = 128
SUBLANE = 8
ONES_ROWS = 16
V7X_VMEM_LIMIT = 56 * 1024 * 1024

NEG = -1e30
INT_MIN = -(2 ** 31)
KEY_NEG_INF = -2139095041
KEY16_NEG_INF = -32641
BF16_ROWS = 16
DSA_TILES_PER_TRIP = 4
NSA_TILES_PER_TRIP = 4
BF16 = jnp.bfloat16
F32 = jnp.float32

_NT = (((1,), (1,)), ((), ()))


def _dot(a, b):
    return jnp.dot(a, b, preferred_element_type=F32)


def _dot_nt(a, b):
    return lax.dot_general(a, b, _NT, preferred_element_type=F32)


def _const_spec(shape):
    nd = len(shape)
    return pl.BlockSpec(shape, lambda *_: (0,) * nd, pipeline_mode=pl.Buffered(1))


def _rmsnorm(x, g):
    ms = jnp.mean(x * x, axis=-1, keepdims=True)
    return x * lax.rsqrt(ms + NORM_EPS) * g


def _lane_tile(a, reps):
    return a if reps == 1 else jnp.concatenate([a] * reps, axis=1)


def _params(*sem):
    return pltpu.CompilerParams(dimension_semantics=sem, vmem_limit_bytes=V7X_VMEM_LIMIT)


_PROJ_SEGS = (
    ("t", Q_WIDTH, "ropeq", BF16),
    ("n", 128, "rope", F32),
    ("n", 128, "plain", F32),
    ("n", 256, "rope_onehot", BF16),
    ("t", 128, "plain", BF16),
    ("n", 256, "rope", BF16),
    ("t", 128, "plain", BF16),
    ("t", Q_WIDTH, "ropeq", BF16),
    ("n", Q_WIDTH, "rope", BF16),
    ("t", Q_WIDTH, "plain", BF16),
    ("t", IDX_HEADS * IDX_DIM, "rope", BF16),
    ("n", 128, "rope", BF16),
    ("n", 1024, "sig", F32),
    ("n", 1024, "sig", F32),
    ("t", SMALL_W, "small", F32),
)
_PROJ_NW = sum(s[1] for s in _PROJ_SEGS if s[0] == "n")
_PROJ_TW = sum(s[1] for s in _PROJ_SEGS if s[0] == "t")
_PROJ_CHUNK = 512


def _rope_n(y, c, sa, sb):
    pieces = []
    for j in range(y.shape[1] // LANE):
        yj = y[:, j * LANE:(j + 1) * LANE]
        pieces.append(yj * c + pltpu.roll(yj, ROPE_HALF, 1) * sa + pltpu.roll(yj, LANE - ROPE_HALF, 1) * sb)
    return pieces[0] if len(pieces) == 1 else jnp.concatenate(pieces, axis=1)


def _rope_t(y, cos, sin):
    pieces = []
    for h in range(y.shape[0] // HEAD_DIM):
        r0 = h * HEAD_DIM
        x1, x2 = y[r0:r0 + ROPE_HALF], y[r0 + ROPE_HALF:r0 + ROPE_DIM]
        pieces += [x1 * cos - x2 * sin, x2 * cos + x1 * sin, y[r0 + ROPE_DIM:r0 + HEAD_DIM]]
    return jnp.concatenate(pieces, axis=0)


def _proj_kernel(x_ref, g_ref, wn_ref, wt_ref, c_ref, sa_ref, sb_ref, ct_ref, st_ref, *out_refs, npos):
    tm = x_ref.shape[0]
    h = _rmsnorm(x_ref[...], g_ref[...]).astype(BF16)
    c, sa, sb = c_ref[...], sa_ref[...], sb_ref[...]
    cos_t, sin_t = ct_ref[0], st_ref[0]
    ncol = trow = 0
    for (layout, width, kind, _), o_ref in zip(_PROJ_SEGS, out_refs):
        if layout == "t":
            y = _dot_nt(wt_ref[trow:trow + width, :], h)
            trow += width
            if kind in ("rope", "ropeq"):
                y = _rope_t(y, cos_t, sin_t)
            if kind == "ropeq":
                y = y * Q_SCALE
            if kind == "small":
                row = lax.broadcasted_iota(jnp.int32, y.shape, 0)
                y = jnp.where(row < N_GATE, jax.nn.sigmoid(y), y * IDX_WEIGHT_SCALE)
            o_ref[0] = y.astype(o_ref.dtype)
            continue
        for c0 in range(0, width, _PROJ_CHUNK):
            cw = min(_PROJ_CHUNK, width - c0)
            y = _dot(h, wn_ref[:, ncol + c0:ncol + c0 + cw])
            if kind in ("rope", "rope_onehot"):
                y = _rope_n(y, c, sa, sb)
            if kind == "rope_onehot":
                pos = (pl.program_id(0) % npos) * tm + lax.broadcasted_iota(jnp.int32, y.shape, 0)
                lane = lax.broadcasted_iota(jnp.int32, y.shape, 1) & (LANE - 1)
                y = jnp.where(lane == HEAD_DIM + pos // SLC_BLOCK, 1.0, y)
            if kind == "sig":
                y = jax.nn.sigmoid(y)
            o_ref[:, c0:c0 + cw] = y.astype(o_ref.dtype)
        ncol += width


def _split_w_in(w):
    d = w.shape[0]
    kvw = NSA_KV_GROUPS * HEAD_DIM
    sizes = (Q_WIDTH, kvw, kvw, kvw, kvw, kvw, kvw, N_GATE, Q_WIDTH, Q_WIDTH, Q_WIDTH,
             IDX_HEADS * IDX_DIM, IDX_DIM, IDX_HEADS, d, d)
    assert sum(sizes) == w.shape[1]
    offs = np.concatenate([[0], np.cumsum(sizes)])
    (nsa_q, k_cmp, v_cmp, k_slc, v_slc, k_win, v_win, nsa_gate, dsa_q, dsa_k, dsa_v,
     idx_q, idx_k, idx_w, gate_nsa, gate_dsa) = [w[:, offs[i]:offs[i + 1]] for i in range(16)]

    def pad_groups(a):
        a = a.reshape(d, NSA_KV_GROUPS, HEAD_DIM)
        return jnp.concatenate([a, jnp.zeros_like(a)], axis=2).reshape(d, NSA_KV_GROUPS * LANE)

    idx_k_pad = jnp.concatenate([idx_k, jnp.zeros_like(idx_k)], axis=1)
    w_n = jnp.concatenate([k_cmp, v_cmp, pad_groups(k_slc), pad_groups(k_win), dsa_k, idx_k_pad,
                           gate_nsa, gate_dsa], axis=1)
    w_t = jnp.concatenate([nsa_q, v_slc, v_win, dsa_q, dsa_v, idx_q, nsa_gate, idx_w], axis=1).T
    assert w_n.shape[1] == _PROJ_NW and w_t.shape[0] == _PROJ_TW
    return w_n.astype(BF16), w_t.astype(BF16)


def _proj(xf, g, w_n, w_t, tables, seq_len, tm):
    t, d = xf.shape
    npos = seq_len // tm
    row = lambda i: (i, 0)
    tab_n = pl.BlockSpec((tm, LANE), lambda i: (i % npos, 0))
    tab_t = pl.BlockSpec((1, ROPE_HALF, tm), lambda i: (i % npos, 0, 0))
    out_specs, out_shape = [], []
    for layout, width, _, dt in _PROJ_SEGS:
        if layout == "t":
            out_specs.append(pl.BlockSpec((1, width, tm), lambda i: (i, 0, 0)))
            out_shape.append(jax.ShapeDtypeStruct((t // tm, width, tm), dt))
        else:
            out_specs.append(pl.BlockSpec((tm, width), row))
            out_shape.append(jax.ShapeDtypeStruct((t, width), dt))
    return pl.pallas_call(
        functools.partial(_proj_kernel, npos=npos),
        grid=(t // tm,),
        in_specs=[pl.BlockSpec((tm, d), row), _const_spec((1, d)), _const_spec((d, _PROJ_NW)),
                  _const_spec((_PROJ_TW, d)), tab_n, tab_n, tab_n, tab_t, tab_t],
        out_specs=out_specs,
        out_shape=out_shape,
        compiler_params=_params("parallel"),
        name="proj",
    )(xf, g.reshape(1, d), w_n, w_t, *tables)


def _cmp_kernel(k_ref, v_ref, pk_ref, w1k_ref, b1k_ref, w2k_ref, b2k_ref,
                pv_ref, w1v_ref, b1v_ref, w2v_ref, b2v_ref, kc_ref, vc_ref, *, n_cmp):
    nb = kc_ref.shape[1]

    def hidden(x_ref, pos_ref, w1_ref, b1_ref):
        halves = []
        for half in range(CMP_BLOCK // CMP_STRIDE):
            acc = None
            for l in range(CMP_STRIDE):
                j = half * CMP_STRIDE + l
                x = x_ref[0, pl.ds(l, nb, stride=CMP_STRIDE), :] + pos_ref[j:j + 1, :]
                part = _dot(x.astype(BF16), w1_ref[j])
                acc = part if acc is None else acc + part
            halves.append(acc)
        return jax.nn.gelu(halves[0] + pltpu.roll(halves[1], nb - 1, 0) + b1_ref[...]).astype(BF16)

    kc = _dot(hidden(k_ref, pk_ref, w1k_ref, b1k_ref), w2k_ref[...]) + b2k_ref[...]
    rows = lax.broadcasted_iota(jnp.int32, kc.shape, 0)
    kc_ref[0] = jnp.where(rows < n_cmp, kc, 0.0).astype(kc_ref.dtype)
    vc = _dot_nt(w2v_ref[...], hidden(v_ref, pv_ref, w1v_ref, b1v_ref)) + b2v_ref[...]
    cols = lax.broadcasted_iota(jnp.int32, vc.shape, 1)
    vc_ref[0] = jnp.where(cols < n_cmp, vc, 0.0).astype(vc_ref.dtype)


def _compress(k_cmp, v_cmp, pk, w1k, b1k, w2k, b2k, pv, w1v, b1v, w2v, b2v, bsz, seq_len):
    g, hd = NSA_KV_GROUPS, HEAD_DIM
    assert CMP_BLOCK == 2 * CMP_STRIDE
    nb = seq_len // CMP_STRIDE
    n_cmp = (seq_len - CMP_BLOCK) // CMP_STRIDE + 1
    eye = jnp.eye(g, dtype=F32)
    per_group = lambda m: jnp.kron(eye, m)
    tile_g = lambda a: jnp.concatenate([a] * g, axis=-1)

    def first_layer(pos, w1, b1):
        w1_bd = jax.vmap(per_group)(w1.reshape(CMP_BLOCK, hd, CMP_HIDDEN)).astype(BF16)
        return tile_g(pos), w1_bd, tile_g(b1).reshape(1, -1)

    w2k_pad = jnp.concatenate([w2k, jnp.zeros_like(w2k)], axis=1)
    b2k_pad = jnp.concatenate([b2k, jnp.zeros_like(b2k)])
    k_w = first_layer(pk, w1k, b1k) + (per_group(w2k_pad).astype(BF16), tile_g(b2k_pad).reshape(1, -1))
    v_w = first_layer(pv, w1v, b1v) + (per_group(w2v.T).astype(BF16), tile_g(b2v).reshape(-1, 1))
    seq_spec = pl.BlockSpec((1, seq_len, g * hd), lambda b: (b, 0, 0))
    first = [_const_spec((CMP_BLOCK, g * hd)), _const_spec((CMP_BLOCK, g * hd, g * CMP_HIDDEN)),
             _const_spec((1, g * CMP_HIDDEN))]
    return pl.pallas_call(
        functools.partial(_cmp_kernel, n_cmp=n_cmp),
        grid=(bsz,),
        in_specs=[seq_spec, seq_spec] + first + [_const_spec((g * CMP_HIDDEN, g * LANE)), _const_spec((1, g * LANE))]
        + first + [_const_spec((g * hd, g * CMP_HIDDEN)), _const_spec((g * hd, 1))],
        out_specs=[pl.BlockSpec((1, nb, g * LANE), lambda b: (b, 0, 0)),
                   pl.BlockSpec((1, g * hd, nb), lambda b: (b, 0, 0))],
        out_shape=[jax.ShapeDtypeStruct((bsz, nb, g * LANE), BF16),
                   jax.ShapeDtypeStruct((bsz, g * hd, nb), BF16)],
        compiler_params=_params("parallel"),
        name="compress",
    )(k_cmp.reshape(bsz, seq_len, g * hd), v_cmp.reshape(bsz, seq_len, g * hd), *k_w, *v_w)


def _fold(x, op, group=SUBLANE):
    parts = [x[r * group:(r + 1) * group] for r in range(x.shape[0] // group)]
    while len(parts) > 1:
        pairs = [op(parts[k], parts[k + 1]) for k in range(0, len(parts) - 1, 2)]
        parts = pairs + ([parts[-1]] if len(parts) % 2 else [])
    return parts[0]


def _loop_pairs(n, body, init, per_trip=2):
    def steps(start, count, carry):
        for s in range(count):
            carry = body(start + s, carry)
        return carry

    carry = lax.fori_loop(0, n // per_trip, lambda j, c: steps(j * per_trip, per_trip, c), init)
    group = per_trip // 2
    while group:
        carry = lax.cond(n & group != 0, functools.partial(steps, n - n % (2 * group), group), lambda c: c, carry)
        group //= 2
    return carry


def _weights(s, m):
    return jnp.exp2((s - m).astype(BF16))


def _with_ones(v_t):
    return jnp.concatenate([v_t, jnp.ones((ONES_ROWS, v_t.shape[1]), v_t.dtype)], axis=0)


def _nsa_kernel(q_ref, kc_ref, vc_ref, ks_ref, vs_ref, kw_ref, vw_ref, sm_ref, ovl_ref, o_ref,
                qm_ref, ss_ref, sw_ref, as_ref, *, t, seq_len):
    r_heads = NSA_GROUP_SIZE
    gi = pl.program_id(1)
    i = pl.program_id(2)
    t0 = i * t
    rows = r_heads * t
    hd = HEAD_DIM

    for r in range(r_heads):
        qm_ref[0:hd, r * t:(r + 1) * t] = q_ref[0, 0, r * hd:(r + 1) * hd, :]
    qm_ref[hd:, :] = jnp.zeros((LANE - hd, rows), BF16)
    tok = t0 + lax.broadcasted_iota(jnp.int32, (1, t), 1)
    tok_r = _lane_tile(tok, r_heads)

    nb = kc_ref.shape[1]
    s_c = _dot(kc_ref[0], qm_ref[...])
    c_end = lax.broadcasted_iota(jnp.int32, (nb, 1), 0) * CMP_STRIDE + (CMP_BLOCK - 1)
    mask_c = c_end <= tok_r
    s_c = jnp.where(mask_c, s_c, NEG)
    s_c = s_c - jnp.max(s_c, axis=0, keepdims=True)
    p_c = jnp.where(mask_c, jnp.exp2(s_c), 0.0)
    p_c = (p_c / jnp.maximum(jnp.sum(p_c, axis=0, keepdims=True), 1e-30)).astype(BF16)
    o_c = _dot(vc_ref[0], p_c)
    imp4 = _dot(ovl_ref[...], p_c)
    imp = imp4[:, 0:t]
    for r in range(1, r_heads):
        imp = imp + imp4[:, r * t:(r + 1) * t]

    n_slc = seq_len // SLC_BLOCK
    n_sel = min(N_SLC, n_slc)
    blk = lax.broadcasted_iota(jnp.int32, (MAX_SLC_BLOCKS, 1), 0)
    dist = (tok // SLC_BLOCK) - blk
    forced = (blk == 0) | ((dist >= 0) & (dist < N_LOCAL_BLOCKS))
    score = jnp.where(forced, SEL_FORCE_SCORE, imp)
    score = jnp.where(blk * SLC_BLOCK <= tok, score, -jnp.inf)
    groups = [score[v * SUBLANE:(v + 1) * SUBLANE] for v in range(MAX_SLC_BLOCKS // SUBLANE)]
    ranks = [jnp.zeros((SUBLANE, t), F32) for _ in groups]
    sub = lax.broadcasted_iota(jnp.int32, (SUBLANE, 1), 0)
    for j in range(n_slc):
        row = score[j:j + 1]
        for v, sv in enumerate(groups):
            ge = jnp.where(row >= sv, 1.0, 0.0)
            gt = jnp.where(row > sv, 1.0, 0.0)
            if v * SUBLANE > j:
                beats = ge
            elif (v + 1) * SUBLANE <= j + 1:
                beats = gt
            else:
                beats = jnp.where(sub + v * SUBLANE > j, ge, gt)
            ranks[v] = ranks[v] + beats
    selbias = jnp.where(jnp.concatenate(ranks, axis=0) < n_sel, 0.0, NEG).astype(BF16)
    for r in range(r_heads):
        qm_ref[hd:hd + MAX_SLC_BLOCKS, r * t:(r + 1) * t] = selbias
    key_off = lax.broadcasted_iota(jnp.int32, (t, 1), 0)
    q_off = lax.broadcasted_iota(jnp.int32, (1, t), 1)
    causal = _lane_tile(jnp.where(key_off <= q_off, 0.0, NEG), r_heads)
    neg8 = jnp.full((SUBLANE, rows), NEG, F32)

    def key_tile(k_ref_, c):
        return k_ref_[0, pl.ds(pl.multiple_of(c * t, t), t), :]

    def slc_scores(c, mx):
        s = _dot(key_tile(ks_ref, c), qm_ref[...])
        ss_ref[c] = s
        return jnp.maximum(mx, _fold(s, jnp.maximum))

    mx = _loop_pairs(i, slc_scores, neg8, NSA_TILES_PER_TRIP)
    s = _dot(key_tile(ks_ref, i), qm_ref[...]) + causal
    ss_ref[i] = s
    m_s = jnp.max(jnp.maximum(mx, _fold(s, jnp.maximum)), axis=0, keepdims=True)

    n_back = -(-WINDOW // t)
    win_tiles = []
    mxw = neg8
    for back in range(n_back, -1, -1):
        if back:
            lo_edge = back * t - WINDOW
            bias = _lane_tile(jnp.where(key_off - q_off > jnp.where(i >= back, lo_edge, t), 0.0, NEG), r_heads)
        else:
            bias = causal
        c = jnp.maximum(i - back, 0)
        s = _dot(key_tile(kw_ref, c), qm_ref[...]) + bias
        sw_ref[n_back - back] = s
        mxw = jnp.maximum(mxw, _fold(s, jnp.maximum))
        win_tiles.append(c)
    m_w = jnp.max(mxw, axis=0, keepdims=True)

    as_ref[...] = jnp.zeros(as_ref.shape, F32)

    def slc_values(c, carry):
        as_ref[...] += _dot(_with_ones(vs_ref[0, c]), _weights(ss_ref[c], m_s))
        return carry

    _loop_pairs(i + 1, slc_values, 0, NSA_TILES_PER_TRIP)
    o_s = as_ref[:hd] * (1.0 / as_ref[hd:hd + 1])

    o_w = jnp.zeros(as_ref.shape, F32)
    for n, c in enumerate(win_tiles):
        o_w = o_w + _dot(_with_ones(vw_ref[0, c]), _weights(sw_ref[n], m_w))
    o_w = o_w[:hd] * (1.0 / o_w[hd:hd + 1])

    gates = sm_ref[0, 0]

    def gate(branch, r):
        rows_g = [gates[branch * Q_HEADS + g * r_heads + r:branch * Q_HEADS + g * r_heads + r + 1]
                  for g in range(NSA_KV_GROUPS)]
        out = rows_g[0]
        for g in range(1, NSA_KV_GROUPS):
            out = jnp.where(gi == g, rows_g[g], out)
        return out

    heads = []
    for r in range(r_heads):
        sl = slice(r * t, (r + 1) * t)
        heads.append(gate(0, r) * o_c[:, sl] + gate(1, r) * o_s[:, sl] + gate(2, r) * o_w[:, sl])
    o_ref[0] = jnp.concatenate(heads, axis=0).T.astype(o_ref.dtype)


def _overlap_t(seq_len):
    nb = seq_len // CMP_STRIDE
    n_cmp = (seq_len - CMP_BLOCK) // CMP_STRIDE + 1
    n_slc = seq_len // SLC_BLOCK
    c_start = np.arange(n_cmp)[None, :] * CMP_STRIDE
    s_start = np.arange(n_slc)[:, None] * SLC_BLOCK
    ov = np.minimum(c_start + CMP_BLOCK, s_start + SLC_BLOCK) - np.maximum(c_start, s_start)
    ovl = np.zeros((MAX_SLC_BLOCKS, nb), np.float32)
    ovl[:n_slc, :n_cmp] = np.clip(ov, 0, None) / CMP_BLOCK
    return jnp.asarray(ovl, BF16)


def _nsa(q_t, kc, vc_t, ks, vs_t, kw, vw_t, small_t, bsz, seq_len, t):
    g, r, hd = NSA_KV_GROUPS, NSA_GROUP_SIZE, HEAD_DIM
    assert seq_len // SLC_BLOCK <= MAX_SLC_BLOCKS
    nb = seq_len // CMP_STRIDE
    nc = seq_len // t
    rows = r * t
    ovl = _overlap_t(seq_len)
    k_spec = pl.BlockSpec((1, seq_len, LANE), lambda b, gi, i: (b, 0, gi))
    v_spec = pl.BlockSpec((1, nc, hd, t), lambda b, gi, i: (b, 0, gi, 0))
    return pl.pallas_call(
        functools.partial(_nsa_kernel, t=t, seq_len=seq_len),
        grid=(bsz, g, nc),
        in_specs=[pl.BlockSpec((1, 1, r * hd, t), lambda b, gi, i: (b, i, gi, 0)),
                  pl.BlockSpec((1, nb, LANE), lambda b, gi, i: (b, 0, gi)),
                  pl.BlockSpec((1, hd, nb), lambda b, gi, i: (b, gi, 0)),
                  k_spec, v_spec, k_spec, v_spec,
                  pl.BlockSpec((1, 1, SMALL_W, t), lambda b, gi, i: (b, i, 0, 0)),
                  _const_spec(ovl.shape)],
        out_specs=pl.BlockSpec((1, t, r * hd), lambda b, gi, i: (b, i, gi)),
        out_shape=jax.ShapeDtypeStruct((bsz, seq_len, Q_WIDTH), BF16),
        scratch_shapes=[pltpu.VMEM((LANE, rows), BF16), pltpu.VMEM((nc, t, rows), F32),
                        pltpu.VMEM((-(-WINDOW // t) + 1, t, rows), F32),
                        pltpu.VMEM((hd + ONES_ROWS, rows), F32)],
        compiler_params=_params("parallel", "parallel", "parallel"),
        name="nsa_attn",
    )(q_t.reshape(bsz, nc, Q_WIDTH, t), kc, vc_t, ks.reshape(bsz, seq_len, -1), vs_t.reshape(bsz, nc, g * hd, t),
      kw.reshape(bsz, seq_len, -1), vw_t.reshape(bsz, nc, g * hd, t), small_t.reshape(bsz, nc, SMALL_W, t), ovl)


def _dsa_kernel(q_ref, k_ref, v_ref, iq_ref, ik_ref, sm_ref, o_ref,
                qm_ref, iqm_ref, sc_ref, sc16_ref, s_ref, acc_ref, kth_ref, *, t, seq_len, k_top):
    i = pl.program_id(1)
    t0 = i * t
    nck = i + 1
    hd = HEAD_DIM
    tok = t0 + lax.broadcasted_iota(jnp.int32, (1, t), 1)

    zeros = jnp.zeros((hd, t), BF16)
    for h in range(Q_HEADS):
        qh = q_ref[0, 0, h * hd:(h + 1) * hd, :]
        qm_ref[h] = jnp.concatenate([qh, zeros] if h % 2 == 0 else [zeros, qh], axis=0)
        iqm_ref[h] = jnp.concatenate([iq_ref[0, 0, h * hd:(h + 1) * hd, :], zeros], axis=0)
    small = sm_ref[0, 0]

    def key_pos(c):
        start = c * t if isinstance(c, int) else pl.multiple_of(c * t, t)
        return start + lax.broadcasted_iota(jnp.int32, (t, 1), 0)

    def score_body(c, carry):
        ik = ik_ref[0, pl.ds(pl.multiple_of(c * t, t), t), :]
        sc = jnp.zeros((t, t), F32)
        for h in range(IDX_HEADS):
            sc = sc + jnp.maximum(_dot(ik, iqm_ref[h]), 0.0) * small[N_GATE + h:N_GATE + h + 1]
        sc = jnp.where(key_pos(c) <= tok, sc, -jnp.inf)
        sc_ref[c] = sc
        sc16_ref[c] = sc.astype(BF16)
        return carry

    _loop_pairs(nck, score_body, 0, DSA_TILES_PER_TRIP)

    def count(indicator):
        def body(c, acc):
            return acc + _fold(indicator(sc_ref[c], key_pos(c)), jnp.add)
        acc = _loop_pairs(nck, body, jnp.zeros((SUBLANE, t), F32))
        return jnp.sum(acc, axis=0, keepdims=True)

    def count_tiles(tiles_ref, n_tiles, indicator, group):
        acc = None
        for c in range(n_tiles):
            part = _fold(indicator(tiles_ref[c]), jnp.add, group)
            acc = part if acc is None else acc + part
        return acc

    kf = float(k_top)
    zero = jnp.zeros((1, t), jnp.int32)

    def decode(code):
        return lax.bitcast_convert_type(jnp.where(code < 0, code ^ jnp.int32(0x7FFFFFFF), code), F32)

    def search(n_tiles):
        one16, zero16 = jnp.ones((), BF16), jnp.zeros((), BF16)

        def decode16(code16):
            pattern = jnp.where(code16 < 0, code16 ^ jnp.int32(0x7FFF), code16)
            return lax.bitcast_convert_type(lax.shift_left(pattern, 16), F32)

        def reached16(code16):
            val = decode16(code16).astype(BF16)
            acc = count_tiles(sc16_ref, n_tiles, lambda sc: jnp.where(sc >= val, one16, zero16), BF16_ROWS)
            return (jnp.sum(acc.astype(F32), axis=0, keepdims=True) >= kf) | (code16 <= KEY16_NEG_INF)

        def bit16_body(b, base16):
            cand = base16 + lax.shift_left(jnp.int32(1), 14 - b)
            return jnp.where(reached16(cand), cand, base16)

        base16 = jnp.where(reached16(zero), zero, jnp.full((1, t), -(2 ** 15), jnp.int32))
        below = lax.fori_loop(0, 15, bit16_body, base16) - 1
        lo = lax.shift_left(below, 16) + jnp.where(below < 0, 0xFFFF, 0)

        def count32(indicator):
            return jnp.sum(count_tiles(sc_ref, n_tiles, indicator, SUBLANE), axis=0, keepdims=True)

        def reached(code):
            val = decode(code)
            return (count32(lambda sc: jnp.where(sc >= val, 1.0, 0.0)) >= kf) | (code <= KEY_NEG_INF)

        def bit_body(b, base):
            cand = base + lax.shift_left(jnp.int32(1), 16 - b)
            return jnp.where(reached(cand), cand, base)

        kth = decode(jnp.maximum(lax.fori_loop(0, 17, bit_body, lo), KEY_NEG_INF))
        return kth, count32(lambda sc: jnp.where(sc >= kth, 1.0, 0.0))

    for n_tiles in range(1, seq_len // t + 1):
        @pl.when(nck == n_tiles)
        def _(n_tiles=n_tiles):
            if n_tiles * t <= k_top:
                found = (jnp.full((1, t), -jnp.inf, F32), jnp.full((1, t), kf, F32))
            else:
                found = search(n_tiles)
            for r, v in enumerate(found):
                kth_ref[r:r + 1, :] = v

    kth = kth_ref[0:1, :]
    n_ge = kth_ref[1:2, :]

    tie = (n_ge > kf) & (kth > -jnp.inf)

    def tie_cut():
        room = kf - count(lambda sc, pos: jnp.where(sc > kth, 1.0, 0.0))
        top_bit = seq_len.bit_length() - 1

        def pos_body(b, x):
            cand = x + lax.shift_left(jnp.int32(1), top_bit - b)
            n = count(lambda sc, pos: jnp.where(sc == kth, jnp.where(pos < cand, 1.0, 0.0), 0.0))
            return jnp.where(n <= room, cand, x)

        x = lax.fori_loop(0, top_bit + 1, pos_body, zero)
        return jnp.where(tie, x, seq_len)

    cut = lax.cond(jnp.max(jnp.where(tie, 1.0, 0.0)) > 0.0, tie_cut,
                   lambda: jnp.full((1, t), seq_len, jnp.int32))
    cut = jnp.where(kth > -jnp.inf, cut, 0)

    def att_scores(c, mx):
        k0 = pl.multiple_of(c * t, t)
        sc = sc_ref[c]
        bias = jnp.where(sc > kth, 0.0, jnp.where(sc == kth, jnp.where(key_pos(c) < cut, 0.0, NEG), NEG))
        out = []
        for h in range(Q_HEADS):
            s = _dot(k_ref[0, pl.ds(k0, t), (h // 2) * LANE:(h // 2 + 1) * LANE], qm_ref[h]) + bias
            s_ref[h, c] = s
            out.append(jnp.maximum(mx[h], _fold(s, jnp.maximum)))
        return tuple(out)

    mx = _loop_pairs(nck, att_scores, (jnp.full((SUBLANE, t), NEG, F32),) * Q_HEADS, DSA_TILES_PER_TRIP)
    m = [jnp.max(mx[h], axis=0, keepdims=True) for h in range(Q_HEADS)]

    acc_ref[...] = jnp.zeros(acc_ref.shape, F32)

    def att_values(c, carry):
        for h in range(Q_HEADS):
            acc_ref[h] += _dot(_with_ones(v_ref[0, c, h * hd:(h + 1) * hd, :]), _weights(s_ref[h, c], m[h]))
        return carry

    _loop_pairs(nck, att_values, 0, DSA_TILES_PER_TRIP)
    out = jnp.concatenate([acc_ref[h, :hd] * (1.0 / acc_ref[h, hd:hd + 1])
                           for h in range(Q_HEADS)], axis=0)
    o_ref[0] = out.T.astype(o_ref.dtype)


def _dsa(q_t, k, v_t, iq_t, ik, small_t, bsz, seq_len, t):
    k_top = min(DSA_TOPK_MAX, seq_len // 4)
    assert t >= k_top
    nc = seq_len // t
    qt_spec = pl.BlockSpec((1, 1, Q_WIDTH, t), lambda b, i: (b, i, 0, 0))
    return pl.pallas_call(
        functools.partial(_dsa_kernel, t=t, seq_len=seq_len, k_top=k_top),
        grid=(bsz, nc),
        in_specs=[qt_spec,
                  pl.BlockSpec((1, seq_len, Q_WIDTH), lambda b, i: (b, 0, 0)),
                  pl.BlockSpec((1, nc, Q_WIDTH, t), lambda b, i: (b, 0, 0, 0)),
                  qt_spec,
                  pl.BlockSpec((1, seq_len, LANE), lambda b, i: (b, 0, 0)),
                  pl.BlockSpec((1, 1, SMALL_W, t), lambda b, i: (b, i, 0, 0))],
        out_specs=pl.BlockSpec((1, t, Q_WIDTH), lambda b, i: (b, i, 0)),
        out_shape=jax.ShapeDtypeStruct((bsz, seq_len, Q_WIDTH), BF16),
        scratch_shapes=[pltpu.VMEM((Q_HEADS, LANE, t), BF16), pltpu.VMEM((IDX_HEADS, LANE, t), BF16),
                        pltpu.VMEM((nc, t, t), F32), pltpu.VMEM((nc, t, t), BF16),
                        pltpu.VMEM((Q_HEADS, nc, t, t), F32),
                        pltpu.VMEM((Q_HEADS, HEAD_DIM + ONES_ROWS, t), F32), pltpu.VMEM((SUBLANE, t), F32)],
        compiler_params=_params("parallel", "parallel"),
        name="dsa_attn",
    )(q_t.reshape(bsz, nc, Q_WIDTH, t), k.reshape(bsz, seq_len, -1), v_t.reshape(bsz, nc, Q_WIDTH, t),
      iq_t.reshape(bsz, nc, Q_WIDTH, t), ik.reshape(bsz, seq_len, -1), small_t.reshape(bsz, nc, SMALL_W, t))


def _merge_ffn_kernel(x_ref, yn_ref, yd_ref, gn_ref, gd_ref, wbn_ref, wbd_ref, wo_ref,
                      g_ref, wg_ref, wu_ref, wd_ref, fg_ref, o_ref, *, final):
    merged = gn_ref[...] * _dot(yn_ref[...], wbn_ref[...]) + gd_ref[...] * _dot(yd_ref[...], wbd_ref[...])
    x = x_ref[...] + _dot(merged.astype(BF16), wo_ref[...])
    h = _rmsnorm(x, g_ref[...]).astype(BF16)
    z = (jax.nn.silu(_dot(h, wg_ref[...])) * _dot(h, wu_ref[...])).astype(BF16)
    y = x + _dot(z, wd_ref[...])
    o_ref[...] = _rmsnorm(y, fg_ref[...]) if final else y


def _merge_ffn(xf, y_nsa, y_dsa, gate_nsa, gate_dsa, wbn, wbd, wo, g, wg, wu, wd, final_g, final, tm):
    t, d = xf.shape
    f = wg.shape[1]
    wn = y_nsa.shape[1]
    row = lambda i: (i, 0)
    return pl.pallas_call(
        functools.partial(_merge_ffn_kernel, final=final),
        grid=(t // tm,),
        in_specs=[pl.BlockSpec((tm, d), row), pl.BlockSpec((tm, wn), row), pl.BlockSpec((tm, wn), row),
                  pl.BlockSpec((tm, d), row), pl.BlockSpec((tm, d), row),
                  _const_spec((wn, d)), _const_spec((wn, d)), _const_spec((d, d)),
                  _const_spec((1, d)), _const_spec((d, f)), _const_spec((d, f)), _const_spec((f, d)),
                  _const_spec((1, d))],
        out_specs=pl.BlockSpec((tm, d), row),
        out_shape=jax.ShapeDtypeStruct((t, d), F32),
        compiler_params=_params("parallel"),
        name="merge_ffn",
    )(xf, y_nsa, y_dsa, gate_nsa, gate_dsa, wbn.astype(BF16), wbd.astype(BF16), wo.astype(BF16),
      g.reshape(1, d), wg.astype(BF16), wu.astype(BF16), wd.astype(BF16), final_g.reshape(1, d))


def _rope_tables(seq_len, tm):
    pos = jnp.arange(seq_len, dtype=F32)
    inv_freq = ROPE_THETA ** (-jnp.arange(0, ROPE_DIM, 2, dtype=F32) / ROPE_DIM)
    ang = pos[:, None] * inv_freq[None, :]
    cos, sin = jnp.cos(ang), jnp.sin(ang)
    rest = HEAD_DIM - ROPE_DIM
    one, zero = jnp.ones((seq_len, rest), F32), jnp.zeros((seq_len, rest), F32)
    zh = jnp.zeros((seq_len, ROPE_HALF), F32)
    two = lambda a: jnp.concatenate([a, a], axis=1)
    tile_t = lambda a: a.reshape(seq_len // tm, tm, ROPE_HALF).transpose(0, 2, 1)
    return (two(jnp.concatenate([cos, cos, one], axis=1)),
            two(jnp.concatenate([zh, sin, zero], axis=1)),
            two(jnp.concatenate([-sin, zh, zero], axis=1)),
            tile_t(cos), tile_t(sin))


def _tile(seq_len):
    return min(256, seq_len)


def kernel(x, attn_norm, w_in, cmpk_pos, cmpk_w1, cmpk_b1, cmpk_w2, cmpk_b2, cmpv_pos, cmpv_w1, cmpv_b1,
           cmpv_w2, cmpv_b2, w_branch_nsa, w_branch_dsa, w_out, ffn_norm, w_ffn_gate, w_ffn_up, w_ffn_down,
           final_norm):
    bsz, seq_len, d = x.shape
    depth = w_in.shape[0]
    t = _tile(seq_len)
    tables = _rope_tables(seq_len, t)
    xf = x.reshape(bsz * seq_len, d)
    for l in range(depth):
        (q_nsa, k_cmp, v_cmp, k_slc, v_slc, k_win, v_win, q_dsa, k_dsa, v_dsa, q_idx, k_idx,
         gate_nsa, gate_dsa, small) = _proj(xf, attn_norm[l], *_split_w_in(w_in[l]), tables, seq_len, t)
        kc, vc = _compress(k_cmp, v_cmp, cmpk_pos[l], cmpk_w1[l], cmpk_b1[l], cmpk_w2[l], cmpk_b2[l],
                           cmpv_pos[l], cmpv_w1[l], cmpv_b1[l], cmpv_w2[l], cmpv_b2[l], bsz, seq_len)
        y_nsa = _nsa(q_nsa, kc, vc, k_slc, v_slc, k_win, v_win, small, bsz, seq_len, t)
        y_dsa = _dsa(q_dsa, k_dsa, v_dsa, q_idx, k_idx, small, bsz, seq_len, t)
        xf = _merge_ffn(xf, y_nsa.reshape(bsz * seq_len, -1), y_dsa.reshape(bsz * seq_len, -1),
                        gate_nsa, gate_dsa, w_branch_nsa[l], w_branch_dsa[l], w_out[l],
                        ffn_norm[l], w_ffn_gate[l], w_ffn_up[l], w_ffn_down[l], final_norm,
                        final=(l == depth - 1), tm=t)
    return xf.reshape(bsz, seq_len, d)
```

```python
import functools

import numpy as np
import jax
import jax.numpy as jnp
from jax import lax
from jax.experimental import pallas as pl
from jax.experimental.pallas import tpu as pltpu

HEAD_DIM = 64
ROPE_DIM = HEAD_DIM // 4
ROPE_HALF = ROPE_DIM // 2
ROPE_THETA = 500000.0
NORM_EPS = 1e-6
ATTN_SCALE = HEAD_DIM ** -0.5
Q_SCALE = ATTN_SCALE * float(np.log2(np.e))
Q_HEADS = 8
Q_WIDTH = Q_HEADS * HEAD_DIM
NSA_KV_GROUPS = 2
NSA_GROUP_SIZE = Q_HEADS // NSA_KV_GROUPS
CMP_BLOCK = 32
CMP_STRIDE = 16
CMP_HIDDEN = 2 * HEAD_DIM
SLC_BLOCK = 64
N_SLC = 16
N_LOCAL_BLOCKS = 2
SEL_FORCE_SCORE = 1e4
WINDOW = 512
IDX_HEADS = 8
IDX_DIM = 64
IDX_WEIGHT_SCALE = (IDX_HEADS * IDX_DIM) ** -0.5
DSA_TOPK_MAX = 256
N_GATE = 3 * Q_HEADS
SMALL_W = N_GATE + IDX_HEADS
MAX_SLC_BLOCKS = 32

LANE = 128
SUBLANE = 8
ONES_ROWS = 16
V7X_VMEM_LIMIT = 56 * 1024 * 1024

NEG = -1e30
KEY_NEG_INF = -2139095041
KEY16_NEG_INF = -32641
BF16_ROWS = 16
FINE_STAGES = (10, 3, 4)
DSA_TILES_PER_TRIP = 4
NSA_TILES_PER_TRIP = 4
BF16 = jnp.bfloat16
F32 = jnp.float32

_NT = (((1,), (1,)), ((), ()))


def _dot(a, b):
    return jnp.dot(a, b, preferred_element_type=F32)


def _dot_nt(a, b):
    return lax.dot_general(a, b, _NT, preferred_element_type=F32)


def _const_spec(shape):
    nd = len(shape)
    return pl.BlockSpec(shape, lambda *_: (0,) * nd, pipeline_mode=pl.Buffered(1))


def _rmsnorm(x, g):
    ms = jnp.mean(x * x, axis=-1, keepdims=True)
    return x * lax.rsqrt(ms + NORM_EPS) * g


def _lane_tile(a, reps):
    return a if reps == 1 else jnp.concatenate([a] * reps, axis=1)


def _params(*sem):
    return pltpu.CompilerParams(dimension_semantics=sem, vmem_limit_bytes=V7X_VMEM_LIMIT)


_PROJ_SEGS = (
    ("t", Q_WIDTH, "ropeq", BF16),
    ("n", 128, "rope", F32),
    ("n", 128, "plain", F32),
    ("n", 256, "rope_onehot", BF16),
    ("t", 128, "plain", BF16),
    ("n", 256, "rope", BF16),
    ("t", 128, "plain", BF16),
    ("t", Q_WIDTH, "ropeq", BF16),
    ("n", Q_WIDTH, "rope", BF16),
    ("t", Q_WIDTH, "plain", BF16),
    ("t", IDX_HEADS * IDX_DIM, "rope", BF16),
    ("n", 128, "rope", BF16),
    ("n", 1024, "sig", F32),
    ("n", 1024, "sig", F32),
    ("t", SMALL_W, "small", F32),
)
_PROJ_NW = sum(s[1] for s in _PROJ_SEGS if s[0] == "n")
_PROJ_TW = sum(s[1] for s in _PROJ_SEGS if s[0] == "t")
_PROJ_CHUNK = 512


def _rope_n(y, c, sa, sb):
    pieces = []
    for j in range(y.shape[1] // LANE):
        yj = y[:, j * LANE:(j + 1) * LANE]
        pieces.append(yj * c + pltpu.roll(yj, ROPE_HALF, 1) * sa + pltpu.roll(yj, LANE - ROPE_HALF, 1) * sb)
    return pieces[0] if len(pieces) == 1 else jnp.concatenate(pieces, axis=1)


def _rope_t(y, cos, sin):
    pieces = []
    for h in range(y.shape[0] // HEAD_DIM):
        r0 = h * HEAD_DIM
        x1, x2 = y[r0:r0 + ROPE_HALF], y[r0 + ROPE_HALF:r0 + ROPE_DIM]
        pieces += [x1 * cos - x2 * sin, x2 * cos + x1 * sin, y[r0 + ROPE_DIM:r0 + HEAD_DIM]]
    return jnp.concatenate(pieces, axis=0)


def _proj_kernel(x_ref, g_ref, wn_ref, wt_ref, c_ref, sa_ref, sb_ref, ct_ref, st_ref, *out_refs, npos):
    tm = x_ref.shape[0]
    h = _rmsnorm(x_ref[...], g_ref[...]).astype(BF16)
    c, sa, sb = c_ref[...], sa_ref[...], sb_ref[...]
    cos_t, sin_t = ct_ref[0], st_ref[0]
    ncol = trow = 0
    for (layout, width, kind, _), o_ref in zip(_PROJ_SEGS, out_refs):
        if layout == "t":
            y = _dot_nt(wt_ref[trow:trow + width, :], h)
            trow += width
            if kind in ("rope", "ropeq"):
                y = _rope_t(y, cos_t, sin_t)
            if kind == "ropeq":
                y = y * Q_SCALE
            if kind == "small":
                row = lax.broadcasted_iota(jnp.int32, y.shape, 0)
                y = jnp.where(row < N_GATE, jax.nn.sigmoid(y), y * IDX_WEIGHT_SCALE)
            o_ref[0] = y.astype(o_ref.dtype)
            continue
        for c0 in range(0, width, _PROJ_CHUNK):
            cw = min(_PROJ_CHUNK, width - c0)
            y = _dot(h, wn_ref[:, ncol + c0:ncol + c0 + cw])
            if kind in ("rope", "rope_onehot"):
                y = _rope_n(y, c, sa, sb)
            if kind == "rope_onehot":
                pos = (pl.program_id(0) % npos) * tm + lax.broadcasted_iota(jnp.int32, y.shape, 0)
                lane = lax.broadcasted_iota(jnp.int32, y.shape, 1) & (LANE - 1)
                y = jnp.where(lane == HEAD_DIM + pos // SLC_BLOCK, 1.0, y)
            if kind == "sig":
                y = jax.nn.sigmoid(y)
            o_ref[:, c0:c0 + cw] = y.astype(o_ref.dtype)
        ncol += width


def _split_w_in(w):
    d = w.shape[0]
    kvw = NSA_KV_GROUPS * HEAD_DIM
    sizes = (Q_WIDTH, kvw, kvw, kvw, kvw, kvw, kvw, N_GATE, Q_WIDTH, Q_WIDTH, Q_WIDTH,
             IDX_HEADS * IDX_DIM, IDX_DIM, IDX_HEADS, d, d)
    assert sum(sizes) == w.shape[1]
    offs = np.concatenate([[0], np.cumsum(sizes)])
    (nsa_q, k_cmp, v_cmp, k_slc, v_slc, k_win, v_win, nsa_gate, dsa_q, dsa_k, dsa_v,
     idx_q, idx_k, idx_w, gate_nsa, gate_dsa) = [w[:, offs[i]:offs[i + 1]] for i in range(16)]

    def pad_groups(a):
        a = a.reshape(d, NSA_KV_GROUPS, HEAD_DIM)
        return jnp.concatenate([a, jnp.zeros_like(a)], axis=2).reshape(d, NSA_KV_GROUPS * LANE)

    idx_k_pad = jnp.concatenate([idx_k, jnp.zeros_like(idx_k)], axis=1)
    w_n = jnp.concatenate([k_cmp, v_cmp, pad_groups(k_slc), pad_groups(k_win), dsa_k, idx_k_pad,
                           gate_nsa, gate_dsa], axis=1)
    w_t = jnp.concatenate([nsa_q, v_slc, v_win, dsa_q, dsa_v, idx_q, nsa_gate, idx_w], axis=1).T
    assert w_n.shape[1] == _PROJ_NW and w_t.shape[0] == _PROJ_TW
    return w_n.astype(BF16), w_t.astype(BF16)


def _proj(xf, g, w_n, w_t, tables, seq_len, tm):
    t, d = xf.shape
    npos = seq_len // tm
    row = lambda i: (i, 0)
    tab_n = pl.BlockSpec((tm, LANE), lambda i: (i % npos, 0))
    tab_t = pl.BlockSpec((1, ROPE_HALF, tm), lambda i: (i % npos, 0, 0))
    out_specs, out_shape = [], []
    for layout, width, _, dt in _PROJ_SEGS:
        if layout == "t":
            out_specs.append(pl.BlockSpec((1, width, tm), lambda i: (i, 0, 0)))
            out_shape.append(jax.ShapeDtypeStruct((t // tm, width, tm), dt))
        else:
            out_specs.append(pl.BlockSpec((tm, width), row))
            out_shape.append(jax.ShapeDtypeStruct((t, width), dt))
    return pl.pallas_call(
        functools.partial(_proj_kernel, npos=npos),
        grid=(t // tm,),
        in_specs=[pl.BlockSpec((tm, d), row), _const_spec((1, d)), _const_spec((d, _PROJ_NW)),
                  _const_spec((_PROJ_TW, d)), tab_n, tab_n, tab_n, tab_t, tab_t],
        out_specs=out_specs,
        out_shape=out_shape,
        compiler_params=_params("parallel"),
        name="proj",
    )(xf, g.reshape(1, d), w_n, w_t, *tables)


def _cmp_kernel(k_ref, v_ref, pk_ref, w1k_ref, b1k_ref, w2k_ref, b2k_ref,
                pv_ref, w1v_ref, b1v_ref, w2v_ref, b2v_ref, kc_ref, vc_ref, *, n_cmp):
    nb = kc_ref.shape[1]

    def hidden(x_ref, pos_ref, w1_ref, b1_ref):
        halves = []
        for half in range(CMP_BLOCK // CMP_STRIDE):
            acc = None
            for l in range(CMP_STRIDE):
                j = half * CMP_STRIDE + l
                x = x_ref[0, pl.ds(l, nb, stride=CMP_STRIDE), :] + pos_ref[j:j + 1, :]
                part = _dot(x.astype(BF16), w1_ref[j])
                acc = part if acc is None else acc + part
            halves.append(acc)
        return jax.nn.gelu(halves[0] + pltpu.roll(halves[1], nb - 1, 0) + b1_ref[...]).astype(BF16)

    kc = _dot(hidden(k_ref, pk_ref, w1k_ref, b1k_ref), w2k_ref[...]) + b2k_ref[...]
    rows = lax.broadcasted_iota(jnp.int32, kc.shape, 0)
    kc_ref[0] = jnp.where(rows < n_cmp, kc, 0.0).astype(kc_ref.dtype)
    vc = _dot_nt(w2v_ref[...], hidden(v_ref, pv_ref, w1v_ref, b1v_ref)) + b2v_ref[...]
    cols = lax.broadcasted_iota(jnp.int32, vc.shape, 1)
    vc_ref[0] = jnp.where(cols < n_cmp, vc, 0.0).astype(vc_ref.dtype)


def _compress(k_cmp, v_cmp, pk, w1k, b1k, w2k, b2k, pv, w1v, b1v, w2v, b2v, bsz, seq_len):
    g, hd = NSA_KV_GROUPS, HEAD_DIM
    assert CMP_BLOCK == 2 * CMP_STRIDE
    nb = seq_len // CMP_STRIDE
    n_cmp = (seq_len - CMP_BLOCK) // CMP_STRIDE + 1
    eye = jnp.eye(g, dtype=F32)
    per_group = lambda m: jnp.kron(eye, m)
    tile_g = lambda a: jnp.concatenate([a] * g, axis=-1)

    def first_layer(pos, w1, b1):
        w1_bd = jax.vmap(per_group)(w1.reshape(CMP_BLOCK, hd, CMP_HIDDEN)).astype(BF16)
        return tile_g(pos), w1_bd, tile_g(b1).reshape(1, -1)

    w2k_pad = jnp.concatenate([w2k, jnp.zeros_like(w2k)], axis=1)
    b2k_pad = jnp.concatenate([b2k, jnp.zeros_like(b2k)])
    k_w = first_layer(pk, w1k, b1k) + (per_group(w2k_pad).astype(BF16), tile_g(b2k_pad).reshape(1, -1))
    v_w = first_layer(pv, w1v, b1v) + (per_group(w2v.T).astype(BF16), tile_g(b2v).reshape(-1, 1))
    seq_spec = pl.BlockSpec((1, seq_len, g * hd), lambda b: (b, 0, 0))
    first = [_const_spec((CMP_BLOCK, g * hd)), _const_spec((CMP_BLOCK, g * hd, g * CMP_HIDDEN)),
             _const_spec((1, g * CMP_HIDDEN))]
    return pl.pallas_call(
        functools.partial(_cmp_kernel, n_cmp=n_cmp),
        grid=(bsz,),
        in_specs=[seq_spec, seq_spec] + first + [_const_spec((g * CMP_HIDDEN, g * LANE)), _const_spec((1, g * LANE))]
        + first + [_const_spec((g * hd, g * CMP_HIDDEN)), _const_spec((g * hd, 1))],
        out_specs=[pl.BlockSpec((1, nb, g * LANE), lambda b: (b, 0, 0)),
                   pl.BlockSpec((1, g * hd, nb), lambda b: (b, 0, 0))],
        out_shape=[jax.ShapeDtypeStruct((bsz, nb, g * LANE), BF16),
                   jax.ShapeDtypeStruct((bsz, g * hd, nb), BF16)],
        compiler_params=_params("parallel"),
        name="compress",
    )(k_cmp.reshape(bsz, seq_len, g * hd), v_cmp.reshape(bsz, seq_len, g * hd), *k_w, *v_w)


def _fold(x, op, group=SUBLANE):
    parts = [x[r * group:(r + 1) * group] for r in range(x.shape[0] // group)]
    while len(parts) > 1:
        pairs = [op(parts[k], parts[k + 1]) for k in range(0, len(parts) - 1, 2)]
        parts = pairs + ([parts[-1]] if len(parts) % 2 else [])
    return parts[0]


def _loop_pairs(n, body, init, per_trip=2):
    def steps(start, count, carry):
        for s in range(count):
            carry = body(start + s, carry)
        return carry

    carry = lax.fori_loop(0, n // per_trip, lambda j, c: steps(j * per_trip, per_trip, c), init)
    group = per_trip // 2
    while group:
        carry = lax.cond(n & group != 0, functools.partial(steps, n - n % (2 * group), group), lambda c: c, carry)
        group //= 2
    return carry


def _weights(s, m):
    return jnp.exp2((s - m).astype(BF16))


def _with_ones(v_t):
    return jnp.concatenate([v_t, jnp.ones((ONES_ROWS, v_t.shape[1]), v_t.dtype)], axis=0)


def _nsa_kernel(q_ref, kc_ref, vc_ref, ks_ref, vs_ref, kw_ref, vw_ref, sm_ref, ovl_ref, o_ref,
                qm_ref, ss_ref, sw_ref, as_ref, *, t, seq_len):
    r_heads = NSA_GROUP_SIZE
    gi = pl.program_id(1)
    i = pl.program_id(2)
    t0 = i * t
    rows = r_heads * t
    hd = HEAD_DIM

    for r in range(r_heads):
        qm_ref[0:hd, r * t:(r + 1) * t] = q_ref[0, 0, r * hd:(r + 1) * hd, :]
    qm_ref[hd:, :] = jnp.zeros((LANE - hd, rows), BF16)
    tok = t0 + lax.broadcasted_iota(jnp.int32, (1, t), 1)
    tok_r = _lane_tile(tok, r_heads)

    nb = kc_ref.shape[1]
    s_c = _dot(kc_ref[0], qm_ref[...])
    c_end = lax.broadcasted_iota(jnp.int32, (nb, 1), 0) * CMP_STRIDE + (CMP_BLOCK - 1)
    mask_c = c_end <= tok_r
    s_c = jnp.where(mask_c, s_c, NEG)
    s_c = s_c - jnp.max(s_c, axis=0, keepdims=True)
    p_c = jnp.where(mask_c, jnp.exp2(s_c), 0.0)
    p_c = (p_c / jnp.maximum(jnp.sum(p_c, axis=0, keepdims=True), 1e-30)).astype(BF16)
    o_c = _dot(vc_ref[0], p_c)
    imp4 = _dot(ovl_ref[...], p_c)
    imp = imp4[:, 0:t]
    for r in range(1, r_heads):
        imp = imp + imp4[:, r * t:(r + 1) * t]

    n_slc = seq_len // SLC_BLOCK
    n_sel = min(N_SLC, n_slc)
    blk = lax.broadcasted_iota(jnp.int32, (MAX_SLC_BLOCKS, 1), 0)
    dist = (tok // SLC_BLOCK) - blk
    forced = (blk == 0) | ((dist >= 0) & (dist < N_LOCAL_BLOCKS))
    score = jnp.where(forced, SEL_FORCE_SCORE, imp)
    score = jnp.where(blk * SLC_BLOCK <= tok, score, -jnp.inf)
    groups = [score[v * SUBLANE:(v + 1) * SUBLANE] for v in range(MAX_SLC_BLOCKS // SUBLANE)]
    ranks = [jnp.zeros((SUBLANE, t), F32) for _ in groups]
    sub = lax.broadcasted_iota(jnp.int32, (SUBLANE, 1), 0)
    for j in range(n_slc):
        row = score[j:j + 1]
        for v, sv in enumerate(groups):
            ge = jnp.where(row >= sv, 1.0, 0.0)
            gt = jnp.where(row > sv, 1.0, 0.0)
            if v * SUBLANE > j:
                beats = ge
            elif (v + 1) * SUBLANE <= j + 1:
                beats = gt
            else:
                beats = jnp.where(sub + v * SUBLANE > j, ge, gt)
            ranks[v] = ranks[v] + beats
    selbias = jnp.where(jnp.concatenate(ranks, axis=0) < n_sel, 0.0, NEG).astype(BF16)
    for r in range(r_heads):
        qm_ref[hd:hd + MAX_SLC_BLOCKS, r * t:(r + 1) * t] = selbias
    key_off = lax.broadcasted_iota(jnp.int32, (t, 1), 0)
    q_off = lax.broadcasted_iota(jnp.int32, (1, t), 1)
    causal = _lane_tile(jnp.where(key_off <= q_off, 0.0, NEG), r_heads)
    neg8 = jnp.full((SUBLANE, rows), NEG, F32)

    def key_tile(k_ref_, c):
        return k_ref_[0, pl.ds(pl.multiple_of(c * t, t), t), :]

    def slc_scores(c, mx):
        s = _dot(key_tile(ks_ref, c), qm_ref[...])
        ss_ref[c] = s
        return jnp.maximum(mx, _fold(s, jnp.maximum))

    mx = _loop_pairs(i, slc_scores, neg8, NSA_TILES_PER_TRIP)
    s = _dot(key_tile(ks_ref, i), qm_ref[...]) + causal
    ss_ref[i] = s
    m_s = jnp.max(jnp.maximum(mx, _fold(s, jnp.maximum)), axis=0, keepdims=True)

    n_back = -(-WINDOW // t)
    win_tiles = []
    mxw = neg8
    for back in range(n_back, -1, -1):
        if back:
            lo_edge = back * t - WINDOW
            bias = _lane_tile(jnp.where(key_off - q_off > jnp.where(i >= back, lo_edge, t), 0.0, NEG), r_heads)
        else:
            bias = causal
        c = jnp.maximum(i - back, 0)
        s = _dot(key_tile(kw_ref, c), qm_ref[...]) + bias
        sw_ref[n_back - back] = s
        mxw = jnp.maximum(mxw, _fold(s, jnp.maximum))
        win_tiles.append(c)
    m_w = jnp.max(mxw, axis=0, keepdims=True)

    as_ref[...] = jnp.zeros(as_ref.shape, F32)

    def slc_values(c, carry):
        as_ref[...] += _dot(_with_ones(vs_ref[0, c]), _weights(ss_ref[c], m_s))
        return carry

    _loop_pairs(i + 1, slc_values, 0, NSA_TILES_PER_TRIP)
    o_s = as_ref[:hd] * (1.0 / as_ref[hd:hd + 1])

    o_w = jnp.zeros(as_ref.shape, F32)
    for n, c in enumerate(win_tiles):
        o_w = o_w + _dot(_with_ones(vw_ref[0, c]), _weights(sw_ref[n], m_w))
    o_w = o_w[:hd] * (1.0 / o_w[hd:hd + 1])

    gates = sm_ref[0, 0]

    def gate(branch, r):
        rows_g = [gates[branch * Q_HEADS + g * r_heads + r:branch * Q_HEADS + g * r_heads + r + 1]
                  for g in range(NSA_KV_GROUPS)]
        out = rows_g[0]
        for g in range(1, NSA_KV_GROUPS):
            out = jnp.where(gi == g, rows_g[g], out)
        return out

    heads = []
    for r in range(r_heads):
        sl = slice(r * t, (r + 1) * t)
        heads.append(gate(0, r) * o_c[:, sl] + gate(1, r) * o_s[:, sl] + gate(2, r) * o_w[:, sl])
    o_ref[0] = jnp.concatenate(heads, axis=0).T.astype(o_ref.dtype)


def _overlap_t(seq_len):
    nb = seq_len // CMP_STRIDE
    n_cmp = (seq_len - CMP_BLOCK) // CMP_STRIDE + 1
    n_slc = seq_len // SLC_BLOCK
    c_start = np.arange(n_cmp)[None, :] * CMP_STRIDE
    s_start = np.arange(n_slc)[:, None] * SLC_BLOCK
    ov = np.minimum(c_start + CMP_BLOCK, s_start + SLC_BLOCK) - np.maximum(c_start, s_start)
    ovl = np.zeros((MAX_SLC_BLOCKS, nb), np.float32)
    ovl[:n_slc, :n_cmp] = np.clip(ov, 0, None) / CMP_BLOCK
    return jnp.asarray(ovl, BF16)


def _nsa(q_t, kc, vc_t, ks, vs_t, kw, vw_t, small_t, bsz, seq_len, t):
    g, r, hd = NSA_KV_GROUPS, NSA_GROUP_SIZE, HEAD_DIM
    assert seq_len // SLC_BLOCK <= MAX_SLC_BLOCKS
    nb = seq_len // CMP_STRIDE
    nc = seq_len // t
    rows = r * t
    ovl = _overlap_t(seq_len)
    k_spec = pl.BlockSpec((1, seq_len, LANE), lambda b, gi, i: (b, 0, gi))
    v_spec = pl.BlockSpec((1, nc, hd, t), lambda b, gi, i: (b, 0, gi, 0))
    return pl.pallas_call(
        functools.partial(_nsa_kernel, t=t, seq_len=seq_len),
        grid=(bsz, g, nc),
        in_specs=[pl.BlockSpec((1, 1, r * hd, t), lambda b, gi, i: (b, i, gi, 0)),
                  pl.BlockSpec((1, nb, LANE), lambda b, gi, i: (b, 0, gi)),
                  pl.BlockSpec((1, hd, nb), lambda b, gi, i: (b, gi, 0)),
                  k_spec, v_spec, k_spec, v_spec,
                  pl.BlockSpec((1, 1, SMALL_W, t), lambda b, gi, i: (b, i, 0, 0)),
                  _const_spec(ovl.shape)],
        out_specs=pl.BlockSpec((1, t, r * hd), lambda b, gi, i: (b, i, gi)),
        out_shape=jax.ShapeDtypeStruct((bsz, seq_len, Q_WIDTH), BF16),
        scratch_shapes=[pltpu.VMEM((LANE, rows), BF16), pltpu.VMEM((nc, t, rows), F32),
                        pltpu.VMEM((-(-WINDOW // t) + 1, t, rows), F32),
                        pltpu.VMEM((hd + ONES_ROWS, rows), F32)],
        compiler_params=_params("parallel", "parallel", "parallel"),
        name="nsa_attn",
    )(q_t.reshape(bsz, nc, Q_WIDTH, t), kc, vc_t, ks.reshape(bsz, seq_len, -1), vs_t.reshape(bsz, nc, g * hd, t),
      kw.reshape(bsz, seq_len, -1), vw_t.reshape(bsz, nc, g * hd, t), small_t.reshape(bsz, nc, SMALL_W, t), ovl)


def _dsa_kernel(q_ref, k_ref, v_ref, iq_ref, ik_ref, sm_ref, o_ref,
                qm_ref, iqm_ref, sc_ref, sc16_ref, s_ref, acc_ref, kth_ref, *, t, seq_len, k_top):
    i = pl.program_id(1)
    t0 = i * t
    nck = i + 1
    hd = HEAD_DIM
    tok = t0 + lax.broadcasted_iota(jnp.int32, (1, t), 1)

    zeros = jnp.zeros((hd, t), BF16)
    for h in range(Q_HEADS):
        qh = q_ref[0, 0, h * hd:(h + 1) * hd, :]
        qm_ref[h] = jnp.concatenate([qh, zeros] if h % 2 == 0 else [zeros, qh], axis=0)
        iqm_ref[h] = jnp.concatenate([iq_ref[0, 0, h * hd:(h + 1) * hd, :], zeros], axis=0)
    small = sm_ref[0, 0]

    def key_pos(c):
        start = c * t if isinstance(c, int) else pl.multiple_of(c * t, t)
        return start + lax.broadcasted_iota(jnp.int32, (t, 1), 0)

    def score_body(c, carry):
        ik = ik_ref[0, pl.ds(pl.multiple_of(c * t, t), t), :]
        sc = jnp.zeros((t, t), F32)
        for h in range(IDX_HEADS):
            sc = sc + jnp.maximum(_dot(ik, iqm_ref[h]), 0.0) * small[N_GATE + h:N_GATE + h + 1]
        sc = jnp.where(key_pos(c) <= tok, sc, -jnp.inf)
        sc_ref[c] = sc
        sc16_ref[c] = sc.astype(BF16)
        return carry

    _loop_pairs(nck, score_body, 0, DSA_TILES_PER_TRIP)

    def count(indicator):
        def body(c, acc):
            return acc + _fold(indicator(sc_ref[c], key_pos(c)), jnp.add)
        acc = _loop_pairs(nck, body, jnp.zeros((SUBLANE, t), F32))
        return jnp.sum(acc, axis=0, keepdims=True)

    def count_tiles(tiles_ref, n_tiles, indicator, group):
        acc = None
        for c in range(n_tiles):
            part = _fold(indicator(tiles_ref[c]), jnp.add, group)
            acc = part if acc is None else acc + part
        return acc

    kf = float(k_top)
    zero = jnp.zeros((1, t), jnp.int32)

    def decode(code):
        return lax.bitcast_convert_type(jnp.where(code < 0, code ^ jnp.int32(0x7FFFFFFF), code), F32)

    def search(n_tiles):
        one16, zero16 = jnp.ones((), BF16), jnp.zeros((), BF16)

        def decode16(code16):
            pattern = jnp.where(code16 < 0, code16 ^ jnp.int32(0x7FFF), code16)
            return lax.bitcast_convert_type(lax.shift_left(pattern, 16), F32)

        def reached16(code16):
            val = decode16(code16).astype(BF16)
            acc = count_tiles(sc16_ref, n_tiles, lambda sc: jnp.where(sc >= val, one16, zero16), BF16_ROWS)
            return (jnp.sum(acc.astype(F32), axis=0, keepdims=True) >= kf) | (code16 <= KEY16_NEG_INF)

        def bit16_body(b, base16):
            cand = base16 + lax.shift_left(jnp.int32(1), 14 - b)
            return jnp.where(reached16(cand), cand, base16)

        base16 = jnp.where(reached16(zero), zero, jnp.full((1, t), -(2 ** 15), jnp.int32))
        below = lax.fori_loop(0, 15, bit16_body, base16) - 1
        lo = lax.shift_left(below, 16) + jnp.where(below < 0, 0xFFFF, 0)

        def bit_body(b, state):
            base, n_base = state
            cand = base + lax.shift_left(jnp.int32(1), 16 - b)
            val = decode(cand)
            n = jnp.sum(count_tiles(sc_ref, n_tiles, lambda sc: jnp.where(sc >= val, 1.0, 0.0), SUBLANE),
                        axis=0, keepdims=True)
            n = jnp.where(cand <= KEY_NEG_INF, kf + 1.0, n)
            return jnp.where(n >= kf, cand, base), jnp.where(n >= kf, n, n_base)

        def run_stages(k, first, state):
            stop = first + FINE_STAGES[k]
            state = lax.fori_loop(first, stop, bit_body, state)
            if k + 1 == len(FINE_STAGES):
                return state
            unsettled = jnp.max(jnp.where(state[1] != kf, 1.0, 0.0)) > 0.0
            return lax.cond(unsettled, functools.partial(run_stages, k + 1, stop), lambda s: s, state)

        base, n_ge = run_stages(0, 0, (lo, jnp.full((1, t), kf + 1.0, F32)))
        return decode(jnp.maximum(base, KEY_NEG_INF)), n_ge

    for n_tiles in range(1, seq_len // t + 1):
        @pl.when(nck == n_tiles)
        def _(n_tiles=n_tiles):
            if n_tiles * t <= k_top:
                found = (jnp.full((1, t), -jnp.inf, F32), jnp.full((1, t), kf, F32))
            else:
                found = search(n_tiles)
            for r, v in enumerate(found):
                kth_ref[r:r + 1, :] = v

    kth = kth_ref[0:1, :]
    n_ge = kth_ref[1:2, :]

    tie = (n_ge > kf) & (kth > -jnp.inf)

    def tie_cut():
        room = kf - count(lambda sc, pos: jnp.where(sc > kth, 1.0, 0.0))
        top_bit = seq_len.bit_length() - 1

        def pos_body(b, x):
            cand = x + lax.shift_left(jnp.int32(1), top_bit - b)
            n = count(lambda sc, pos: jnp.where(sc == kth, jnp.where(pos < cand, 1.0, 0.0), 0.0))
            return jnp.where(n <= room, cand, x)

        x = lax.fori_loop(0, top_bit + 1, pos_body, zero)
        return jnp.where(tie, x, seq_len)

    cut = lax.cond(jnp.max(jnp.where(tie, 1.0, 0.0)) > 0.0, tie_cut,
                   lambda: jnp.full((1, t), seq_len, jnp.int32))
    cut = jnp.where(kth > -jnp.inf, cut, 0)

    def att_scores(c, mx):
        k0 = pl.multiple_of(c * t, t)
        sc = sc_ref[c]
        bias = jnp.where(sc > kth, 0.0, jnp.where(sc == kth, jnp.where(key_pos(c) < cut, 0.0, NEG), NEG))
        out = []
        for h in range(Q_HEADS):
            s = _dot(k_ref[0, pl.ds(k0, t), (h // 2) * LANE:(h // 2 + 1) * LANE], qm_ref[h]) + bias
            s_ref[h, c] = s
            out.append(jnp.maximum(mx[h], _fold(s, jnp.maximum)))
        return tuple(out)

    mx = _loop_pairs(nck, att_scores, (jnp.full((SUBLANE, t), NEG, F32),) * Q_HEADS, DSA_TILES_PER_TRIP)
    m = [jnp.max(mx[h], axis=0, keepdims=True) for h in range(Q_HEADS)]

    acc_ref[...] = jnp.zeros(acc_ref.shape, F32)

    def att_values(c, carry):
        for h in range(Q_HEADS):
            acc_ref[h] += _dot(_with_ones(v_ref[0, c, h * hd:(h + 1) * hd, :]), _weights(s_ref[h, c], m[h]))
        return carry

    _loop_pairs(nck, att_values, 0, DSA_TILES_PER_TRIP)
    out = jnp.concatenate([acc_ref[h, :hd] * (1.0 / acc_ref[h, hd:hd + 1])
                           for h in range(Q_HEADS)], axis=0)
    o_ref[0] = out.T.astype(o_ref.dtype)


def _dsa(q_t, k, v_t, iq_t, ik, small_t, bsz, seq_len, t):
    k_top = min(DSA_TOPK_MAX, seq_len // 4)
    assert t >= k_top
    nc = seq_len // t
    qt_spec = pl.BlockSpec((1, 1, Q_WIDTH, t), lambda b, i: (b, i, 0, 0))
    return pl.pallas_call(
        functools.partial(_dsa_kernel, t=t, seq_len=seq_len, k_top=k_top),
        grid=(bsz, nc),
        in_specs=[qt_spec,
                  pl.BlockSpec((1, seq_len, Q_WIDTH), lambda b, i: (b, 0, 0)),
                  pl.BlockSpec((1, nc, Q_WIDTH, t), lambda b, i: (b, 0, 0, 0)),
                  qt_spec,
                  pl.BlockSpec((1, seq_len, LANE), lambda b, i: (b, 0, 0)),
                  pl.BlockSpec((1, 1, SMALL_W, t), lambda b, i: (b, i, 0, 0))],
        out_specs=pl.BlockSpec((1, t, Q_WIDTH), lambda b, i: (b, i, 0)),
        out_shape=jax.ShapeDtypeStruct((bsz, seq_len, Q_WIDTH), BF16),
        scratch_shapes=[pltpu.VMEM((Q_HEADS, LANE, t), BF16), pltpu.VMEM((IDX_HEADS, LANE, t), BF16),
                        pltpu.VMEM((nc, t, t), F32), pltpu.VMEM((nc, t, t), BF16),
                        pltpu.VMEM((Q_HEADS, nc, t, t), F32),
                        pltpu.VMEM((Q_HEADS, HEAD_DIM + ONES_ROWS, t), F32), pltpu.VMEM((SUBLANE, t), F32)],
        compiler_params=_params("parallel", "parallel"),
        name="dsa_attn",
    )(q_t.reshape(bsz, nc, Q_WIDTH, t), k.reshape(bsz, seq_len, -1), v_t.reshape(bsz, nc, Q_WIDTH, t),
      iq_t.reshape(bsz, nc, Q_WIDTH, t), ik.reshape(bsz, seq_len, -1), small_t.reshape(bsz, nc, SMALL_W, t))


def _merge_ffn_kernel(x_ref, yn_ref, yd_ref, gn_ref, gd_ref, wbn_ref, wbd_ref, wo_ref,
                      g_ref, wg_ref, wu_ref, wd_ref, fg_ref, o_ref, *, final):
    merged = gn_ref[...] * _dot(yn_ref[...], wbn_ref[...]) + gd_ref[...] * _dot(yd_ref[...], wbd_ref[...])
    x = x_ref[...] + _dot(merged.astype(BF16), wo_ref[...])
    h = _rmsnorm(x, g_ref[...]).astype(BF16)
    z = (jax.nn.silu(_dot(h, wg_ref[...])) * _dot(h, wu_ref[...])).astype(BF16)
    y = x + _dot(z, wd_ref[...])
    o_ref[...] = _rmsnorm(y, fg_ref[...]) if final else y


def _merge_ffn(xf, y_nsa, y_dsa, gate_nsa, gate_dsa, wbn, wbd, wo, g, wg, wu, wd, final_g, final, tm):
    t, d = xf.shape
    f = wg.shape[1]
    wn = y_nsa.shape[1]
    row = lambda i: (i, 0)
    return pl.pallas_call(
        functools.partial(_merge_ffn_kernel, final=final),
        grid=(t // tm,),
        in_specs=[pl.BlockSpec((tm, d), row), pl.BlockSpec((tm, wn), row), pl.BlockSpec((tm, wn), row),
                  pl.BlockSpec((tm, d), row), pl.BlockSpec((tm, d), row),
                  _const_spec((wn, d)), _const_spec((wn, d)), _const_spec((d, d)),
                  _const_spec((1, d)), _const_spec((d, f)), _const_spec((d, f)), _const_spec((f, d)),
                  _const_spec((1, d))],
        out_specs=pl.BlockSpec((tm, d), row),
        out_shape=jax.ShapeDtypeStruct((t, d), F32),
        compiler_params=_params("parallel"),
        name="merge_ffn",
    )(xf, y_nsa, y_dsa, gate_nsa, gate_dsa, wbn.astype(BF16), wbd.astype(BF16), wo.astype(BF16),
      g.reshape(1, d), wg.astype(BF16), wu.astype(BF16), wd.astype(BF16), final_g.reshape(1, d))


def _rope_tables(seq_len, tm):
    pos = jnp.arange(seq_len, dtype=F32)
    inv_freq = ROPE_THETA ** (-jnp.arange(0, ROPE_DIM, 2, dtype=F32) / ROPE_DIM)
    ang = pos[:, None] * inv_freq[None, :]
    cos, sin = jnp.cos(ang), jnp.sin(ang)
    rest = HEAD_DIM - ROPE_DIM
    one, zero = jnp.ones((seq_len, rest), F32), jnp.zeros((seq_len, rest), F32)
    zh = jnp.zeros((seq_len, ROPE_HALF), F32)
    two = lambda a: jnp.concatenate([a, a], axis=1)
    tile_t = lambda a: a.reshape(seq_len // tm, tm, ROPE_HALF).transpose(0, 2, 1)
    return (two(jnp.concatenate([cos, cos, one], axis=1)),
            two(jnp.concatenate([zh, sin, zero], axis=1)),
            two(jnp.concatenate([-sin, zh, zero], axis=1)),
            tile_t(cos), tile_t(sin))


def _tile(seq_len):
    return min(256, seq_len)


def kernel(x, attn_norm, w_in, cmpk_pos, cmpk_w1, cmpk_b1, cmpk_w2, cmpk_b2, cmpv_pos, cmpv_w1, cmpv_b1,
           cmpv_w2, cmpv_b2, w_branch_nsa, w_branch_dsa, w_out, ffn_norm, w_ffn_gate, w_ffn_up, w_ffn_down,
           final_norm):
    bsz, seq_len, d = x.shape
    depth = w_in.shape[0]
    t = _tile(seq_len)
    tables = _rope_tables(seq_len, t)
    xf = x.reshape(bsz * seq_len, d)
    for l in range(depth):
        (q_nsa, k_cmp, v_cmp, k_slc, v_slc, k_win, v_win, q_dsa, k_dsa, v_dsa, q_idx, k_idx,
         gate_nsa, gate_dsa, small) = _proj(xf, attn_norm[l], *_split_w_in(w_in[l]), tables, seq_len, t)
        kc, vc = _compress(k_cmp, v_cmp, cmpk_pos[l], cmpk_w1[l], cmpk_b1[l], cmpk_w2[l], cmpk_b2[l],
                           cmpv_pos[l], cmpv_w1[l], cmpv_b1[l], cmpv_w2[l], cmpv_b2[l], bsz, seq_len)
        y_nsa = _nsa(q_nsa, kc, vc, k_slc, v_slc, k_win, v_win, small, bsz, seq_len, t)
        y_dsa = _dsa(q_dsa, k_dsa, v_dsa, q_idx, k_idx, small, bsz, seq_len, t)
        xf = _merge_ffn(xf, y_nsa.reshape(bsz * seq_len, -1), y_dsa.reshape(bsz * seq_len, -1),
                        gate_nsa, gate_dsa, w_branch_nsa[l], w_branch_dsa[l], w_out[l],
                        ffn_norm[l], w_ffn_gate[l], w_ffn_up[l], w_ffn_down[l], final_norm,
                        final=(l == depth - 1), tm=t)
    return xf.reshape(bsz, seq_len, d)
```
